```python
import jax, jax.numpy as jnp
from jax import lax
import numpy as np

D_MODEL = 1024
BATCH = 4
SEQ = 8192
DEPTH = 1
DEC_BATCH = 8
DEC_SEQ = 64
PAST_LEN = 1024

CHUNK = 64
N_MEM = 256
N_BRANCHES = 3
A_HEADS = 6
A_KV_HEADS = 2
HEAD_DIM = 128
ROT_DIM = HEAD_DIM // 4
IDX_HEADS = 8
IDX_DIM = 64
IDX_ROT_DIM = IDX_DIM // 4
TOPK_MAX = 256
Q_BLOCK = 128
B_HEADS = 6
B_KEY_DIM = 128
B_VAL_DIM = 128
C_HEADS = 4
C_HEAD_DIM = 128

A_WIDTH = A_HEADS * HEAD_DIM
B_WIDTH = B_HEADS * B_VAL_DIM
C_WIDTH = C_HEADS * C_HEAD_DIM
D_FF = 4 * D_MODEL
ROPE_THETA = 500000.0
EPS = 1e-6

IN_WIDTHS = (
    A_HEADS * HEAD_DIM,
    A_KV_HEADS * HEAD_DIM,
    A_KV_HEADS * HEAD_DIM,
    IDX_HEADS * IDX_DIM,
    IDX_DIM,
    IDX_HEADS,
    B_HEADS * B_KEY_DIM,
    B_HEADS * B_KEY_DIM,
    B_HEADS * B_VAL_DIM,
    B_HEADS * B_VAL_DIM,
    C_WIDTH,
    N_BRANCHES * D_MODEL,
)
IN_WIDTH = sum(IN_WIDTHS)

kernel_name = "dsa_hgrn2_memory_gated_streaming_encoder_step"


def rms_norm(x, gain):
    xf = x.astype(jnp.float32)
    y = xf * lax.rsqrt(jnp.mean(xf * xf, axis=-1, keepdims=True) + EPS)
    return (y * gain.astype(jnp.float32)).astype(x.dtype)


def partial_rope(x, pos, rot_dim):
    half = rot_dim // 2
    inv_freq = ROPE_THETA ** (-jnp.arange(half, dtype=jnp.float32) / half)
    ang = pos.astype(jnp.float32)[:, None] * inv_freq[None, :]
    cos = jnp.cos(ang)[:, None, :]
    sin = jnp.sin(ang)[:, None, :]
    xr = x[..., :rot_dim].astype(jnp.float32)
    x1, x2 = xr[..., :half], xr[..., half:]
    rot = jnp.concatenate([x1 * cos - x2 * sin, x2 * cos + x1 * sin], axis=-1)
    return jnp.concatenate([rot.astype(x.dtype), x[..., rot_dim:]], axis=-1)


def split_projection(h, w_in, pos):
    bsz, t = h.shape[:2]
    points = [int(s) for s in np.cumsum(IN_WIDTHS)[:-1]]
    a_q, a_k, a_v, i_q, i_k, i_w, b_q, b_f, b_i, b_g, c_q, gates = jnp.split(h @ w_in, points, axis=-1)
    a_q = partial_rope(a_q.reshape(bsz, t, A_HEADS, HEAD_DIM), pos, ROT_DIM)
    a_k = partial_rope(a_k.reshape(bsz, t, A_KV_HEADS, HEAD_DIM), pos, ROT_DIM)
    a_v = a_v.reshape(bsz, t, A_KV_HEADS, HEAD_DIM)
    i_q = partial_rope(i_q.reshape(bsz, t, IDX_HEADS, IDX_DIM), pos, IDX_ROT_DIM)
    i_k = partial_rope(i_k.reshape(bsz, t, 1, IDX_DIM), pos, IDX_ROT_DIM)[:, :, 0]
    c_q = c_q.reshape(bsz, t, C_HEADS, C_HEAD_DIM)
    return a_q, a_k, a_v, i_q, i_k, i_w, b_q, b_f, b_i, b_g, c_q, gates


def dsa_block(q, iq, iw, limit, k, v, ik, topk):
    n_keys = k.shape[1]
    key_ok = jnp.arange(n_keys, dtype=jnp.int32)[None, :] < limit[:, None]
    dots = jnp.einsum('bqhd,bsd->bqhs', iq, ik).astype(jnp.float32) * (IDX_DIM ** -0.5)
    score = jnp.einsum('bqhs,bqh->bqs', jax.nn.relu(dots), iw.astype(jnp.float32)) * (IDX_HEADS ** -0.5)
    score = jnp.where(key_ok[None], score, -jnp.inf)
    _, sel = lax.top_k(score, topk)
    valid = sel < limit[None, :, None]
    take = jax.vmap(lambda arr, ix: arr[ix])
    k_sel = take(k, sel)
    v_sel = take(v, sel)
    bsz, nq = q.shape[:2]
    qg = q.reshape(bsz, nq, A_KV_HEADS, A_HEADS // A_KV_HEADS, HEAD_DIM)
    s = jnp.einsum('bqkgd,bqnkd->bqkgn', qg, k_sel).astype(jnp.float32) * (HEAD_DIM ** -0.5)
    s = jnp.where(valid[:, :, None, None, :], s, -jnp.inf)
    p = jax.nn.softmax(s, axis=-1).astype(v.dtype)
    o = jnp.einsum('bqkgn,bqnkd->bqkgd', p, v_sel)
    return o.reshape(bsz, nq, A_WIDTH)


def dsa_prompt(q, iq, iw, k, v, ik):
    bsz, t = q.shape[:2]
    topk = min(TOPK_MAX, t // 4)
    nblk = t // Q_BLOCK
    pos = jnp.arange(t, dtype=jnp.int32)
    limit = (pos // CHUNK + 1) * CHUNK

    def blocks(a):
        return jnp.moveaxis(a.reshape(bsz, nblk, Q_BLOCK, *a.shape[2:]), 1, 0)

    xs = (blocks(q), blocks(iq), blocks(iw), limit.reshape(nblk, Q_BLOCK))
    out = lax.map(lambda blk: dsa_block(blk[0], blk[1], blk[2], blk[3], k, v, ik, topk), xs)
    return jnp.moveaxis(out, 0, 1).reshape(bsz, t, A_WIDTH)


def hgrn2_gates(b_q, b_f, b_i, lower_bound):
    bsz, t = b_q.shape[:2]
    shp = (bsz, t, B_HEADS, B_KEY_DIM)
    f = lower_bound + (1.0 - lower_bound) * jax.nn.sigmoid(b_f.astype(jnp.float32))
    q = jax.nn.silu(b_q.astype(jnp.float32)).reshape(shp)
    k = (1.0 - f).reshape(shp)
    log_f = jnp.log(f).reshape(shp)
    v = b_i.astype(jnp.float32).reshape(bsz, t, B_HEADS, B_VAL_DIM)
    return q, k, v, log_f


def hgrn2_chunk(state, q, k, v, log_f):
    c = q.shape[1]
    b = jnp.cumsum(log_f, axis=1)
    causal = jnp.tril(jnp.ones((c, c), dtype=bool))[None, :, :, None, None]
    decay = jnp.exp(jnp.where(causal, b[:, :, None] - b[:, None, :], -jnp.inf))
    scores = jnp.einsum('btshd,bshd->bhts', q[:, :, None] * decay, k)
    o = jnp.einsum('bhts,bshv->bthv', scores, v) + jnp.einsum('bthd,bhdv->bthv', q * jnp.exp(b), state)
    b_end = b[:, -1]
    new_state = jnp.exp(b_end)[..., None] * state + jnp.einsum(
        'bshd,bshv->bhdv', k * jnp.exp(b_end[:, None] - b), v)
    return new_state, o


def hgrn2_prompt(q, k, v, log_f):
    bsz, t = q.shape[:2]
    n = t // CHUNK

    def chunks(a):
        return jnp.moveaxis(a.reshape(bsz, n, CHUNK, *a.shape[2:]), 1, 0)

    s0 = jnp.zeros((bsz, B_HEADS, B_KEY_DIM, B_VAL_DIM), jnp.float32)
    s_final, o = lax.scan(lambda s, xs: hgrn2_chunk(s, xs[0], xs[1], xs[2], xs[3]), s0,
                          (chunks(q), chunks(k), chunks(v), chunks(log_f)))
    return s_final, jnp.moveaxis(o, 0, 1).reshape(bsz, t, B_HEADS, B_VAL_DIM)


def hgrn2_readout(o, b_g, gain):
    o = rms_norm(o, gain)
    return (o.reshape(*o.shape[:2], B_WIDTH) * jax.nn.silu(b_g.astype(jnp.float32))).astype(b_g.dtype)


def memory_kv(mem, gain, w_mem_kv):
    bsz, n = mem.shape[:2]
    mk, mv = jnp.split(rms_norm(mem, gain) @ w_mem_kv, 2, axis=-1)
    return mk.reshape(bsz, n, C_HEADS, C_HEAD_DIM), mv.reshape(bsz, n, C_HEADS, C_HEAD_DIM)


def memory_attend(q, mem_k, mem_v):
    s = jnp.einsum('bthd,bnhd->bhtn', q, mem_k.astype(q.dtype)).astype(jnp.float32) * (C_HEAD_DIM ** -0.5)
    p = jax.nn.softmax(s, axis=-1).astype(q.dtype)
    o = jnp.einsum('bhtn,bnhd->bthd', p, mem_v.astype(q.dtype))
    return o.reshape(*q.shape[:2], C_WIDTH)


def merge_and_ffn(x, y_a, y_b, y_c, gates, lp):
    g_a, g_b, g_c = jnp.split(jax.nn.sigmoid(gates), N_BRANCHES, axis=-1)
    merged = g_a * (y_a @ lp['w_out_a']) + g_b * (y_b @ lp['w_out_b']) + g_c * (y_c @ lp['w_out_c'])
    x = x + rms_norm(merged @ lp['w_out'], lp['post_mix_gain'])
    u = jax.nn.relu(rms_norm(x, lp['pre_ffn_gain']) @ lp['w_up'])
    return x + rms_norm((u * u) @ lp['w_down'], lp['post_ffn_gain'])


def layer_prompt(x, mem, pos, lower_bound, lp):
    h = rms_norm(x, lp['pre_mix_gain'])
    a_q, a_k, a_v, i_q, i_k, i_w, b_q, b_f, b_i, b_g, c_q, gates = split_projection(h, lp['w_in'], pos)
    y_a = dsa_prompt(a_q, i_q, i_w, a_k, a_v, i_k)
    q, k, v, log_f = hgrn2_gates(b_q, b_f, b_i, lower_bound)
    s_new, o = hgrn2_prompt(q, k, v, log_f)
    y_b = hgrn2_readout(o, b_g, lp['hgrn_norm_gain'])
    mem_k, mem_v = memory_kv(mem, lp['mem_norm_gain'], lp['w_mem_kv'])
    y_c = memory_attend(c_q, mem_k, mem_v)
    x = merge_and_ffn(x, y_a, y_b, y_c, gates, lp)
    return x, (a_k, a_v, i_k, s_new, mem_k, mem_v)


def layer_sample(x, pos, c_k, c_v, c_ik, s_h, c_mk, c_mv, lower_bound, lp):
    h = rms_norm(x, lp['pre_mix_gain'])
    a_q, a_k, a_v, i_q, i_k, i_w, b_q, b_f, b_i, b_g, c_q, gates = split_projection(h, lp['w_in'], pos)
    k_all = jnp.concatenate([c_k.astype(a_k.dtype), a_k], axis=1)
    v_all = jnp.concatenate([c_v.astype(a_v.dtype), a_v], axis=1)
    ik_all = jnp.concatenate([c_ik.astype(i_k.dtype), i_k], axis=1)
    n_keys = k_all.shape[1]
    limit = jnp.full((x.shape[1],), n_keys, dtype=jnp.int32)
    y_a = dsa_block(a_q, i_q, i_w, limit, k_all, v_all, ik_all, min(TOPK_MAX, n_keys // 4))
    q, k, v, log_f = hgrn2_gates(b_q, b_f, b_i, lower_bound)
    s_new, o = hgrn2_chunk(s_h.astype(jnp.float32), q, k, v, log_f)
    y_b = hgrn2_readout(o, b_g, lp['hgrn_norm_gain'])
    y_c = memory_attend(c_q, c_mk, c_mv)
    x = merge_and_ffn(x, y_a, y_b, y_c, gates, lp)
    return x, (a_k, a_v, i_k, s_new)


def setup_inputs(seed: int = 0) -> dict:
    key = jax.random.key(seed)
    ks = jax.random.split(key, 24)

    def nrm(k, shape, scale=1.0):
        return jax.random.normal(k, shape, jnp.float32) * scale

    def gain(k, shape):
        return 1.0 + 0.05 * jax.random.normal(k, shape, jnp.float32)

    return {
        'x_prompt': nrm(ks[0], (BATCH, SEQ, D_MODEL)),
        'x_sample': nrm(ks[1], (DEC_BATCH, DEC_SEQ, D_MODEL)),
        'cache_k': nrm(ks[2], (DEPTH, DEC_BATCH, PAST_LEN, A_KV_HEADS, HEAD_DIM)),
        'cache_v': nrm(ks[3], (DEPTH, DEC_BATCH, PAST_LEN, A_KV_HEADS, HEAD_DIM)),
        'cache_idx_k': nrm(ks[4], (DEPTH, DEC_BATCH, PAST_LEN, IDX_DIM)),
        'state_hgrn': nrm(ks[5], (DEPTH, DEC_BATCH, B_HEADS, B_KEY_DIM, B_VAL_DIM)),
        'cache_mem_k': nrm(ks[6], (DEPTH, DEC_BATCH, N_MEM, C_HEADS, C_HEAD_DIM)),
        'cache_mem_v': nrm(ks[7], (DEPTH, DEC_BATCH, N_MEM, C_HEADS, C_HEAD_DIM)),
        'mem_prompt': nrm(ks[8], (BATCH, N_MEM, D_MODEL)),
        'w_in': nrm(ks[9], (DEPTH, D_MODEL, IN_WIDTH), D_MODEL ** -0.5),
        'w_mem_kv': nrm(ks[10], (DEPTH, D_MODEL, 2 * C_WIDTH), D_MODEL ** -0.5),
        'mem_norm_gain': gain(ks[11], (DEPTH, D_MODEL)),
        'hgrn_lb_logits': nrm(ks[12], (DEPTH + 1, B_HEADS * B_KEY_DIM), 0.5),
        'hgrn_norm_gain': gain(ks[13], (DEPTH, B_HEADS, B_VAL_DIM)),
        'w_out_a': nrm(ks[14], (DEPTH, A_WIDTH, D_MODEL), A_WIDTH ** -0.5),
        'w_out_b': nrm(ks[15], (DEPTH, B_WIDTH, D_MODEL), B_WIDTH ** -0.5),
        'w_out_c': nrm(ks[16], (DEPTH, C_WIDTH, D_MODEL), C_WIDTH ** -0.5),
        'w_out': nrm(ks[17], (DEPTH, D_MODEL, D_MODEL), D_MODEL ** -0.5),
        'pre_mix_gain': gain(ks[18], (DEPTH, D_MODEL)),
        'post_mix_gain': gain(ks[19], (DEPTH, D_MODEL)),
        'pre_ffn_gain': gain(ks[20], (DEPTH, D_MODEL)),
        'post_ffn_gain': gain(ks[21], (DEPTH, D_MODEL)),
        'w_up': nrm(ks[22], (DEPTH, D_MODEL, D_FF), D_MODEL ** -0.5),
        'w_down': nrm(ks[23], (DEPTH, D_FF, D_MODEL), D_FF ** -0.5),
    }


def reference(x_prompt, x_sample, cache_k, cache_v, cache_idx_k, state_hgrn, cache_mem_k, cache_mem_v,
              mem_prompt, w_in, w_mem_kv, mem_norm_gain, hgrn_lb_logits, hgrn_norm_gain, w_out_a, w_out_b,
              w_out_c, w_out, pre_mix_gain, post_mix_gain, pre_ffn_gain, post_ffn_gain, w_up, w_down):
    lower_bounds = jnp.cumsum(jax.nn.softmax(hgrn_lb_logits.astype(jnp.float32), axis=0), axis=0)
    pos_p = jnp.arange(x_prompt.shape[1], dtype=jnp.int32)
    pos_s = cache_k.shape[2] + jnp.arange(x_sample.shape[1], dtype=jnp.int32)
    xp, xs = x_prompt, x_sample
    outs_p, outs_s = [], []
    for l in range(DEPTH):
        lp = {
            'w_in': w_in[l], 'w_mem_kv': w_mem_kv[l], 'mem_norm_gain': mem_norm_gain[l],
            'hgrn_norm_gain': hgrn_norm_gain[l], 'w_out_a': w_out_a[l], 'w_out_b': w_out_b[l],
            'w_out_c': w_out_c[l], 'w_out': w_out[l], 'pre_mix_gain': pre_mix_gain[l],
            'post_mix_gain': post_mix_gain[l], 'pre_ffn_gain': pre_ffn_gain[l],
            'post_ffn_gain': post_ffn_gain[l], 'w_up': w_up[l], 'w_down': w_down[l],
        }
        xp, st_p = layer_prompt(xp, mem_prompt, pos_p, lower_bounds[l], lp)
        xs, st_s = layer_sample(xs, pos_s, cache_k[l], cache_v[l], cache_idx_k[l], state_hgrn[l],
                                cache_mem_k[l], cache_mem_v[l], lower_bounds[l], lp)
        outs_p.append(st_p)
        outs_s.append(st_s)
    new_k_p, new_v_p, new_ik_p, new_s_p, new_mk_p, new_mv_p = [jnp.stack(a) for a in zip(*outs_p)]
    new_k_s, new_v_s, new_ik_s, new_s_s = [jnp.stack(a) for a in zip(*outs_s)]
    return (xp, xs, new_k_p, new_v_p, new_ik_p, new_s_p, new_mk_p, new_mv_p,
            new_k_s, new_v_s, new_ik_s, new_s_s)
```

```python
import functools

import jax
import jax.numpy as jnp
import numpy as np
from jax import lax
from jax.experimental import pallas as pl
from jax.experimental.pallas import tpu as pltpu

CHUNK = 64
N_BRANCHES = 3
A_HEADS, A_KV_HEADS, HEAD_DIM = 6, 2, 128
ROT_DIM = HEAD_DIM // 4
IDX_HEADS, IDX_DIM = 8, 64
IDX_ROT_DIM = IDX_DIM // 4
TOPK_MAX = 256
B_HEADS, B_KEY_DIM, B_VAL_DIM = 6, 128, 128
C_HEADS, C_HEAD_DIM = 4, 128
ROPE_THETA = 500000.0
EPS = 1e-6

A_WIDTH = A_HEADS * HEAD_DIM
KV_WIDTH = A_KV_HEADS * HEAD_DIM
IQ_WIDTH = IDX_HEADS * IDX_DIM
B_WIDTH = B_HEADS * B_KEY_DIM
C_WIDTH = C_HEADS * C_HEAD_DIM

LANES = 128
SUBLANES = 8
VMEM_LIMIT_BYTES = 56 * 1024 * 1024

MXU_DTYPE = jnp.bfloat16
F32 = jnp.float32
I32 = jnp.int32

IW_LANE = 96
NEG_BIG = -1e30
INT_MIN = -(2 ** 31)
KEY_NEG_INF = INT_MIN + 0x00800000

OFF_AQ = 0
OFF_AK = OFF_AQ + A_WIDTH
OFF_AV = OFF_AK + KV_WIDTH
OFF_IQ = OFF_AV + KV_WIDTH
OFF_IKW = OFF_IQ + IQ_WIDTH
OFF_BQ = OFF_IKW + LANES
OFF_BF = OFF_BQ + B_WIDTH
OFF_BI = OFF_BF + B_WIDTH
OFF_BG = OFF_BI + B_WIDTH
OFF_CQ = OFF_BG + B_WIDTH
OFF_GATES = OFF_CQ + C_WIDTH


def _sigmoid(x):
    return 1.0 / (1.0 + jnp.exp(-x))


def _dot(a, b):
    return jnp.dot(a, b, preferred_element_type=F32)


def _dot_nt(a, b):
    return lax.dot_general(a, b, (((1,), (1,)), ((), ())), preferred_element_type=F32)


def _rms(x, gain):
    return x * lax.rsqrt(jnp.mean(x * x, axis=-1, keepdims=True) + EPS) * gain


def _resident(shape):
    zeros = (0,) * len(shape)
    return pl.BlockSpec(shape, lambda *_: zeros, pipeline_mode=pl.Buffered(1))


def _rope(y, cos, sin_up, sin_down, half):
    return y * cos + pltpu.roll(y, half, 1) * sin_up + pltpu.roll(y, LANES - half, 1) * sin_down


def _in_proj_kernel(x_ref, gain_ref, lb_ref, w_ref, ca_ref, sau_ref, sad_ref, ci_ref, siu_ref, sid_ref,
                    aq_ref, k32_ref, v32_ref, k16_ref, v16_ref, iq_ref, ikw_ref, ika_ref, ikb_ref,
                    hq_ref, hk_ref, hlf_ref, hv_ref, hg_ref, cq_ref, gt_ref):
    hb = _rms(x_ref[...], gain_ref[...]).astype(MXU_DTYPE)

    def proj(c0, width):
        return _dot(hb, w_ref[:, c0:c0 + width])

    ca, sau, sad = ca_ref[...], sau_ref[...], sad_ref[...]
    ci, siu, sid = ci_ref[...], siu_ref[...], sid_ref[...]

    y = proj(OFF_AQ, A_WIDTH)
    for h in range(A_HEADS):
        sl = slice(h * LANES, (h + 1) * LANES)
        aq_ref[:, sl] = (_rope(y[:, sl], ca, sau, sad, ROT_DIM // 2) * (HEAD_DIM ** -0.5)).astype(MXU_DTYPE)

    y = proj(OFF_AK, KV_WIDTH)
    for h in range(A_KV_HEADS):
        sl = slice(h * LANES, (h + 1) * LANES)
        r = _rope(y[:, sl], ca, sau, sad, ROT_DIM // 2)
        k32_ref[:, sl] = r
        k16_ref[:, sl] = r.astype(MXU_DTYPE)

    y = proj(OFF_AV, KV_WIDTH)
    v32_ref[...] = y
    v16_ref[...] = y.astype(MXU_DTYPE)

    y = proj(OFF_IQ, IQ_WIDTH)
    for j in range(IQ_WIDTH // LANES):
        sl = slice(j * LANES, (j + 1) * LANES)
        iq_ref[:, sl] = _rope(y[:, sl], ci, siu, sid, IDX_ROT_DIM // 2).astype(MXU_DTYPE)

    r = _rope(proj(OFF_IKW, LANES), ci, siu, sid, IDX_ROT_DIM // 2)
    lane = lax.broadcasted_iota(I32, r.shape, 1)
    ikw_ref[...] = jnp.where(lane < IDX_DIM, r, r * ((IDX_DIM ** -0.5) * (IDX_HEADS ** -0.5)))
    ik_low = jnp.where(lane < IDX_DIM, r, 0.0)
    ika_ref[...] = ik_low.astype(MXU_DTYPE)
    ikb_ref[...] = pltpu.roll(ik_low, IDX_DIM, 1).astype(MXU_DTYPE)

    y = proj(OFF_BQ, B_WIDTH)
    hq_ref[...] = (y * _sigmoid(y)).astype(MXU_DTYPE)
    lb = lb_ref[...]
    f = lb + (1.0 - lb) * _sigmoid(proj(OFF_BF, B_WIDTH))
    hk_ref[...] = (1.0 - f).astype(MXU_DTYPE)
    hlf_ref[...] = jnp.log(f)
    hv_ref[...] = proj(OFF_BI, B_WIDTH).astype(MXU_DTYPE)
    y = proj(OFF_BG, B_WIDTH)
    hg_ref[...] = (y * _sigmoid(y)).astype(MXU_DTYPE)

    cq_ref[...] = (proj(OFF_CQ, C_WIDTH) * (C_HEAD_DIM ** -0.5)).astype(MXU_DTYPE)

    n_gate = gt_ref.shape[1]
    for c0 in range(0, n_gate, B_WIDTH):
        gt_ref[:, c0:c0 + B_WIDTH] = _sigmoid(proj(OFF_GATES + c0, B_WIDTH)).astype(MXU_DTYPE)


def _rope_tables(pos, rot_dim, head_dim):
    half = rot_dim // 2
    inv_freq = ROPE_THETA ** (-jnp.arange(half, dtype=F32) / half)
    ang = pos.astype(F32)[:, None] * inv_freq[None, :]
    cos, sin = jnp.cos(ang), jnp.sin(ang)
    lane = np.arange(LANES) % head_dim
    idx = lane % half
    first, second = lane < half, (lane >= half) & (lane < rot_dim)
    cos_t = jnp.where(first | second, cos[:, idx], 1.0)
    sin_up = jnp.where(second, sin[:, idx], 0.0)
    sin_down = jnp.where(first, -sin[:, idx], 0.0)
    return cos_t, sin_up, sin_down


def _in_proj(x, pos, gain, lower_bound, w_packed, row_tile):
    bsz, t, d = x.shape
    n = bsz * t
    tm = row_tile
    assert n % tm == 0 and (t % tm == 0 or tm % t == 0)
    tables = _rope_tables(pos, ROT_DIM, HEAD_DIM) + _rope_tables(pos, IDX_ROT_DIM, IDX_DIM)
    table_rows = max(t, tm)
    tables = [jnp.tile(tb, (table_rows // t, 1)) for tb in tables]
    n_table_blocks = table_rows // tm
    n_gate = N_BRANCHES * d
    w_width = w_packed.shape[1]

    def rows(width):
        return pl.BlockSpec((tm, width), lambda i: (i, 0))

    table_spec = pl.BlockSpec((tm, LANES), lambda i: (i % n_table_blocks, 0))
    widths_dtypes = [
        (A_WIDTH, MXU_DTYPE), (KV_WIDTH, F32), (KV_WIDTH, F32), (KV_WIDTH, MXU_DTYPE), (KV_WIDTH, MXU_DTYPE),
        (IQ_WIDTH, MXU_DTYPE), (LANES, F32), (LANES, MXU_DTYPE), (LANES, MXU_DTYPE),
        (B_WIDTH, MXU_DTYPE), (B_WIDTH, MXU_DTYPE), (B_WIDTH, F32), (B_WIDTH, MXU_DTYPE), (B_WIDTH, MXU_DTYPE),
        (C_WIDTH, MXU_DTYPE), (n_gate, MXU_DTYPE)]
    outs = pl.pallas_call(
        _in_proj_kernel,
        grid=(n // tm,),
        in_specs=[rows(d), _resident((1, d)), _resident((1, B_WIDTH)), _resident((d, w_width))] + [table_spec] * 6,
        out_specs=[rows(w) for w, _ in widths_dtypes],
        out_shape=[jax.ShapeDtypeStruct((n, w), dt) for w, dt in widths_dtypes],
        compiler_params=pltpu.CompilerParams(dimension_semantics=("arbitrary",), vmem_limit_bytes=VMEM_LIMIT_BYTES),
        name="in_proj",
    )(x.reshape(n, d), gain.reshape(1, d), lower_bound.reshape(1, B_WIDTH), w_packed, *tables)
    return [o.reshape(bsz, t, o.shape[-1]) for o in outs]


def _pack_w_in(w, d):
    widths = (A_WIDTH, KV_WIDTH, KV_WIDTH, IQ_WIDTH, IDX_DIM, IDX_HEADS, B_WIDTH, B_WIDTH, B_WIDTH, B_WIDTH,
              C_WIDTH, N_BRANCHES * d)
    points = [int(s) for s in np.cumsum(widths)[:-1]]
    a_q, a_k, a_v, i_q, i_k, i_w, b_q, b_f, b_i, b_g, c_q, gates = jnp.split(w, points, axis=-1)
    ikw = jnp.concatenate([i_k, jnp.zeros((d, IW_LANE - IDX_DIM), w.dtype), i_w,
                           jnp.zeros((d, LANES - IW_LANE - IDX_HEADS), w.dtype)], axis=-1)
    return jnp.concatenate([a_q, a_k, a_v, i_q, ikw, b_q, b_f, b_i, b_g, c_q, gates], axis=-1).astype(MXU_DTYPE)


def _sortable_key(score):
    bits = pltpu.bitcast(score, I32)
    return jnp.where(bits < 0, (bits ^ 0x7FFFFFFF) + 1, bits)


def _lane_tiled(x, n_tiles):
    return x if n_tiles == 1 else jnp.concatenate([x] * n_tiles, axis=1)


def _dsa_kernel(aq_ref, iq_ref, ikw_ref, ika_ref, ikb_ref, k_ref, v_ref, ya_ref,
                key_sc, thr_sc, m_sc, l_sc, acc_sc, *, tq, tk, key_off, topk, row_sub, idx_bits):
    qi = pl.program_id(1)
    n_lane_tiles = tk // LANES
    n_vis = key_off + (qi + 1) * tq
    n_kb = lax.div(n_vis + (tk - 1), tk)
    row = qi * tq + lax.broadcasted_iota(I32, (tq, 1), 0)
    limit = key_off + (lax.shift_right_logical(row, CHUNK.bit_length() - 1) + 1) * CHUNK

    iw = ikw_ref[0, :, IW_LANE:IW_LANE + IDX_HEADS]

    def score_block(kb, carry):
        k0 = pl.multiple_of(kb * tk, tk)
        ika = ika_ref[0, pl.ds(k0, tk), :]
        ikb = ikb_ref[0, pl.ds(k0, tk), :]
        acc = jnp.zeros((tq, tk), F32)
        for j in range(IQ_WIDTH // LANES):
            iqt = iq_ref[0, :, j * LANES:(j + 1) * LANES]
            acc = acc + jnp.maximum(_dot_nt(iqt, ika), 0.0) * iw[:, 2 * j:2 * j + 1]
            acc = acc + jnp.maximum(_dot_nt(iqt, ikb), 0.0) * iw[:, 2 * j + 1:2 * j + 2]
        col = k0 + lax.broadcasted_iota(I32, (tq, tk), 1)
        key_sc[kb] = jnp.where(col < limit, _sortable_key(acc), KEY_NEG_INF)
        return carry

    lax.fori_loop(0, n_kb, score_block, 0)

    def count_rows(rows, pred):
        def body(kb, acc):
            blk = key_sc[kb, rows, :]
            base = kb * tk
            for c in range(n_lane_tiles):
                cols = base + c * LANES + lax.broadcasted_iota(I32, (row_sub, LANES), 1)
                acc = acc + jnp.where(pred(blk[:, c * LANES:(c + 1) * LANES], cols), 1, 0)
            return acc
        acc = lax.fori_loop(0, n_kb, body, jnp.zeros((row_sub, LANES), I32))
        return jnp.sum(acc.astype(F32), axis=1, keepdims=True)

    for rs in range(tq // row_sub):
        rows = pl.ds(rs * row_sub, row_sub)

        def key_bit(it, u):
            trial = u | jnp.left_shift(jnp.int32(1), 31 - it)
            cand = trial ^ INT_MIN
            n_ge = count_rows(rows, lambda k, _: k >= cand)
            return jnp.where(n_ge >= topk, trial, u)

        u = lax.fori_loop(0, 32, key_bit, jnp.zeros((row_sub, LANES), I32))
        thr = u ^ INT_MIN
        n_gt = count_rows(rows, lambda k, _: k > thr)
        n_ge = count_rows(rows, lambda k, _: k >= thr)
        need = topk - n_gt
        has_tie = jnp.where((n_ge > topk) & (thr[:, :1] > KEY_NEG_INF), 1.0, 0.0)

        @pl.when(jnp.max(has_tie) > 0.0)
        def _():
            def idx_bit(it, cut):
                trial = cut | jnp.left_shift(jnp.int32(1), idx_bits - 1 - it)
                n_before = count_rows(rows, lambda k, c: (k == thr) & (c < trial))
                return jnp.where(n_before < need, trial, cut)

            cut = lax.fori_loop(0, idx_bits, idx_bit, jnp.zeros((row_sub, LANES), I32))

            def demote(kb, carry):
                blk = key_sc[kb, rows, :]
                cols = kb * tk + lax.broadcasted_iota(I32, (row_sub, tk), 1)
                drop = (blk == _lane_tiled(thr, n_lane_tiles)) & (cols > _lane_tiled(cut, n_lane_tiles))
                key_sc[kb, rows, :] = jnp.where(drop, _lane_tiled(thr, n_lane_tiles) - 1, blk)
                return carry

            lax.fori_loop(0, n_kb, demote, 0)

        thr_sc[rows, :] = jnp.maximum(thr, KEY_NEG_INF + 1)

    m_sc[...] = jnp.full(m_sc.shape, NEG_BIG, F32)
    l_sc[...] = jnp.zeros(l_sc.shape, F32)
    acc_sc[...] = jnp.zeros(acc_sc.shape, F32)
    group = A_HEADS // A_KV_HEADS

    def attend_block(kb, carry):
        k0 = pl.multiple_of(kb * tk, tk)
        sel = key_sc[kb] >= _lane_tiled(thr_sc[...], n_lane_tiles)
        for g in range(A_KV_HEADS):
            kblk = k_ref[0, pl.ds(k0, tk), g * LANES:(g + 1) * LANES]
            vblk = v_ref[0, pl.ds(k0, tk), g * LANES:(g + 1) * LANES]
            for r in range(group):
                h = g * group + r
                q = aq_ref[0, :, h * LANES:(h + 1) * LANES]
                s = jnp.where(sel, _dot_nt(q, kblk), NEG_BIG)
                m_prev = m_sc[h]
                m_new = jnp.maximum(m_prev, jnp.max(s, axis=1, keepdims=True))
                alpha = jnp.exp(m_prev - m_new)
                p = jnp.exp(s - m_new[:, :1])
                l_sc[h] = alpha * l_sc[h] + jnp.sum(p, axis=1, keepdims=True)
                acc_sc[h] = alpha * acc_sc[h] + _dot(p.astype(MXU_DTYPE), vblk)
                m_sc[h] = m_new
        return carry

    lax.fori_loop(0, n_kb, attend_block, 0)
    for h in range(A_HEADS):
        ya_ref[0, :, h * LANES:(h + 1) * LANES] = (acc_sc[h] / l_sc[h]).astype(ya_ref.dtype)


def _dsa(aq, iq, ikw, ika, ikb, k16, v16, *, n_valid_keys, q_tile, key_tile):
    bsz, t, _ = aq.shape
    lp = k16.shape[1]
    tq, tk = q_tile, key_tile
    assert t % tq == 0 and tq % CHUNK == 0 and lp % tk == 0 and tk % LANES == 0 and n_valid_keys <= lp
    topk = min(TOPK_MAX, n_valid_keys // 4)
    assert tk >= topk
    row_sub = min(tq, 128)
    kernel = functools.partial(
        _dsa_kernel, tq=tq, tk=tk, key_off=n_valid_keys - t, topk=topk, row_sub=row_sub,
        idx_bits=max(1, int(lp - 1).bit_length()))

    def q_spec(width):
        return pl.BlockSpec((1, tq, width), lambda b, i: (b, i, 0))

    def key_spec(width):
        return pl.BlockSpec((1, lp, width), lambda b, i: (b, 0, 0), pipeline_mode=pl.Buffered(1))

    return pl.pallas_call(
        kernel,
        grid=(bsz, t // tq),
        in_specs=[q_spec(A_WIDTH), q_spec(IQ_WIDTH), q_spec(LANES), key_spec(LANES), key_spec(LANES),
                  key_spec(KV_WIDTH), key_spec(KV_WIDTH)],
        out_specs=q_spec(A_WIDTH),
        out_shape=jax.ShapeDtypeStruct((bsz, t, A_WIDTH), MXU_DTYPE),
        scratch_shapes=[pltpu.VMEM((lp // tk, tq, tk), I32), pltpu.VMEM((tq, LANES), I32),
                        pltpu.VMEM((A_HEADS, tq, LANES), F32), pltpu.VMEM((A_HEADS, tq, LANES), F32),
                        pltpu.VMEM((A_HEADS, tq, LANES), F32)],
        compiler_params=pltpu.CompilerParams(dimension_semantics=("arbitrary", "arbitrary"),
                                             vmem_limit_bytes=VMEM_LIMIT_BYTES),
        name="dsa",
    )(aq, iq, ikw, ika, ikb, k16, v16)


SUB = SUBLANES
N_SUB = CHUNK // SUB
INTER_ROWS = SUB * (N_SUB * (N_SUB - 1) // 2)


def _split3(x):
    hi = x.astype(MXU_DTYPE)
    r1 = x - hi.astype(F32)
    mid = r1.astype(MXU_DTYPE)
    lo = (r1 - mid.astype(F32)).astype(MXU_DTYPE)
    return hi, mid, lo


def _hgrn_kernel(q_ref, k_ref, lf_ref, v_ref, g_ref, gain_ref, s0_ref, yb_ref, sout_ref, st_sc):
    c = pl.program_id(1)

    @pl.when(c == 0)
    def _():
        for h in range(B_HEADS):
            st_sc[h] = s0_ref[0, h].T

    q = q_ref[0].astype(F32)
    k = k_ref[0].astype(F32)
    v16 = v_ref[0]
    lf = lf_ref[0]

    ti = lax.broadcasted_iota(I32, (CHUNK, CHUNK), 0)
    si = lax.broadcasted_iota(I32, (CHUNK, CHUNK), 1)
    tri = jnp.where(si <= ti, 1.0, 0.0).astype(MXU_DTYPE)
    hi, mid, lo = _split3(lf)
    b = _dot(tri, hi) + _dot(tri, mid) + _dot(tri, lo)

    def row_bcast(x, r, n):
        return jnp.broadcast_to(x[r:r + 1, :], (n, x.shape[1]))

    b_start = jnp.concatenate(
        [jnp.zeros((SUB, B_WIDTH), F32)] + [row_bcast(b, j * SUB - 1, SUB) for j in range(1, N_SUB)], axis=0)
    q_hat = (q * jnp.exp(b - b_start)).astype(MXU_DTYPE)
    q_state = (q * jnp.exp(b)).astype(MXU_DTYPE)
    b_end = b[CHUNK - 1:CHUNK, :]
    k_end = k * jnp.exp(b_end - b)

    k_hat = jnp.concatenate(
        [k[:i * SUB] * jnp.exp(row_bcast(b, i * SUB - 1, i * SUB) - b[:i * SUB]) for i in range(1, N_SUB)],
        axis=0).astype(MXU_DTYPE)
    v_hat = jnp.concatenate([v16[:i * SUB] for i in range(1, N_SUB)], axis=0)

    q3 = q.reshape(N_SUB, SUB, B_WIDTH)
    k3 = k.reshape(N_SUB, SUB, B_WIDTH)
    b3 = b.reshape(N_SUB, SUB, B_WIDTH)
    intra = []
    for s in range(SUB):
        kp = jnp.broadcast_to(k3[:, s:s + 1, :], q3.shape)
        bp = jnp.broadcast_to(b3[:, s:s + 1, :], q3.shape)
        x = q3 * kp * jnp.exp(jnp.minimum(b3 - bp, 0.0))
        intra.append(x.reshape(CHUNK, B_WIDTH).astype(MXU_DTYPE))

    pr = lax.broadcasted_iota(I32, (SUB * LANES, CHUNK), 0)
    pc = lax.broadcasted_iota(I32, (SUB * LANES, CHUNK), 1)
    place = jnp.where(lax.shift_right_logical(pr, LANES.bit_length() - 1) == (pc & (SUB - 1)), 1.0, 0.0
                      ).astype(MXU_DTYPE)
    t_sub = lax.shift_right_logical(ti, SUB.bit_length() - 1)
    intra_ok = (t_sub == lax.shift_right_logical(si, SUB.bit_length() - 1)) & (si <= ti)
    tt = lax.broadcasted_iota(I32, (CHUNK, INTER_ROWS), 0)
    cc = lax.broadcasted_iota(I32, (CHUNK, INTER_ROWS), 1)
    seg = jnp.ones((CHUNK, INTER_ROWS), I32)
    for i in range(2, N_SUB):
        seg = seg + jnp.where(cc >= SUB * (i * (i - 1) // 2), 1, 0)
    inter_ok = lax.shift_right_logical(tt, SUB.bit_length() - 1) == seg

    gain = gain_ref[...]
    for h in range(B_HEADS):
        sl = slice(h * LANES, (h + 1) * LANES)
        st = st_sc[h]
        x_cat = jnp.concatenate([x[:, sl] for x in intra], axis=1)
        p_intra = jnp.where(intra_ok, _dot(x_cat, place), 0.0).astype(MXU_DTYPE)
        p_inter = jnp.where(inter_ok, _dot_nt(q_hat[:, sl], k_hat[:, sl]), 0.0).astype(MXU_DTYPE)
        o = (_dot(p_inter, v_hat[:, sl]) + _dot(p_intra, v16[:, sl])
             + _dot_nt(q_state[:, sl], st.astype(MXU_DTYPE)))
        st_new = st * jnp.exp(b_end[:, sl]) + _dot(v16[:, sl].astype(F32).T.astype(MXU_DTYPE),
                                                    k_end[:, sl].astype(MXU_DTYPE))
        st_sc[h] = st_new
        y = _rms(o, gain[:, sl]) * g_ref[0, :, sl].astype(F32)
        yb_ref[0, :, sl] = y.astype(yb_ref.dtype)

    @pl.when(c == pl.num_programs(1) - 1)
    def _():
        for h in range(B_HEADS):
            sout_ref[0, h] = st_sc[h].T


def _hgrn(hq, hk, hlf, hv, hg, norm_gain, state0):
    bsz, t, _ = hq.shape
    assert t % CHUNK == 0

    def seq_spec():
        return pl.BlockSpec((1, CHUNK, B_WIDTH), lambda b, c: (b, c, 0))

    state_spec = pl.BlockSpec((1, B_HEADS, B_KEY_DIM, B_VAL_DIM), lambda b, c: (b, 0, 0, 0))
    return pl.pallas_call(
        _hgrn_kernel,
        grid=(bsz, t // CHUNK),
        in_specs=[seq_spec()] * 5 + [pl.BlockSpec((1, B_WIDTH), lambda b, c: (0, 0)), state_spec],
        out_specs=[seq_spec(), state_spec],
        out_shape=[jax.ShapeDtypeStruct((bsz, t, B_WIDTH), MXU_DTYPE),
                   jax.ShapeDtypeStruct((bsz, B_HEADS, B_KEY_DIM, B_VAL_DIM), F32)],
        scratch_shapes=[pltpu.VMEM((B_HEADS, B_VAL_DIM, B_KEY_DIM), F32)],
        compiler_params=pltpu.CompilerParams(dimension_semantics=("arbitrary", "arbitrary"),
                                             vmem_limit_bytes=VMEM_LIMIT_BYTES),
        name="hgrn",
    )(hq, hk, hlf, hv, hg, norm_gain.reshape(1, B_WIDTH), state0)


def _memory_kv_kernel(mem_ref, gain_ref, w_ref, k32_ref, v32_ref, k16_ref, v16_ref):
    hb = _rms(mem_ref[0], gain_ref[...]).astype(MXU_DTYPE)
    mk = _dot(hb, w_ref[:, :C_WIDTH])
    mv = _dot(hb, w_ref[:, C_WIDTH:])
    k32_ref[0] = mk
    v32_ref[0] = mv
    k16_ref[0] = mk.astype(MXU_DTYPE)
    v16_ref[0] = mv.astype(MXU_DTYPE)


def _memory_kv(mem, gain, w16):
    bsz, n, d = mem.shape
    spec = pl.BlockSpec((1, n, C_WIDTH), lambda b: (b, 0, 0))
    return pl.pallas_call(
        _memory_kv_kernel,
        grid=(bsz,),
        in_specs=[pl.BlockSpec((1, n, d), lambda b: (b, 0, 0)), _resident((1, d)), _resident((d, 2 * C_WIDTH))],
        out_specs=[spec] * 4,
        out_shape=[jax.ShapeDtypeStruct((bsz, n, C_WIDTH), dt) for dt in (F32, F32, MXU_DTYPE, MXU_DTYPE)],
        compiler_params=pltpu.CompilerParams(dimension_semantics=("arbitrary",), vmem_limit_bytes=VMEM_LIMIT_BYTES),
        name="memory_kv",
    )(mem, gain.reshape(1, d), w16)


def _merge_ffn_kernel(x_ref, ya_ref, yb_ref, cq_ref, mk_ref, mv_ref, gt_ref, wa_ref, wb_ref, wc_ref, wo_ref,
                      g_mix_ref, g_pre_ref, g_post_ref, wu_ref, wd_ref, out_ref):
    d = x_ref.shape[-1]
    heads = []
    for h in range(C_HEADS):
        sl = slice(h * C_HEAD_DIM, (h + 1) * C_HEAD_DIM)
        s = _dot_nt(cq_ref[0, :, sl], mk_ref[0, :, sl])
        p = jnp.exp(s - jnp.max(s, axis=-1, keepdims=True))
        p = p / jnp.sum(p, axis=-1, keepdims=True)
        heads.append(_dot(p.astype(MXU_DTYPE), mv_ref[0, :, sl]).astype(MXU_DTYPE))
    y_c = jnp.concatenate(heads, axis=1)

    merged = (gt_ref[0, :, 0:d].astype(F32) * _dot(ya_ref[0], wa_ref[...])
              + gt_ref[0, :, d:2 * d].astype(F32) * _dot(yb_ref[0], wb_ref[...])
              + gt_ref[0, :, 2 * d:3 * d].astype(F32) * _dot(y_c, wc_ref[...]))
    x1 = x_ref[0] + _rms(_dot(merged.astype(MXU_DTYPE), wo_ref[...]), g_mix_ref[...])
    u = jnp.maximum(_dot(_rms(x1, g_pre_ref[...]).astype(MXU_DTYPE), wu_ref[...]), 0.0)
    down = _dot((u * u).astype(MXU_DTYPE), wd_ref[...])
    out_ref[0] = x1 + _rms(down, g_post_ref[...])


def _merge_ffn(x, ya, yb, cq, mk16, mv16, gates, weights, gains, row_tile):
    bsz, t, d = x.shape
    tm = row_tile
    assert t % tm == 0
    n_mem = mk16.shape[1]

    def rows(width):
        return pl.BlockSpec((1, tm, width), lambda b, i: (b, i, 0))

    mem_spec = pl.BlockSpec((1, n_mem, C_WIDTH), lambda b, i: (b, 0, 0))
    return pl.pallas_call(
        _merge_ffn_kernel,
        grid=(bsz, t // tm),
        in_specs=[rows(d), rows(A_WIDTH), rows(B_WIDTH), rows(C_WIDTH), mem_spec, mem_spec, rows(N_BRANCHES * d)]
        + [_resident(w.shape) for w in weights[:4]] + [_resident((1, d))] * 3
        + [_resident(w.shape) for w in weights[4:]],
        out_specs=rows(d),
        out_shape=jax.ShapeDtypeStruct((bsz, t, d), F32),
        compiler_params=pltpu.CompilerParams(dimension_semantics=("arbitrary", "arbitrary"),
                                             vmem_limit_bytes=VMEM_LIMIT_BYTES),
        name="merge_ffn",
    )(x, ya, yb, cq, mk16, mv16, gates, *weights[:4], *[g.reshape(1, d) for g in gains], *weights[4:])


def _layer(x, pos, lower_bound, lp, *, cache, state0, mem16, row_tile, q_tile, key_tile):
    bsz, t, d = x.shape
    (aq, k32, v32, k16, v16, iq, ikw, ika, ikb, hq, hk, hlf, hv, hg, cq, gates) = _in_proj(
        x, pos, lp["pre_mix_gain"], lower_bound, lp["w_in"], min(row_tile, bsz * t))

    keys = [ika, ikb, k16, v16]
    if cache is not None:
        keys = [jnp.concatenate([c, kn], axis=1) for c, kn in zip(cache, keys)]
    n_keys = keys[0].shape[1]
    pad = (-n_keys) % key_tile
    if pad:
        keys = [jnp.pad(kk, ((0, 0), (0, pad), (0, 0))) for kk in keys]
    ya = _dsa(aq, iq, ikw, *keys, n_valid_keys=n_keys, q_tile=q_tile, key_tile=key_tile)

    yb, s_new = _hgrn(hq, hk, hlf, hv, hg, lp["hgrn_norm_gain"], state0)

    out = _merge_ffn(x, ya, yb, cq, mem16[0], mem16[1], gates,
                     [lp["w_out_a"], lp["w_out_b"], lp["w_out_c"], lp["w_out"], lp["w_up"], lp["w_down"]],
                     [lp["post_mix_gain"], lp["pre_ffn_gain"], lp["post_ffn_gain"]], row_tile=min(row_tile, t))
    new_k = k32.reshape(bsz, t, A_KV_HEADS, HEAD_DIM)
    new_v = v32.reshape(bsz, t, A_KV_HEADS, HEAD_DIM)
    return out, (new_k, new_v, ikw[..., :IDX_DIM], s_new)


def _tiles(t):
    return 256, min(256, t)


def kernel(x_prompt, x_sample, cache_k, cache_v, cache_idx_k, state_hgrn, cache_mem_k, cache_mem_v, mem_prompt, w_in, w_mem_kv, mem_norm_gain, hgrn_lb_logits, hgrn_norm_gain, w_out_a, w_out_b, w_out_c, w_out, pre_mix_gain, post_mix_gain, pre_ffn_gain, post_ffn_gain, w_up, w_down):
    depth = w_in.shape[0]
    d = x_prompt.shape[-1]
    bp, tp, _ = x_prompt.shape
    bs, ts, _ = x_sample.shape
    past = cache_k.shape[2]
    lower_bounds = jnp.cumsum(jax.nn.softmax(hgrn_lb_logits.astype(F32), axis=0), axis=0)
    pos_p = jnp.arange(tp, dtype=I32)
    pos_s = past + jnp.arange(ts, dtype=I32)
    bf = lambda a: a.astype(MXU_DTYPE)

    xp, xs = x_prompt, x_sample
    outs_p, outs_s = [], []
    for l in range(depth):
        lp = {
            "w_in": _pack_w_in(w_in[l], d), "hgrn_norm_gain": hgrn_norm_gain[l],
            "w_out_a": bf(w_out_a[l]), "w_out_b": bf(w_out_b[l]), "w_out_c": bf(w_out_c[l]), "w_out": bf(w_out[l]),
            "pre_mix_gain": pre_mix_gain[l], "post_mix_gain": post_mix_gain[l], "pre_ffn_gain": pre_ffn_gain[l],
            "post_ffn_gain": post_ffn_gain[l], "w_up": bf(w_up[l]), "w_down": bf(w_down[l]),
        }
        mk32, mv32, mk16, mv16 = _memory_kv(mem_prompt, mem_norm_gain[l], bf(w_mem_kv[l]))
        n_mem = mem_prompt.shape[1]
        row_tile, q_tile = _tiles(tp)
        xp, st_p = _layer(xp, pos_p, lower_bounds[l], lp, cache=None,
                          state0=jnp.zeros((bp, B_HEADS, B_KEY_DIM, B_VAL_DIM), F32), mem16=(mk16, mv16),
                          row_tile=row_tile, q_tile=q_tile, key_tile=512)
        outs_p.append(st_p + (mk32.reshape(bp, n_mem, C_HEADS, C_HEAD_DIM), mv32.reshape(bp, n_mem, C_HEADS, C_HEAD_DIM)))

        cik = cache_idx_k[l]
        zeros_ik = jnp.zeros_like(cik)
        cache = (bf(jnp.concatenate([cik, zeros_ik], axis=-1)), bf(jnp.concatenate([zeros_ik, cik], axis=-1)),
                 bf(cache_k[l].reshape(bs, past, KV_WIDTH)), bf(cache_v[l].reshape(bs, past, KV_WIDTH)))
        mem16 = (bf(cache_mem_k[l].reshape(bs, -1, C_WIDTH)), bf(cache_mem_v[l].reshape(bs, -1, C_WIDTH)))
        row_tile, q_tile = _tiles(ts)
        xs, st_s = _layer(xs, pos_s, lower_bounds[l], lp, cache=cache, state0=state_hgrn[l].astype(F32),
                          mem16=mem16, row_tile=row_tile, q_tile=q_tile, key_tile=384)
        outs_s.append(st_s)

    new_k_p, new_v_p, new_ik_p, new_s_p, new_mk_p, new_mv_p = [jnp.stack(a) for a in zip(*outs_p)]
    new_k_s, new_v_s, new_ik_s, new_s_s = [jnp.stack(a) for a in zip(*outs_s)]
    return (xp, xs, new_k_p, new_v_p, new_ik_p, new_s_p, new_mk_p, new_mv_p, new_k_s, new_v_s, new_ik_s, new_s_s)
```

```python
import functools

import jax
import jax.numpy as jnp
import numpy as np
from jax import lax
from jax.experimental import pallas as pl
from jax.experimental.pallas import tpu as pltpu

CHUNK = 64
N_BRANCHES = 3
A_HEADS, A_KV_HEADS, HEAD_DIM = 6, 2, 128
ROT_DIM = HEAD_DIM // 4
IDX_HEADS, IDX_DIM = 8, 64
IDX_ROT_DIM = IDX_DIM // 4
TOPK_MAX = 256
B_HEADS, B_KEY_DIM, B_VAL_DIM = 6, 128, 128
C_HEADS, C_HEAD_DIM = 4, 128
ROPE_THETA = 500000.0
EPS = 1e-6

A_WIDTH = A_HEADS * HEAD_DIM
KV_WIDTH = A_KV_HEADS * HEAD_DIM
IQ_WIDTH = IDX_HEADS * IDX_DIM
B_WIDTH = B_HEADS * B_KEY_DIM
C_WIDTH = C_HEADS * C_HEAD_DIM

LANES = 128
SUBLANES = 8
VMEM_LIMIT_BYTES = 56 * 1024 * 1024

MXU_DTYPE = jnp.bfloat16
F32 = jnp.float32
I32 = jnp.int32

IW_LANE = 96
NEG_BIG = -1e30
LOG2E = 1.4426950408889634
ATT_ROWS = 128
INT_MIN = -(2 ** 31)
KEY_NEG_INF = INT_MIN + 0x00800000

OFF_AQ = 0
OFF_AK = OFF_AQ + A_WIDTH
OFF_AV = OFF_AK + KV_WIDTH
OFF_IQ = OFF_AV + KV_WIDTH
OFF_IKW = OFF_IQ + IQ_WIDTH
OFF_BQ = OFF_IKW + LANES
OFF_BF = OFF_BQ + B_WIDTH
OFF_BI = OFF_BF + B_WIDTH
OFF_BG = OFF_BI + B_WIDTH
OFF_CQ = OFF_BG + B_WIDTH
OFF_GATES = OFF_CQ + C_WIDTH


def _sigmoid(x):
    return 1.0 / (1.0 + jnp.exp(-x))


def _dot(a, b):
    return jnp.dot(a, b, preferred_element_type=F32)


def _dot_nt(a, b):
    return lax.dot_general(a, b, (((1,), (1,)), ((), ())), preferred_element_type=F32)


def _rms(x, gain):
    return x * lax.rsqrt(jnp.mean(x * x, axis=-1, keepdims=True) + EPS) * gain


def _resident(shape):
    zeros = (0,) * len(shape)
    return pl.BlockSpec(shape, lambda *_: zeros, pipeline_mode=pl.Buffered(1))


def _rope(y, cos, sin_up, sin_down, half):
    return y * cos + pltpu.roll(y, half, 1) * sin_up + pltpu.roll(y, LANES - half, 1) * sin_down


def _in_proj_kernel(x_ref, gain_ref, lb_ref, w_ref, ca_ref, sau_ref, sad_ref, ci_ref, siu_ref, sid_ref,
                    aq_ref, k32_ref, v32_ref, k16_ref, v16_ref, iq_ref, ikw_ref, ika_ref, ikb_ref,
                    hq_ref, hk_ref, hlf_ref, hv_ref, hg_ref, cq_ref, gt_ref):
    hb = _rms(x_ref[...], gain_ref[...]).astype(MXU_DTYPE)

    def proj(c0, width):
        return _dot(hb, w_ref[:, c0:c0 + width])

    ca, sau, sad = ca_ref[...], sau_ref[...], sad_ref[...]
    ci, siu, sid = ci_ref[...], siu_ref[...], sid_ref[...]

    y = proj(OFF_AQ, A_WIDTH)
    for h in range(A_HEADS):
        sl = slice(h * LANES, (h + 1) * LANES)
        aq_ref[:, sl] = (_rope(y[:, sl], ca, sau, sad, ROT_DIM // 2) * ((HEAD_DIM ** -0.5) * LOG2E)).astype(MXU_DTYPE)

    y = proj(OFF_AK, KV_WIDTH)
    for h in range(A_KV_HEADS):
        sl = slice(h * LANES, (h + 1) * LANES)
        r = _rope(y[:, sl], ca, sau, sad, ROT_DIM // 2)
        k32_ref[:, sl] = r
        k16_ref[:, sl] = r.astype(MXU_DTYPE)

    y = proj(OFF_AV, KV_WIDTH)
    v32_ref[...] = y
    v16_ref[...] = y.astype(MXU_DTYPE)

    y = proj(OFF_IQ, IQ_WIDTH)
    for j in range(IQ_WIDTH // LANES):
        sl = slice(j * LANES, (j + 1) * LANES)
        iq_ref[:, sl] = _rope(y[:, sl], ci, siu, sid, IDX_ROT_DIM // 2).astype(MXU_DTYPE)

    r = _rope(proj(OFF_IKW, LANES), ci, siu, sid, IDX_ROT_DIM // 2)
    lane = lax.broadcasted_iota(I32, r.shape, 1)
    ikw_ref[...] = jnp.where(lane < IDX_DIM, r, r * ((IDX_DIM ** -0.5) * (IDX_HEADS ** -0.5)))
    ik_low = jnp.where(lane < IDX_DIM, r, 0.0)
    ika_ref[...] = ik_low.astype(MXU_DTYPE)
    ikb_ref[...] = pltpu.roll(ik_low, IDX_DIM, 1).astype(MXU_DTYPE)

    y = proj(OFF_BQ, B_WIDTH)
    hq_ref[...] = (y * _sigmoid(y)).astype(MXU_DTYPE)
    lb = lb_ref[...]
    f = lb + (1.0 - lb) * _sigmoid(proj(OFF_BF, B_WIDTH))
    hk_ref[...] = (1.0 - f).astype(MXU_DTYPE)
    hlf_ref[...] = jnp.log(f)
    hv_ref[...] = proj(OFF_BI, B_WIDTH).astype(MXU_DTYPE)
    y = proj(OFF_BG, B_WIDTH)
    hg_ref[...] = (y * _sigmoid(y)).astype(MXU_DTYPE)

    cq_ref[...] = (proj(OFF_CQ, C_WIDTH) * (C_HEAD_DIM ** -0.5)).astype(MXU_DTYPE)

    n_gate = gt_ref.shape[1]
    for c0 in range(0, n_gate, B_WIDTH):
        gt_ref[:, c0:c0 + B_WIDTH] = _sigmoid(proj(OFF_GATES + c0, B_WIDTH)).astype(MXU_DTYPE)


def _rope_tables(pos, rot_dim, head_dim):
    half = rot_dim // 2
    inv_freq = ROPE_THETA ** (-jnp.arange(half, dtype=F32) / half)
    ang = pos.astype(F32)[:, None] * inv_freq[None, :]
    cos, sin = jnp.cos(ang), jnp.sin(ang)
    lane = np.arange(LANES) % head_dim
    idx = lane % half
    first, second = lane < half, (lane >= half) & (lane < rot_dim)
    cos_t = jnp.where(first | second, cos[:, idx], 1.0)
    sin_up = jnp.where(second, sin[:, idx], 0.0)
    sin_down = jnp.where(first, -sin[:, idx], 0.0)
    return cos_t, sin_up, sin_down


def _in_proj(x, pos, gain, lower_bound, w_packed, row_tile):
    bsz, t, d = x.shape
    n = bsz * t
    tm = row_tile
    assert n % tm == 0 and (t % tm == 0 or tm % t == 0)
    tables = _rope_tables(pos, ROT_DIM, HEAD_DIM) + _rope_tables(pos, IDX_ROT_DIM, IDX_DIM)
    table_rows = max(t, tm)
    tables = [jnp.tile(tb, (table_rows // t, 1)) for tb in tables]
    n_table_blocks = table_rows // tm
    n_gate = N_BRANCHES * d
    w_width = w_packed.shape[1]

    def rows(width):
        return pl.BlockSpec((tm, width), lambda i: (i, 0))

    table_spec = pl.BlockSpec((tm, LANES), lambda i: (i % n_table_blocks, 0))
    widths_dtypes = [
        (A_WIDTH, MXU_DTYPE), (KV_WIDTH, F32), (KV_WIDTH, F32), (KV_WIDTH, MXU_DTYPE), (KV_WIDTH, MXU_DTYPE),
        (IQ_WIDTH, MXU_DTYPE), (LANES, F32), (LANES, MXU_DTYPE), (LANES, MXU_DTYPE),
        (B_WIDTH, MXU_DTYPE), (B_WIDTH, MXU_DTYPE), (B_WIDTH, F32), (B_WIDTH, MXU_DTYPE), (B_WIDTH, MXU_DTYPE),
        (C_WIDTH, MXU_DTYPE), (n_gate, MXU_DTYPE)]
    outs = pl.pallas_call(
        _in_proj_kernel,
        grid=(n // tm,),
        in_specs=[rows(d), _resident((1, d)), _resident((1, B_WIDTH)), _resident((d, w_width))] + [table_spec] * 6,
        out_specs=[rows(w) for w, _ in widths_dtypes],
        out_shape=[jax.ShapeDtypeStruct((n, w), dt) for w, dt in widths_dtypes],
        compiler_params=pltpu.CompilerParams(dimension_semantics=("arbitrary",), vmem_limit_bytes=VMEM_LIMIT_BYTES),
        name="in_proj",
    )(x.reshape(n, d), gain.reshape(1, d), lower_bound.reshape(1, B_WIDTH), w_packed, *tables)
    return [o.reshape(bsz, t, o.shape[-1]) for o in outs]


def _pack_w_in(w, d):
    widths = (A_WIDTH, KV_WIDTH, KV_WIDTH, IQ_WIDTH, IDX_DIM, IDX_HEADS, B_WIDTH, B_WIDTH, B_WIDTH, B_WIDTH,
              C_WIDTH, N_BRANCHES * d)
    points = [int(s) for s in np.cumsum(widths)[:-1]]
    a_q, a_k, a_v, i_q, i_k, i_w, b_q, b_f, b_i, b_g, c_q, gates = jnp.split(w, points, axis=-1)
    ikw = jnp.concatenate([i_k, jnp.zeros((d, IW_LANE - IDX_DIM), w.dtype), i_w,
                           jnp.zeros((d, LANES - IW_LANE - IDX_HEADS), w.dtype)], axis=-1)
    return jnp.concatenate([a_q, a_k, a_v, i_q, ikw, b_q, b_f, b_i, b_g, c_q, gates], axis=-1).astype(MXU_DTYPE)


def _sortable_key(score):
    bits = pltpu.bitcast(score, I32)
    return jnp.where(bits < 0, (bits ^ 0x7FFFFFFF) + 1, bits)


def _lane_tiled(x, n_tiles):
    return x if n_tiles == 1 else jnp.concatenate([x] * n_tiles, axis=1)


def _dsa_kernel(aq_ref, iq_ref, ikw_ref, ika_ref, ikb_ref, k_ref, v_ref, ya_ref,
                key_sc, thr_sc, m_sc, l_sc, acc_sc, *, tq, tk, key_off, topk, row_sub, idx_bits):
    qi = pl.program_id(1)
    n_lane_tiles = tk // LANES
    n_vis = key_off + (qi + 1) * tq
    n_kb = lax.div(n_vis + (tk - 1), tk)
    row = qi * tq + lax.broadcasted_iota(I32, (tq, 1), 0)
    limit = key_off + (lax.shift_right_logical(row, CHUNK.bit_length() - 1) + 1) * CHUNK

    iw = ikw_ref[0, :, IW_LANE:IW_LANE + IDX_HEADS]

    def score_block(kb, carry):
        k0 = pl.multiple_of(kb * tk, tk)
        ika = ika_ref[0, pl.ds(k0, tk), :]
        ikb = ikb_ref[0, pl.ds(k0, tk), :]
        acc = jnp.zeros((tq, tk), F32)
        for j in range(IQ_WIDTH // LANES):
            iqt = iq_ref[0, :, j * LANES:(j + 1) * LANES]
            acc = acc + jnp.maximum(_dot_nt(iqt, ika), 0.0) * iw[:, 2 * j:2 * j + 1]
            acc = acc + jnp.maximum(_dot_nt(iqt, ikb), 0.0) * iw[:, 2 * j + 1:2 * j + 2]
        col = k0 + lax.broadcasted_iota(I32, (tq, tk), 1)
        key_sc[kb] = jnp.where(col < limit, _sortable_key(acc), KEY_NEG_INF)
        return carry

    lax.fori_loop(0, n_kb, score_block, 0)

    def count_rows(rows, pred):
        def body(kb, acc):
            blk = key_sc[kb, rows, :]
            base = kb * tk
            for c in range(n_lane_tiles):
                cols = base + c * LANES + lax.broadcasted_iota(I32, (row_sub, LANES), 1)
                acc = acc + jnp.where(pred(blk[:, c * LANES:(c + 1) * LANES], cols), 1, 0)
            return acc
        acc = lax.fori_loop(0, n_kb, body, jnp.zeros((row_sub, LANES), I32))
        return jnp.sum(acc.astype(F32), axis=1, keepdims=True)

    for rs in range(tq // row_sub):
        rows = pl.ds(rs * row_sub, row_sub)

        def key_bit(it, u):
            trial = u | jnp.left_shift(jnp.int32(1), 31 - it)
            cand = trial ^ INT_MIN
            n_ge = count_rows(rows, lambda k, _: k >= cand)
            return jnp.where(n_ge >= topk, trial, u)

        u = lax.fori_loop(0, 32, key_bit, jnp.zeros((row_sub, LANES), I32))
        thr = u ^ INT_MIN
        n_gt = count_rows(rows, lambda k, _: k > thr)
        n_ge = count_rows(rows, lambda k, _: k >= thr)
        need = topk - n_gt
        has_tie = jnp.where((n_ge > topk) & (thr[:, :1] > KEY_NEG_INF), 1.0, 0.0)

        @pl.when(jnp.max(has_tie) > 0.0)
        def _():
            def idx_bit(it, cut):
                trial = cut | jnp.left_shift(jnp.int32(1), idx_bits - 1 - it)
                n_before = count_rows(rows, lambda k, c: (k == thr) & (c < trial))
                return jnp.where(n_before < need, trial, cut)

            cut = lax.fori_loop(0, idx_bits, idx_bit, jnp.zeros((row_sub, LANES), I32))

            def demote(kb, carry):
                blk = key_sc[kb, rows, :]
                cols = kb * tk + lax.broadcasted_iota(I32, (row_sub, tk), 1)
                drop = (blk == _lane_tiled(thr, n_lane_tiles)) & (cols > _lane_tiled(cut, n_lane_tiles))
                key_sc[kb, rows, :] = jnp.where(drop, _lane_tiled(thr, n_lane_tiles) - 1, blk)
                return carry

            lax.fori_loop(0, n_kb, demote, 0)

        thr_sc[rows, :] = jnp.maximum(thr, KEY_NEG_INF + 1)

    m_sc[...] = jnp.full(m_sc.shape, NEG_BIG, F32)
    l_sc[...] = jnp.zeros(l_sc.shape, F32)
    acc_sc[...] = jnp.zeros(acc_sc.shape, F32)
    group = A_HEADS // A_KV_HEADS

    ones_blk = jnp.ones((tk, LANES), MXU_DTYPE)
    row_att = min(tq, ATT_ROWS)

    def attend_block(kb, carry):
        k0 = pl.multiple_of(kb * tk, tk)
        bias_sc = key_sc.at[kb]
        bias = jnp.where(bias_sc[...] >= _lane_tiled(thr_sc[...], n_lane_tiles), 0.0, NEG_BIG)
        bias_sc[...] = pltpu.bitcast(bias, I32)
        for g in range(A_KV_HEADS):
            kblk = k_ref[0, pl.ds(k0, tk), g * LANES:(g + 1) * LANES]
            vaug = jnp.concatenate([v_ref[0, pl.ds(k0, tk), g * LANES:(g + 1) * LANES], ones_blk], axis=1)
            for r in range(group):
                h = g * group + r
                for rt in range(tq // row_att):
                    rows = pl.ds(rt * row_att, row_att)
                    q = aq_ref[0, rows, h * LANES:(h + 1) * LANES]
                    s = _dot_nt(q, kblk) + pltpu.bitcast(bias_sc[rows, :], F32)
                    m_prev = m_sc[h, rows, :]
                    m_new = jnp.maximum(m_prev, jnp.max(s, axis=1, keepdims=True))
                    alpha = jnp.exp2(m_prev - m_new)
                    p = jnp.exp2(s - _lane_tiled(m_new, n_lane_tiles))
                    pv = _dot(p.astype(MXU_DTYPE), vaug)
                    acc_sc[h, rows, :] = alpha * acc_sc[h, rows, :] + pv[:, :LANES]
                    l_sc[h, rows, :] = alpha * l_sc[h, rows, :] + pv[:, LANES:]
                    m_sc[h, rows, :] = m_new
        return carry

    lax.fori_loop(0, n_kb, attend_block, 0)
    for h in range(A_HEADS):
        ya_ref[0, :, h * LANES:(h + 1) * LANES] = (acc_sc[h] / l_sc[h]).astype(ya_ref.dtype)


def _dsa(aq, iq, ikw, ika, ikb, k16, v16, *, n_valid_keys, q_tile, key_tile):
    bsz, t, _ = aq.shape
    lp = k16.shape[1]
    tq, tk = q_tile, key_tile
    assert t % tq == 0 and tq % CHUNK == 0 and lp % tk == 0 and tk % LANES == 0 and n_valid_keys <= lp
    topk = min(TOPK_MAX, n_valid_keys // 4)
    assert tk >= topk
    row_sub = min(tq, 128)
    kernel = functools.partial(
        _dsa_kernel, tq=tq, tk=tk, key_off=n_valid_keys - t, topk=topk, row_sub=row_sub,
        idx_bits=max(1, int(lp - 1).bit_length()))

    def q_spec(width):
        return pl.BlockSpec((1, tq, width), lambda b, i: (b, i, 0))

    def key_spec(width):
        return pl.BlockSpec((1, lp, width), lambda b, i: (b, 0, 0), pipeline_mode=pl.Buffered(1))

    return pl.pallas_call(
        kernel,
        grid=(bsz, t // tq),
        in_specs=[q_spec(A_WIDTH), q_spec(IQ_WIDTH), q_spec(LANES), key_spec(LANES), key_spec(LANES),
                  key_spec(KV_WIDTH), key_spec(KV_WIDTH)],
        out_specs=q_spec(A_WIDTH),
        out_shape=jax.ShapeDtypeStruct((bsz, t, A_WIDTH), MXU_DTYPE),
        scratch_shapes=[pltpu.VMEM((lp // tk, tq, tk), I32), pltpu.VMEM((tq, LANES), I32),
                        pltpu.VMEM((A_HEADS, tq, LANES), F32), pltpu.VMEM((A_HEADS, tq, LANES), F32),
                        pltpu.VMEM((A_HEADS, tq, LANES), F32)],
        compiler_params=pltpu.CompilerParams(dimension_semantics=("arbitrary", "arbitrary"),
                                             vmem_limit_bytes=VMEM_LIMIT_BYTES),
        name="dsa",
    )(aq, iq, ikw, ika, ikb, k16, v16)


SUB = SUBLANES
N_SUB = CHUNK // SUB
INTER_ROWS = SUB * (N_SUB * (N_SUB - 1) // 2)


def _split3(x):
    hi = x.astype(MXU_DTYPE)
    r1 = x - hi.astype(F32)
    mid = r1.astype(MXU_DTYPE)
    lo = (r1 - mid.astype(F32)).astype(MXU_DTYPE)
    return hi, mid, lo


def _hgrn_kernel(q_ref, k_ref, lf_ref, v_ref, g_ref, gain_ref, s0_ref, yb_ref, sout_ref, st_sc):
    c = pl.program_id(1)

    @pl.when(c == 0)
    def _():
        for h in range(B_HEADS):
            st_sc[h] = s0_ref[0, h].T

    q = q_ref[0].astype(F32)
    k = k_ref[0].astype(F32)
    v16 = v_ref[0]
    lf = lf_ref[0]

    ti = lax.broadcasted_iota(I32, (CHUNK, CHUNK), 0)
    si = lax.broadcasted_iota(I32, (CHUNK, CHUNK), 1)
    tri = jnp.where(si <= ti, 1.0, 0.0).astype(MXU_DTYPE)
    hi, mid, lo = _split3(lf)
    b = _dot(tri, hi) + _dot(tri, mid) + _dot(tri, lo)

    def row_bcast(x, r, n):
        return jnp.broadcast_to(x[r:r + 1, :], (n, x.shape[1]))

    b_start = jnp.concatenate(
        [jnp.zeros((SUB, B_WIDTH), F32)] + [row_bcast(b, j * SUB - 1, SUB) for j in range(1, N_SUB)], axis=0)
    q_hat = (q * jnp.exp(b - b_start)).astype(MXU_DTYPE)
    q_state = (q * jnp.exp(b)).astype(MXU_DTYPE)
    b_end = b[CHUNK - 1:CHUNK, :]
    k_end = k * jnp.exp(b_end - b)

    k_hat = jnp.concatenate(
        [k[:i * SUB] * jnp.exp(row_bcast(b, i * SUB - 1, i * SUB) - b[:i * SUB]) for i in range(1, N_SUB)],
        axis=0).astype(MXU_DTYPE)
    v_hat = jnp.concatenate([v16[:i * SUB] for i in range(1, N_SUB)], axis=0)

    q3 = q.reshape(N_SUB, SUB, B_WIDTH)
    k3 = k.reshape(N_SUB, SUB, B_WIDTH)
    b3 = b.reshape(N_SUB, SUB, B_WIDTH)
    intra = []
    for s in range(SUB):
        kp = jnp.broadcast_to(k3[:, s:s + 1, :], q3.shape)
        bp = jnp.broadcast_to(b3[:, s:s + 1, :], q3.shape)
        x = q3 * kp * jnp.exp(jnp.minimum(b3 - bp, 0.0))
        intra.append(x.reshape(CHUNK, B_WIDTH).astype(MXU_DTYPE))

    pr = lax.broadcasted_iota(I32, (SUB * LANES, CHUNK), 0)
    pc = lax.broadcasted_iota(I32, (SUB * LANES, CHUNK), 1)
    place = jnp.where(lax.shift_right_logical(pr, LANES.bit_length() - 1) == (pc & (SUB - 1)), 1.0, 0.0
                      ).astype(MXU_DTYPE)
    t_sub = lax.shift_right_logical(ti, SUB.bit_length() - 1)
    intra_ok = (t_sub == lax.shift_right_logical(si, SUB.bit_length() - 1)) & (si <= ti)
    tt = lax.broadcasted_iota(I32, (CHUNK, INTER_ROWS), 0)
    cc = lax.broadcasted_iota(I32, (CHUNK, INTER_ROWS), 1)
    seg = jnp.ones((CHUNK, INTER_ROWS), I32)
    for i in range(2, N_SUB):
        seg = seg + jnp.where(cc >= SUB * (i * (i - 1) // 2), 1, 0)
    inter_ok = lax.shift_right_logical(tt, SUB.bit_length() - 1) == seg

    gain = gain_ref[...]
    for h in range(B_HEADS):
        sl = slice(h * LANES, (h + 1) * LANES)
        st = st_sc[h]
        x_cat = jnp.concatenate([x[:, sl] for x in intra], axis=1)
        p_intra = jnp.where(intra_ok, _dot(x_cat, place), 0.0).astype(MXU_DTYPE)
        p_inter = jnp.where(inter_ok, _dot_nt(q_hat[:, sl], k_hat[:, sl]), 0.0).astype(MXU_DTYPE)
        o = (_dot(p_inter, v_hat[:, sl]) + _dot(p_intra, v16[:, sl])
             + _dot_nt(q_state[:, sl], st.astype(MXU_DTYPE)))
        st_new = st * jnp.exp(b_end[:, sl]) + _dot(v16[:, sl].astype(F32).T.astype(MXU_DTYPE),
                                                    k_end[:, sl].astype(MXU_DTYPE))
        st_sc[h] = st_new
        y = _rms(o, gain[:, sl]) * g_ref[0, :, sl].astype(F32)
        yb_ref[0, :, sl] = y.astype(yb_ref.dtype)

    @pl.when(c == pl.num_programs(1) - 1)
    def _():
        for h in range(B_HEADS):
            sout_ref[0, h] = st_sc[h].T


def _hgrn(hq, hk, hlf, hv, hg, norm_gain, state0):
    bsz, t, _ = hq.shape
    assert t % CHUNK == 0

    def seq_spec():
        return pl.BlockSpec((1, CHUNK, B_WIDTH), lambda b, c: (b, c, 0))

    state_spec = pl.BlockSpec((1, B_HEADS, B_KEY_DIM, B_VAL_DIM), lambda b, c: (b, 0, 0, 0))
    return pl.pallas_call(
        _hgrn_kernel,
        grid=(bsz, t // CHUNK),
        in_specs=[seq_spec()] * 5 + [pl.BlockSpec((1, B_WIDTH), lambda b, c: (0, 0)), state_spec],
        out_specs=[seq_spec(), state_spec],
        out_shape=[jax.ShapeDtypeStruct((bsz, t, B_WIDTH), MXU_DTYPE),
                   jax.ShapeDtypeStruct((bsz, B_HEADS, B_KEY_DIM, B_VAL_DIM), F32)],
        scratch_shapes=[pltpu.VMEM((B_HEADS, B_VAL_DIM, B_KEY_DIM), F32)],
        compiler_params=pltpu.CompilerParams(dimension_semantics=("arbitrary", "arbitrary"),
                                             vmem_limit_bytes=VMEM_LIMIT_BYTES),
        name="hgrn",
    )(hq, hk, hlf, hv, hg, norm_gain.reshape(1, B_WIDTH), state0)


def _memory_kv_kernel(mem_ref, gain_ref, w_ref, k32_ref, v32_ref, k16_ref, v16_ref):
    hb = _rms(mem_ref[0], gain_ref[...]).astype(MXU_DTYPE)
    mk = _dot(hb, w_ref[:, :C_WIDTH])
    mv = _dot(hb, w_ref[:, C_WIDTH:])
    k32_ref[0] = mk
    v32_ref[0] = mv
    k16_ref[0] = mk.astype(MXU_DTYPE)
    v16_ref[0] = mv.astype(MXU_DTYPE)


def _memory_kv(mem, gain, w16):
    bsz, n, d = mem.shape
    spec = pl.BlockSpec((1, n, C_WIDTH), lambda b: (b, 0, 0))
    return pl.pallas_call(
        _memory_kv_kernel,
        grid=(bsz,),
        in_specs=[pl.BlockSpec((1, n, d), lambda b: (b, 0, 0)), _resident((1, d)), _resident((d, 2 * C_WIDTH))],
        out_specs=[spec] * 4,
        out_shape=[jax.ShapeDtypeStruct((bsz, n, C_WIDTH), dt) for dt in (F32, F32, MXU_DTYPE, MXU_DTYPE)],
        compiler_params=pltpu.CompilerParams(dimension_semantics=("arbitrary",), vmem_limit_bytes=VMEM_LIMIT_BYTES),
        name="memory_kv",
    )(mem, gain.reshape(1, d), w16)


def _merge_ffn_kernel(x_ref, ya_ref, yb_ref, cq_ref, mk_ref, mv_ref, gt_ref, wa_ref, wb_ref, wc_ref, wo_ref,
                      g_mix_ref, g_pre_ref, g_post_ref, wu_ref, wd_ref, out_ref):
    d = x_ref.shape[-1]
    heads = []
    for h in range(C_HEADS):
        sl = slice(h * C_HEAD_DIM, (h + 1) * C_HEAD_DIM)
        s = _dot_nt(cq_ref[0, :, sl], mk_ref[0, :, sl])
        p = jnp.exp(s - jnp.max(s, axis=-1, keepdims=True))
        p = p / jnp.sum(p, axis=-1, keepdims=True)
        heads.append(_dot(p.astype(MXU_DTYPE), mv_ref[0, :, sl]).astype(MXU_DTYPE))
    y_c = jnp.concatenate(heads, axis=1)

    merged = (gt_ref[0, :, 0:d].astype(F32) * _dot(ya_ref[0], wa_ref[...])
              + gt_ref[0, :, d:2 * d].astype(F32) * _dot(yb_ref[0], wb_ref[...])
              + gt_ref[0, :, 2 * d:3 * d].astype(F32) * _dot(y_c, wc_ref[...]))
    x1 = x_ref[0] + _rms(_dot(merged.astype(MXU_DTYPE), wo_ref[...]), g_mix_ref[...])
    u = jnp.maximum(_dot(_rms(x1, g_pre_ref[...]).astype(MXU_DTYPE), wu_ref[...]), 0.0)
    down = _dot((u * u).astype(MXU_DTYPE), wd_ref[...])
    out_ref[0] = x1 + _rms(down, g_post_ref[...])


def _merge_ffn(x, ya, yb, cq, mk16, mv16, gates, weights, gains, row_tile):
    bsz, t, d = x.shape
    tm = row_tile
    assert t % tm == 0
    n_mem = mk16.shape[1]

    def rows(width):
        return pl.BlockSpec((1, tm, width), lambda b, i: (b, i, 0))

    mem_spec = pl.BlockSpec((1, n_mem, C_WIDTH), lambda b, i: (b, 0, 0))
    return pl.pallas_call(
        _merge_ffn_kernel,
        grid=(bsz, t // tm),
        in_specs=[rows(d), rows(A_WIDTH), rows(B_WIDTH), rows(C_WIDTH), mem_spec, mem_spec, rows(N_BRANCHES * d)]
        + [_resident(w.shape) for w in weights[:4]] + [_resident((1, d))] * 3
        + [_resident(w.shape) for w in weights[4:]],
        out_specs=rows(d),
        out_shape=jax.ShapeDtypeStruct((bsz, t, d), F32),
        compiler_params=pltpu.CompilerParams(dimension_semantics=("arbitrary", "arbitrary"),
                                             vmem_limit_bytes=VMEM_LIMIT_BYTES),
        name="merge_ffn",
    )(x, ya, yb, cq, mk16, mv16, gates, *weights[:4], *[g.reshape(1, d) for g in gains], *weights[4:])


def _layer(x, pos, lower_bound, lp, *, cache, state0, mem16, row_tile, q_tile, key_tile):
    bsz, t, d = x.shape
    (aq, k32, v32, k16, v16, iq, ikw, ika, ikb, hq, hk, hlf, hv, hg, cq, gates) = _in_proj(
        x, pos, lp["pre_mix_gain"], lower_bound, lp["w_in"], min(row_tile, bsz * t))

    keys = [ika, ikb, k16, v16]
    if cache is not None:
        keys = [jnp.concatenate([c, kn], axis=1) for c, kn in zip(cache, keys)]
    n_keys = keys[0].shape[1]
    pad = (-n_keys) % key_tile
    if pad:
        keys = [jnp.pad(kk, ((0, 0), (0, pad), (0, 0))) for kk in keys]
    ya = _dsa(aq, iq, ikw, *keys, n_valid_keys=n_keys, q_tile=q_tile, key_tile=key_tile)

    yb, s_new = _hgrn(hq, hk, hlf, hv, hg, lp["hgrn_norm_gain"], state0)

    out = _merge_ffn(x, ya, yb, cq, mem16[0], mem16[1], gates,
                     [lp["w_out_a"], lp["w_out_b"], lp["w_out_c"], lp["w_out"], lp["w_up"], lp["w_down"]],
                     [lp["post_mix_gain"], lp["pre_ffn_gain"], lp["post_ffn_gain"]], row_tile=min(row_tile, t))
    new_k = k32.reshape(bsz, t, A_KV_HEADS, HEAD_DIM)
    new_v = v32.reshape(bsz, t, A_KV_HEADS, HEAD_DIM)
    return out, (new_k, new_v, ikw[..., :IDX_DIM], s_new)


def _tiles(t):
    return 256, min(256, t)


def kernel(x_prompt, x_sample, cache_k, cache_v, cache_idx_k, state_hgrn, cache_mem_k, cache_mem_v, mem_prompt, w_in, w_mem_kv, mem_norm_gain, hgrn_lb_logits, hgrn_norm_gain, w_out_a, w_out_b, w_out_c, w_out, pre_mix_gain, post_mix_gain, pre_ffn_gain, post_ffn_gain, w_up, w_down):
    depth = w_in.shape[0]
    d = x_prompt.shape[-1]
    bp, tp, _ = x_prompt.shape
    bs, ts, _ = x_sample.shape
    past = cache_k.shape[2]
    lower_bounds = jnp.cumsum(jax.nn.softmax(hgrn_lb_logits.astype(F32), axis=0), axis=0)
    pos_p = jnp.arange(tp, dtype=I32)
    pos_s = past + jnp.arange(ts, dtype=I32)
    bf = lambda a: a.astype(MXU_DTYPE)

    xp, xs = x_prompt, x_sample
    outs_p, outs_s = [], []
    for l in range(depth):
        lp = {
            "w_in": _pack_w_in(w_in[l], d), "hgrn_norm_gain": hgrn_norm_gain[l],
            "w_out_a": bf(w_out_a[l]), "w_out_b": bf(w_out_b[l]), "w_out_c": bf(w_out_c[l]), "w_out": bf(w_out[l]),
            "pre_mix_gain": pre_mix_gain[l], "post_mix_gain": post_mix_gain[l], "pre_ffn_gain": pre_ffn_gain[l],
            "post_ffn_gain": post_ffn_gain[l], "w_up": bf(w_up[l]), "w_down": bf(w_down[l]),
        }
        mk32, mv32, mk16, mv16 = _memory_kv(mem_prompt, mem_norm_gain[l], bf(w_mem_kv[l]))
        n_mem = mem_prompt.shape[1]
        row_tile, q_tile = _tiles(tp)
        xp, st_p = _layer(xp, pos_p, lower_bounds[l], lp, cache=None,
                          state0=jnp.zeros((bp, B_HEADS, B_KEY_DIM, B_VAL_DIM), F32), mem16=(mk16, mv16),
                          row_tile=row_tile, q_tile=q_tile, key_tile=512)
        outs_p.append(st_p + (mk32.reshape(bp, n_mem, C_HEADS, C_HEAD_DIM), mv32.reshape(bp, n_mem, C_HEADS, C_HEAD_DIM)))

        cik = cache_idx_k[l]
        zeros_ik = jnp.zeros_like(cik)
        cache = (bf(jnp.concatenate([cik, zeros_ik], axis=-1)), bf(jnp.concatenate([zeros_ik, cik], axis=-1)),
                 bf(cache_k[l].reshape(bs, past, KV_WIDTH)), bf(cache_v[l].reshape(bs, past, KV_WIDTH)))
        mem16 = (bf(cache_mem_k[l].reshape(bs, -1, C_WIDTH)), bf(cache_mem_v[l].reshape(bs, -1, C_WIDTH)))
        row_tile, q_tile = _tiles(ts)
        xs, st_s = _layer(xs, pos_s, lower_bounds[l], lp, cache=cache, state0=state_hgrn[l].astype(F32),
                          mem16=mem16, row_tile=row_tile, q_tile=q_tile, key_tile=384)
        outs_s.append(st_s)

    new_k_p, new_v_p, new_ik_p, new_s_p, new_mk_p, new_mv_p = [jnp.stack(a) for a in zip(*outs_p)]
    new_k_s, new_v_s, new_ik_s, new_s_s = [jnp.stack(a) for a in zip(*outs_s)]
    return (xp, xs, new_k_p, new_v_p, new_ik_p, new_s_p, new_mk_p, new_mv_p, new_k_s, new_v_s, new_ik_s, new_s_s)
```

```python
import functools

import jax
import jax.numpy as jnp
import numpy as np
from jax import lax
from jax.experimental import pallas as pl
from jax.experimental.pallas import tpu as pltpu

CHUNK = 64
N_BRANCHES = 3
A_HEADS, A_KV_HEADS, HEAD_DIM = 6, 2, 128
ROT_DIM = HEAD_DIM // 4
IDX_HEADS, IDX_DIM = 8, 64
IDX_ROT_DIM = IDX_DIM // 4
TOPK_MAX = 256
B_HEADS, B_KEY_DIM, B_VAL_DIM = 6, 128, 128
C_HEADS, C_HEAD_DIM = 4, 128
ROPE_THETA = 500000.0
EPS = 1e-6

A_WIDTH = A_HEADS * HEAD_DIM
KV_WIDTH = A_KV_HEADS * HEAD_DIM
IQ_WIDTH = IDX_HEADS * IDX_DIM
B_WIDTH = B_HEADS * B_KEY_DIM
C_WIDTH = C_HEADS * C_HEAD_DIM

LANES = 128
SUBLANES = 8
VMEM_LIMIT_BYTES = 56 * 1024 * 1024

MXU_DTYPE = jnp.bfloat16
F32 = jnp.float32
I32 = jnp.int32
I16 = jnp.int16
HALF_BIAS = 1 << 15

IW_LANE = 96
NEG_BIG = -1e30
LOG2E = 1.4426950408889634
ATT_ROWS = 128
SELECT_ROWS = 256
INT_MIN = -(2 ** 31)
KEY_NEG_INF = INT_MIN + 0x00800000

OFF_AQ = 0
OFF_AK = OFF_AQ + A_WIDTH
OFF_AV = OFF_AK + KV_WIDTH
OFF_IQ = OFF_AV + KV_WIDTH
OFF_IKW = OFF_IQ + IQ_WIDTH
OFF_BQ = OFF_IKW + LANES
OFF_BF = OFF_BQ + B_WIDTH
OFF_BI = OFF_BF + B_WIDTH
OFF_BG = OFF_BI + B_WIDTH
OFF_CQ = OFF_BG + B_WIDTH
OFF_GATES = OFF_CQ + C_WIDTH


def _sigmoid(x):
    return 1.0 / (1.0 + jnp.exp(-x))


def _dot(a, b):
    return jnp.dot(a, b, preferred_element_type=F32)


def _dot_nt(a, b):
    return lax.dot_general(a, b, (((1,), (1,)), ((), ())), preferred_element_type=F32)


def _rms(x, gain):
    return x * lax.rsqrt(jnp.mean(x * x, axis=-1, keepdims=True) + EPS) * gain


def _resident(shape):
    zeros = (0,) * len(shape)
    return pl.BlockSpec(shape, lambda *_: zeros, pipeline_mode=pl.Buffered(1))


def _rope(y, cos, sin_up, sin_down, half):
    return y * cos + pltpu.roll(y, half, 1) * sin_up + pltpu.roll(y, LANES - half, 1) * sin_down


def _in_proj_kernel(x_ref, gain_ref, lb_ref, w_ref, ca_ref, sau_ref, sad_ref, ci_ref, siu_ref, sid_ref,
                    aq_ref, k32_ref, v32_ref, k16_ref, v16_ref, iq_ref, ikw_ref, ika_ref, ikb_ref,
                    hq_ref, hk_ref, hlf_ref, hv_ref, hg_ref, cq_ref, gt_ref):
    hb = _rms(x_ref[...], gain_ref[...]).astype(MXU_DTYPE)

    def proj(c0, width):
        return _dot(hb, w_ref[:, c0:c0 + width])

    ca, sau, sad = ca_ref[...], sau_ref[...], sad_ref[...]
    ci, siu, sid = ci_ref[...], siu_ref[...], sid_ref[...]

    y = proj(OFF_AQ, A_WIDTH)
    for h in range(A_HEADS):
        sl = slice(h * LANES, (h + 1) * LANES)
        aq_ref[:, sl] = (_rope(y[:, sl], ca, sau, sad, ROT_DIM // 2) * ((HEAD_DIM ** -0.5) * LOG2E)).astype(MXU_DTYPE)

    y = proj(OFF_AK, KV_WIDTH)
    for h in range(A_KV_HEADS):
        sl = slice(h * LANES, (h + 1) * LANES)
        r = _rope(y[:, sl], ca, sau, sad, ROT_DIM // 2)
        k32_ref[:, sl] = r
        k16_ref[:, sl] = r.astype(MXU_DTYPE)

    y = proj(OFF_AV, KV_WIDTH)
    v32_ref[...] = y
    v16_ref[...] = y.astype(MXU_DTYPE)

    y = proj(OFF_IQ, IQ_WIDTH)
    for j in range(IQ_WIDTH // LANES):
        sl = slice(j * LANES, (j + 1) * LANES)
        iq_ref[:, sl] = _rope(y[:, sl], ci, siu, sid, IDX_ROT_DIM // 2).astype(MXU_DTYPE)

    r = _rope(proj(OFF_IKW, LANES), ci, siu, sid, IDX_ROT_DIM // 2)
    lane = lax.broadcasted_iota(I32, r.shape, 1)
    ikw_ref[...] = jnp.where(lane < IDX_DIM, r, r * ((IDX_DIM ** -0.5) * (IDX_HEADS ** -0.5)))
    ik_low = jnp.where(lane < IDX_DIM, r, 0.0)
    ika_ref[...] = ik_low.astype(MXU_DTYPE)
    ikb_ref[...] = pltpu.roll(ik_low, IDX_DIM, 1).astype(MXU_DTYPE)

    y = proj(OFF_BQ, B_WIDTH)
    hq_ref[...] = (y * _sigmoid(y)).astype(MXU_DTYPE)
    lb = lb_ref[...]
    f = lb + (1.0 - lb) * _sigmoid(proj(OFF_BF, B_WIDTH))
    hk_ref[...] = (1.0 - f).astype(MXU_DTYPE)
    hlf_ref[...] = jnp.log(f)
    hv_ref[...] = proj(OFF_BI, B_WIDTH).astype(MXU_DTYPE)
    y = proj(OFF_BG, B_WIDTH)
    hg_ref[...] = (y * _sigmoid(y)).astype(MXU_DTYPE)

    cq_ref[...] = (proj(OFF_CQ, C_WIDTH) * (C_HEAD_DIM ** -0.5)).astype(MXU_DTYPE)

    n_gate = gt_ref.shape[1]
    for c0 in range(0, n_gate, B_WIDTH):
        gt_ref[:, c0:c0 + B_WIDTH] = _sigmoid(proj(OFF_GATES + c0, B_WIDTH)).astype(MXU_DTYPE)


def _rope_tables(pos, rot_dim, head_dim):
    half = rot_dim // 2
    inv_freq = ROPE_THETA ** (-jnp.arange(half, dtype=F32) / half)
    ang = pos.astype(F32)[:, None] * inv_freq[None, :]
    cos, sin = jnp.cos(ang), jnp.sin(ang)
    lane = np.arange(LANES) % head_dim
    idx = lane % half
    first, second = lane < half, (lane >= half) & (lane < rot_dim)
    cos_t = jnp.where(first | second, cos[:, idx], 1.0)
    sin_up = jnp.where(second, sin[:, idx], 0.0)
    sin_down = jnp.where(first, -sin[:, idx], 0.0)
    return cos_t, sin_up, sin_down


def _in_proj(x, pos, gain, lower_bound, w_packed, row_tile):
    bsz, t, d = x.shape
    n = bsz * t
    tm = row_tile
    assert n % tm == 0 and (t % tm == 0 or tm % t == 0)
    tables = _rope_tables(pos, ROT_DIM, HEAD_DIM) + _rope_tables(pos, IDX_ROT_DIM, IDX_DIM)
    table_rows = max(t, tm)
    tables = [jnp.tile(tb, (table_rows // t, 1)) for tb in tables]
    n_table_blocks = table_rows // tm
    n_gate = N_BRANCHES * d
    w_width = w_packed.shape[1]

    def rows(width):
        return pl.BlockSpec((tm, width), lambda i: (i, 0))

    table_spec = pl.BlockSpec((tm, LANES), lambda i: (i % n_table_blocks, 0))
    widths_dtypes = [
        (A_WIDTH, MXU_DTYPE), (KV_WIDTH, F32), (KV_WIDTH, F32), (KV_WIDTH, MXU_DTYPE), (KV_WIDTH, MXU_DTYPE),
        (IQ_WIDTH, MXU_DTYPE), (LANES, F32), (LANES, MXU_DTYPE), (LANES, MXU_DTYPE),
        (B_WIDTH, MXU_DTYPE), (B_WIDTH, MXU_DTYPE), (B_WIDTH, F32), (B_WIDTH, MXU_DTYPE), (B_WIDTH, MXU_DTYPE),
        (C_WIDTH, MXU_DTYPE), (n_gate, MXU_DTYPE)]
    outs = pl.pallas_call(
        _in_proj_kernel,
        grid=(n // tm,),
        in_specs=[rows(d), _resident((1, d)), _resident((1, B_WIDTH)), _resident((d, w_width))] + [table_spec] * 6,
        out_specs=[rows(w) for w, _ in widths_dtypes],
        out_shape=[jax.ShapeDtypeStruct((n, w), dt) for w, dt in widths_dtypes],
        compiler_params=pltpu.CompilerParams(dimension_semantics=("arbitrary",), vmem_limit_bytes=VMEM_LIMIT_BYTES),
        name="in_proj",
    )(x.reshape(n, d), gain.reshape(1, d), lower_bound.reshape(1, B_WIDTH), w_packed, *tables)
    return [o.reshape(bsz, t, o.shape[-1]) for o in outs]


def _pack_w_in(w, d):
    widths = (A_WIDTH, KV_WIDTH, KV_WIDTH, IQ_WIDTH, IDX_DIM, IDX_HEADS, B_WIDTH, B_WIDTH, B_WIDTH, B_WIDTH,
              C_WIDTH, N_BRANCHES * d)
    points = [int(s) for s in np.cumsum(widths)[:-1]]
    a_q, a_k, a_v, i_q, i_k, i_w, b_q, b_f, b_i, b_g, c_q, gates = jnp.split(w, points, axis=-1)
    ikw = jnp.concatenate([i_k, jnp.zeros((d, IW_LANE - IDX_DIM), w.dtype), i_w,
                           jnp.zeros((d, LANES - IW_LANE - IDX_HEADS), w.dtype)], axis=-1)
    return jnp.concatenate([a_q, a_k, a_v, i_q, ikw, b_q, b_f, b_i, b_g, c_q, gates], axis=-1).astype(MXU_DTYPE)


def _sortable_key(score):
    bits = pltpu.bitcast(score, I32)
    return jnp.where(bits < 0, (bits ^ 0x7FFFFFFF) + 1, bits)


def _lane_tiled(x, n_tiles):
    return x if n_tiles == 1 else jnp.concatenate([x] * n_tiles, axis=1)


def _dsa_kernel(aq_ref, iq_ref, ikw_ref, ika_ref, ikb_ref, k_ref, v_ref, ya_ref,
                key_sc, hi_sc, lo_sc, thr_sc, m_sc, l_sc, acc_sc, *, tq, tk, key_off, topk, row_sub, idx_bits):
    qi = pl.program_id(1)
    n_lane_tiles = tk // LANES
    n_vis = key_off + (qi + 1) * tq
    n_kb = lax.div(n_vis + (tk - 1), tk)
    row = qi * tq + lax.broadcasted_iota(I32, (tq, 1), 0)
    limit = key_off + (lax.shift_right_logical(row, CHUNK.bit_length() - 1) + 1) * CHUNK

    iw = ikw_ref[0, :, IW_LANE:IW_LANE + IDX_HEADS]

    def score_block(kb, carry):
        k0 = pl.multiple_of(kb * tk, tk)
        ika = ika_ref[0, pl.ds(k0, tk), :]
        ikb = ikb_ref[0, pl.ds(k0, tk), :]
        acc = jnp.zeros((tq, tk), F32)
        for j in range(IQ_WIDTH // LANES):
            iqt = iq_ref[0, :, j * LANES:(j + 1) * LANES]
            acc = acc + jnp.maximum(_dot_nt(iqt, ika), 0.0) * iw[:, 2 * j:2 * j + 1]
            acc = acc + jnp.maximum(_dot_nt(iqt, ikb), 0.0) * iw[:, 2 * j + 1:2 * j + 2]
        col = k0 + lax.broadcasted_iota(I32, (tq, tk), 1)
        key = jnp.where(col < limit, _sortable_key(acc), KEY_NEG_INF)
        key_sc[kb] = key
        hi_sc[kb] = lax.shift_right_arithmetic(key, 16).astype(I16)
        lo_sc[kb] = ((key & 0xFFFF) - HALF_BIAS).astype(I16)
        return carry

    lax.fori_loop(0, n_kb, score_block, 0)

    def count_rows(ref, rows, pred):
        def body(kb, acc):
            blk = ref[kb, rows, :]
            base = kb * tk
            for c in range(n_lane_tiles):
                cols = base + c * LANES + lax.broadcasted_iota(I32, (row_sub, LANES), 1)
                hit = pred(blk[:, c * LANES:(c + 1) * LANES], cols)
                acc = acc + jnp.where(hit, jnp.ones((), acc.dtype), jnp.zeros((), acc.dtype))
            return acc
        acc = lax.fori_loop(0, n_kb, body, jnp.zeros((row_sub, LANES), ref.dtype))
        return jnp.sum(acc.astype(F32), axis=1, keepdims=True)

    def bisect16(ref, rows, n_wanted):
        def bit(it, u):
            trial = u | jnp.left_shift(jnp.int32(1), 15 - it)
            cand = (trial - HALF_BIAS).astype(I16)
            n_ge = count_rows(ref, rows, lambda v, _: v >= cand)
            return jnp.where(n_ge >= n_wanted, trial, u)
        return lax.fori_loop(0, 16, bit, jnp.zeros((row_sub, LANES), I32))

    for rs in range(tq // row_sub):
        rows = pl.ds(rs * row_sub, row_sub)

        hi_u = bisect16(hi_sc, rows, topk)
        hi_star = (hi_u - HALF_BIAS).astype(I16)
        n_hi_gt = count_rows(hi_sc, rows, lambda v, _: v > hi_star)

        def keep_bucket(kb, carry):
            in_bucket = hi_sc[kb, rows, :] == _lane_tiled(hi_star, n_lane_tiles)
            lo_sc[kb, rows, :] = jnp.where(in_bucket, lo_sc[kb, rows, :], jnp.full((), -HALF_BIAS, I16))
            return carry

        lax.fori_loop(0, n_kb, keep_bucket, 0)
        lo_u = bisect16(lo_sc, rows, topk - n_hi_gt)
        lo_star = (lo_u - HALF_BIAS).astype(I16)
        thr = jnp.left_shift(hi_u - HALF_BIAS, 16) | lo_u
        n_gt = n_hi_gt + count_rows(lo_sc, rows, lambda v, _: v > lo_star)
        n_ge = n_hi_gt + count_rows(lo_sc, rows, lambda v, _: v >= lo_star)
        need = topk - n_gt
        has_tie = jnp.where((n_ge > topk) & (thr[:, :1] > KEY_NEG_INF), 1.0, 0.0)

        @pl.when(jnp.max(has_tie) > 0.0)
        def _():
            def idx_bit(it, cut):
                trial = cut | jnp.left_shift(jnp.int32(1), idx_bits - 1 - it)
                n_before = count_rows(key_sc, rows, lambda k, c: (k == thr) & (c < trial))
                return jnp.where(n_before < need, trial, cut)

            cut = lax.fori_loop(0, idx_bits, idx_bit, jnp.zeros((row_sub, LANES), I32))

            def demote(kb, carry):
                blk = key_sc[kb, rows, :]
                cols = kb * tk + lax.broadcasted_iota(I32, (row_sub, tk), 1)
                drop = (blk == _lane_tiled(thr, n_lane_tiles)) & (cols > _lane_tiled(cut, n_lane_tiles))
                key_sc[kb, rows, :] = jnp.where(drop, _lane_tiled(thr, n_lane_tiles) - 1, blk)
                return carry

            lax.fori_loop(0, n_kb, demote, 0)

        thr_sc[rows, :] = jnp.maximum(thr, KEY_NEG_INF + 1)

    m_sc[...] = jnp.full(m_sc.shape, NEG_BIG, F32)
    l_sc[...] = jnp.zeros(l_sc.shape, F32)
    acc_sc[...] = jnp.zeros(acc_sc.shape, F32)
    group = A_HEADS // A_KV_HEADS

    ones_blk = jnp.ones((tk, LANES), MXU_DTYPE)
    row_att = min(tq, ATT_ROWS)

    def attend_block(kb, carry):
        k0 = pl.multiple_of(kb * tk, tk)
        bias_sc = key_sc.at[kb]
        bias = jnp.where(bias_sc[...] >= _lane_tiled(thr_sc[...], n_lane_tiles), 0.0, NEG_BIG)
        bias_sc[...] = pltpu.bitcast(bias, I32)
        for g in range(A_KV_HEADS):
            kblk = k_ref[0, pl.ds(k0, tk), g * LANES:(g + 1) * LANES]
            vaug = jnp.concatenate([v_ref[0, pl.ds(k0, tk), g * LANES:(g + 1) * LANES], ones_blk], axis=1)
            for r in range(group):
                h = g * group + r
                for rt in range(tq // row_att):
                    rows = pl.ds(rt * row_att, row_att)
                    q = aq_ref[0, rows, h * LANES:(h + 1) * LANES]
                    s = _dot_nt(q, kblk) + pltpu.bitcast(bias_sc[rows, :], F32)
                    m_prev = m_sc[h, rows, :]
                    m_new = jnp.maximum(m_prev, jnp.max(s, axis=1, keepdims=True))
                    alpha = jnp.exp2(m_prev - m_new)
                    p = jnp.exp2(s - _lane_tiled(m_new, n_lane_tiles))
                    pv = _dot(p.astype(MXU_DTYPE), vaug)
                    acc_sc[h, rows, :] = alpha * acc_sc[h, rows, :] + pv[:, :LANES]
                    l_sc[h, rows, :] = alpha * l_sc[h, rows, :] + pv[:, LANES:]
                    m_sc[h, rows, :] = m_new
        return carry

    lax.fori_loop(0, n_kb, attend_block, 0)
    for h in range(A_HEADS):
        ya_ref[0, :, h * LANES:(h + 1) * LANES] = (acc_sc[h] / l_sc[h]).astype(ya_ref.dtype)


def _dsa(aq, iq, ikw, ika, ikb, k16, v16, *, n_valid_keys, q_tile, key_tile):
    bsz, t, _ = aq.shape
    lp = k16.shape[1]
    tq, tk = q_tile, key_tile
    assert t % tq == 0 and tq % CHUNK == 0 and lp % tk == 0 and tk % LANES == 0 and n_valid_keys <= lp
    topk = min(TOPK_MAX, n_valid_keys // 4)
    assert tk >= topk
    row_sub = min(tq, SELECT_ROWS)
    kernel = functools.partial(
        _dsa_kernel, tq=tq, tk=tk, key_off=n_valid_keys - t, topk=topk, row_sub=row_sub,
        idx_bits=max(1, int(lp - 1).bit_length()))

    def q_spec(width):
        return pl.BlockSpec((1, tq, width), lambda b, i: (b, i, 0))

    def key_spec(width):
        return pl.BlockSpec((1, lp, width), lambda b, i: (b, 0, 0), pipeline_mode=pl.Buffered(1))

    return pl.pallas_call(
        kernel,
        grid=(bsz, t // tq),
        in_specs=[q_spec(A_WIDTH), q_spec(IQ_WIDTH), q_spec(LANES), key_spec(LANES), key_spec(LANES),
                  key_spec(KV_WIDTH), key_spec(KV_WIDTH)],
        out_specs=q_spec(A_WIDTH),
        out_shape=jax.ShapeDtypeStruct((bsz, t, A_WIDTH), MXU_DTYPE),
        scratch_shapes=[pltpu.VMEM((lp // tk, tq, tk), I32), pltpu.VMEM((lp // tk, tq, tk), I16),
                        pltpu.VMEM((lp // tk, tq, tk), I16), pltpu.VMEM((tq, LANES), I32),
                        pltpu.VMEM((A_HEADS, tq, LANES), F32), pltpu.VMEM((A_HEADS, tq, LANES), F32),
                        pltpu.VMEM((A_HEADS, tq, LANES), F32)],
        compiler_params=pltpu.CompilerParams(dimension_semantics=("arbitrary", "arbitrary"),
                                             vmem_limit_bytes=VMEM_LIMIT_BYTES),
        name="dsa",
    )(aq, iq, ikw, ika, ikb, k16, v16)


SUB = SUBLANES
N_SUB = CHUNK // SUB
INTER_ROWS = SUB * (N_SUB * (N_SUB - 1) // 2)


def _split3(x):
    hi = x.astype(MXU_DTYPE)
    r1 = x - hi.astype(F32)
    mid = r1.astype(MXU_DTYPE)
    lo = (r1 - mid.astype(F32)).astype(MXU_DTYPE)
    return hi, mid, lo


def _hgrn_kernel(q_ref, k_ref, lf_ref, v_ref, g_ref, gain_ref, s0_ref, yb_ref, sout_ref, st_sc):
    c = pl.program_id(1)

    @pl.when(c == 0)
    def _():
        for h in range(B_HEADS):
            st_sc[h] = s0_ref[0, h].T

    q = q_ref[0].astype(F32)
    k = k_ref[0].astype(F32)
    v16 = v_ref[0]
    lf = lf_ref[0]

    ti = lax.broadcasted_iota(I32, (CHUNK, CHUNK), 0)
    si = lax.broadcasted_iota(I32, (CHUNK, CHUNK), 1)
    tri = jnp.where(si <= ti, 1.0, 0.0).astype(MXU_DTYPE)
    hi, mid, lo = _split3(lf)
    b = _dot(tri, hi) + _dot(tri, mid) + _dot(tri, lo)

    def row_bcast(x, r, n):
        return jnp.broadcast_to(x[r:r + 1, :], (n, x.shape[1]))

    b_start = jnp.concatenate(
        [jnp.zeros((SUB, B_WIDTH), F32)] + [row_bcast(b, j * SUB - 1, SUB) for j in range(1, N_SUB)], axis=0)
    q_hat = (q * jnp.exp(b - b_start)).astype(MXU_DTYPE)
    q_state = (q * jnp.exp(b)).astype(MXU_DTYPE)
    b_end = b[CHUNK - 1:CHUNK, :]
    k_end = k * jnp.exp(b_end - b)

    k_hat = jnp.concatenate(
        [k[:i * SUB] * jnp.exp(row_bcast(b, i * SUB - 1, i * SUB) - b[:i * SUB]) for i in range(1, N_SUB)],
        axis=0).astype(MXU_DTYPE)
    v_hat = jnp.concatenate([v16[:i * SUB] for i in range(1, N_SUB)], axis=0)

    q3 = q.reshape(N_SUB, SUB, B_WIDTH)
    k3 = k.reshape(N_SUB, SUB, B_WIDTH)
    b3 = b.reshape(N_SUB, SUB, B_WIDTH)
    intra = []
    for s in range(SUB):
        kp = jnp.broadcast_to(k3[:, s:s + 1, :], q3.shape)
        bp = jnp.broadcast_to(b3[:, s:s + 1, :], q3.shape)
        x = q3 * kp * jnp.exp(jnp.minimum(b3 - bp, 0.0))
        intra.append(x.reshape(CHUNK, B_WIDTH).astype(MXU_DTYPE))

    pr = lax.broadcasted_iota(I32, (SUB * LANES, CHUNK), 0)
    pc = lax.broadcasted_iota(I32, (SUB * LANES, CHUNK), 1)
    place = jnp.where(lax.shift_right_logical(pr, LANES.bit_length() - 1) == (pc & (SUB - 1)), 1.0, 0.0
                      ).astype(MXU_DTYPE)
    t_sub = lax.shift_right_logical(ti, SUB.bit_length() - 1)
    intra_ok = (t_sub == lax.shift_right_logical(si, SUB.bit_length() - 1)) & (si <= ti)
    tt = lax.broadcasted_iota(I32, (CHUNK, INTER_ROWS), 0)
    cc = lax.broadcasted_iota(I32, (CHUNK, INTER_ROWS), 1)
    seg = jnp.ones((CHUNK, INTER_ROWS), I32)
    for i in range(2, N_SUB):
        seg = seg + jnp.where(cc >= SUB * (i * (i - 1) // 2), 1, 0)
    inter_ok = lax.shift_right_logical(tt, SUB.bit_length() - 1) == seg

    gain = gain_ref[...]
    for h in range(B_HEADS):
        sl = slice(h * LANES, (h + 1) * LANES)
        st = st_sc[h]
        x_cat = jnp.concatenate([x[:, sl] for x in intra], axis=1)
        p_intra = jnp.where(intra_ok, _dot(x_cat, place), 0.0).astype(MXU_DTYPE)
        p_inter = jnp.where(inter_ok, _dot_nt(q_hat[:, sl], k_hat[:, sl]), 0.0).astype(MXU_DTYPE)
        o = (_dot(p_inter, v_hat[:, sl]) + _dot(p_intra, v16[:, sl])
             + _dot_nt(q_state[:, sl], st.astype(MXU_DTYPE)))
        st_new = st * jnp.exp(b_end[:, sl]) + _dot(v16[:, sl].astype(F32).T.astype(MXU_DTYPE),
                                                    k_end[:, sl].astype(MXU_DTYPE))
        st_sc[h] = st_new
        y = _rms(o, gain[:, sl]) * g_ref[0, :, sl].astype(F32)
        yb_ref[0, :, sl] = y.astype(yb_ref.dtype)

    @pl.when(c == pl.num_programs(1) - 1)
    def _():
        for h in range(B_HEADS):
            sout_ref[0, h] = st_sc[h].T


def _hgrn(hq, hk, hlf, hv, hg, norm_gain, state0):
    bsz, t, _ = hq.shape
    assert t % CHUNK == 0

    def seq_spec():
        return pl.BlockSpec((1, CHUNK, B_WIDTH), lambda b, c: (b, c, 0))

    state_spec = pl.BlockSpec((1, B_HEADS, B_KEY_DIM, B_VAL_DIM), lambda b, c: (b, 0, 0, 0))
    return pl.pallas_call(
        _hgrn_kernel,
        grid=(bsz, t // CHUNK),
        in_specs=[seq_spec()] * 5 + [pl.BlockSpec((1, B_WIDTH), lambda b, c: (0, 0)), state_spec],
        out_specs=[seq_spec(), state_spec],
        out_shape=[jax.ShapeDtypeStruct((bsz, t, B_WIDTH), MXU_DTYPE),
                   jax.ShapeDtypeStruct((bsz, B_HEADS, B_KEY_DIM, B_VAL_DIM), F32)],
        scratch_shapes=[pltpu.VMEM((B_HEADS, B_VAL_DIM, B_KEY_DIM), F32)],
        compiler_params=pltpu.CompilerParams(dimension_semantics=("arbitrary", "arbitrary"),
                                             vmem_limit_bytes=VMEM_LIMIT_BYTES),
        name="hgrn",
    )(hq, hk, hlf, hv, hg, norm_gain.reshape(1, B_WIDTH), state0)


def _memory_kv_kernel(mem_ref, gain_ref, w_ref, k32_ref, v32_ref, k16_ref, v16_ref):
    hb = _rms(mem_ref[0], gain_ref[...]).astype(MXU_DTYPE)
    mk = _dot(hb, w_ref[:, :C_WIDTH])
    mv = _dot(hb, w_ref[:, C_WIDTH:])
    k32_ref[0] = mk
    v32_ref[0] = mv
    k16_ref[0] = mk.astype(MXU_DTYPE)
    v16_ref[0] = mv.astype(MXU_DTYPE)


def _memory_kv(mem, gain, w16):
    bsz, n, d = mem.shape
    spec = pl.BlockSpec((1, n, C_WIDTH), lambda b: (b, 0, 0))
    return pl.pallas_call(
        _memory_kv_kernel,
        grid=(bsz,),
        in_specs=[pl.BlockSpec((1, n, d), lambda b: (b, 0, 0)), _resident((1, d)), _resident((d, 2 * C_WIDTH))],
        out_specs=[spec] * 4,
        out_shape=[jax.ShapeDtypeStruct((bsz, n, C_WIDTH), dt) for dt in (F32, F32, MXU_DTYPE, MXU_DTYPE)],
        compiler_params=pltpu.CompilerParams(dimension_semantics=("arbitrary",), vmem_limit_bytes=VMEM_LIMIT_BYTES),
        name="memory_kv",
    )(mem, gain.reshape(1, d), w16)


def _merge_ffn_kernel(x_ref, ya_ref, yb_ref, cq_ref, mk_ref, mv_ref, gt_ref, wa_ref, wb_ref, wc_ref, wo_ref,
                      g_mix_ref, g_pre_ref, g_post_ref, wu_ref, wd_ref, out_ref):
    d = x_ref.shape[-1]
    heads = []
    for h in range(C_HEADS):
        sl = slice(h * C_HEAD_DIM, (h + 1) * C_HEAD_DIM)
        s = _dot_nt(cq_ref[0, :, sl], mk_ref[0, :, sl])
        p = jnp.exp(s - jnp.max(s, axis=-1, keepdims=True))
        p = p / jnp.sum(p, axis=-1, keepdims=True)
        heads.append(_dot(p.astype(MXU_DTYPE), mv_ref[0, :, sl]).astype(MXU_DTYPE))
    y_c = jnp.concatenate(heads, axis=1)

    merged = (gt_ref[0, :, 0:d].astype(F32) * _dot(ya_ref[0], wa_ref[...])
              + gt_ref[0, :, d:2 * d].astype(F32) * _dot(yb_ref[0], wb_ref[...])
              + gt_ref[0, :, 2 * d:3 * d].astype(F32) * _dot(y_c, wc_ref[...]))
    x1 = x_ref[0] + _rms(_dot(merged.astype(MXU_DTYPE), wo_ref[...]), g_mix_ref[...])
    u = jnp.maximum(_dot(_rms(x1, g_pre_ref[...]).astype(MXU_DTYPE), wu_ref[...]), 0.0)
    down = _dot((u * u).astype(MXU_DTYPE), wd_ref[...])
    out_ref[0] = x1 + _rms(down, g_post_ref[...])


def _merge_ffn(x, ya, yb, cq, mk16, mv16, gates, weights, gains, row_tile):
    bsz, t, d = x.shape
    tm = row_tile
    assert t % tm == 0
    n_mem = mk16.shape[1]

    def rows(width):
        return pl.BlockSpec((1, tm, width), lambda b, i: (b, i, 0))

    mem_spec = pl.BlockSpec((1, n_mem, C_WIDTH), lambda b, i: (b, 0, 0))
    return pl.pallas_call(
        _merge_ffn_kernel,
        grid=(bsz, t // tm),
        in_specs=[rows(d), rows(A_WIDTH), rows(B_WIDTH), rows(C_WIDTH), mem_spec, mem_spec, rows(N_BRANCHES * d)]
        + [_resident(w.shape) for w in weights[:4]] + [_resident((1, d))] * 3
        + [_resident(w.shape) for w in weights[4:]],
        out_specs=rows(d),
        out_shape=jax.ShapeDtypeStruct((bsz, t, d), F32),
        compiler_params=pltpu.CompilerParams(dimension_semantics=("arbitrary", "arbitrary"),
                                             vmem_limit_bytes=VMEM_LIMIT_BYTES),
        name="merge_ffn",
    )(x, ya, yb, cq, mk16, mv16, gates, *weights[:4], *[g.reshape(1, d) for g in gains], *weights[4:])


def _layer(x, pos, lower_bound, lp, *, cache, state0, mem16, row_tile, q_tile, key_tile):
    bsz, t, d = x.shape
    (aq, k32, v32, k16, v16, iq, ikw, ika, ikb, hq, hk, hlf, hv, hg, cq, gates) = _in_proj(
        x, pos, lp["pre_mix_gain"], lower_bound, lp["w_in"], min(row_tile, bsz * t))

    keys = [ika, ikb, k16, v16]
    if cache is not None:
        keys = [jnp.concatenate([c, kn], axis=1) for c, kn in zip(cache, keys)]
    n_keys = keys[0].shape[1]
    pad = (-n_keys) % key_tile
    if pad:
        keys = [jnp.pad(kk, ((0, 0), (0, pad), (0, 0))) for kk in keys]
    ya = _dsa(aq, iq, ikw, *keys, n_valid_keys=n_keys, q_tile=q_tile, key_tile=key_tile)

    yb, s_new = _hgrn(hq, hk, hlf, hv, hg, lp["hgrn_norm_gain"], state0)

    out = _merge_ffn(x, ya, yb, cq, mem16[0], mem16[1], gates,
                     [lp["w_out_a"], lp["w_out_b"], lp["w_out_c"], lp["w_out"], lp["w_up"], lp["w_down"]],
                     [lp["post_mix_gain"], lp["pre_ffn_gain"], lp["post_ffn_gain"]], row_tile=min(row_tile, t))
    new_k = k32.reshape(bsz, t, A_KV_HEADS, HEAD_DIM)
    new_v = v32.reshape(bsz, t, A_KV_HEADS, HEAD_DIM)
    return out, (new_k, new_v, ikw[..., :IDX_DIM], s_new)


def _tiles(t):
    return 256, min(256, t)


def kernel(x_prompt, x_sample, cache_k, cache_v, cache_idx_k, state_hgrn, cache_mem_k, cache_mem_v, mem_prompt, w_in, w_mem_kv, mem_norm_gain, hgrn_lb_logits, hgrn_norm_gain, w_out_a, w_out_b, w_out_c, w_out, pre_mix_gain, post_mix_gain, pre_ffn_gain, post_ffn_gain, w_up, w_down):
    depth = w_in.shape[0]
    d = x_prompt.shape[-1]
    bp, tp, _ = x_prompt.shape
    bs, ts, _ = x_sample.shape
    past = cache_k.shape[2]
    lower_bounds = jnp.cumsum(jax.nn.softmax(hgrn_lb_logits.astype(F32), axis=0), axis=0)
    pos_p = jnp.arange(tp, dtype=I32)
    pos_s = past + jnp.arange(ts, dtype=I32)
    bf = lambda a: a.astype(MXU_DTYPE)

    xp, xs = x_prompt, x_sample
    outs_p, outs_s = [], []
    for l in range(depth):
        lp = {
            "w_in": _pack_w_in(w_in[l], d), "hgrn_norm_gain": hgrn_norm_gain[l],
            "w_out_a": bf(w_out_a[l]), "w_out_b": bf(w_out_b[l]), "w_out_c": bf(w_out_c[l]), "w_out": bf(w_out[l]),
            "pre_mix_gain": pre_mix_gain[l], "post_mix_gain": post_mix_gain[l], "pre_ffn_gain": pre_ffn_gain[l],
            "post_ffn_gain": post_ffn_gain[l], "w_up": bf(w_up[l]), "w_down": bf(w_down[l]),
        }
        mk32, mv32, mk16, mv16 = _memory_kv(mem_prompt, mem_norm_gain[l], bf(w_mem_kv[l]))
        n_mem = mem_prompt.shape[1]
        row_tile, q_tile = _tiles(tp)
        xp, st_p = _layer(xp, pos_p, lower_bounds[l], lp, cache=None,
                          state0=jnp.zeros((bp, B_HEADS, B_KEY_DIM, B_VAL_DIM), F32), mem16=(mk16, mv16),
                          row_tile=row_tile, q_tile=q_tile, key_tile=512)
        outs_p.append(st_p + (mk32.reshape(bp, n_mem, C_HEADS, C_HEAD_DIM), mv32.reshape(bp, n_mem, C_HEADS, C_HEAD_DIM)))

        cik = cache_idx_k[l]
        zeros_ik = jnp.zeros_like(cik)
        cache = (bf(jnp.concatenate([cik, zeros_ik], axis=-1)), bf(jnp.concatenate([zeros_ik, cik], axis=-1)),
                 bf(cache_k[l].reshape(bs, past, KV_WIDTH)), bf(cache_v[l].reshape(bs, past, KV_WIDTH)))
        mem16 = (bf(cache_mem_k[l].reshape(bs, -1, C_WIDTH)), bf(cache_mem_v[l].reshape(bs, -1, C_WIDTH)))
        row_tile, q_tile = _tiles(ts)
        xs, st_s = _layer(xs, pos_s, lower_bounds[l], lp, cache=cache, state0=state_hgrn[l].astype(F32),
                          mem16=mem16, row_tile=row_tile, q_tile=q_tile, key_tile=384)
        outs_s.append(st_s)

    new_k_p, new_v_p, new_ik_p, new_s_p, new_mk_p, new_mv_p = [jnp.stack(a) for a in zip(*outs_p)]
    new_k_s, new_v_s, new_ik_s, new_s_s = [jnp.stack(a) for a in zip(*outs_s)]
    return (xp, xs, new_k_p, new_v_p, new_ik_p, new_s_p, new_mk_p, new_mv_p, new_k_s, new_v_s, new_ik_s, new_s_s)
```

```python
import functools

import jax
import jax.numpy as jnp
import numpy as np
from jax import lax
from jax.experimental import pallas as pl
from jax.experimental.pallas import tpu as pltpu

CHUNK = 64
N_BRANCHES = 3
A_HEADS, A_KV_HEADS, HEAD_DIM = 6, 2, 128
ROT_DIM = HEAD_DIM // 4
IDX_HEADS, IDX_DIM = 8, 64
IDX_ROT_DIM = IDX_DIM // 4
TOPK_MAX = 256
B_HEADS, B_KEY_DIM, B_VAL_DIM = 6, 128, 128
C_HEADS, C_HEAD_DIM = 4, 128
ROPE_THETA = 500000.0
EPS = 1e-6

A_WIDTH = A_HEADS * HEAD_DIM
KV_WIDTH = A_KV_HEADS * HEAD_DIM
IQ_WIDTH = IDX_HEADS * IDX_DIM
B_WIDTH = B_HEADS * B_KEY_DIM
C_WIDTH = C_HEADS * C_HEAD_DIM

LANES = 128
SUBLANES = 8
VMEM_LIMIT_BYTES = 56 * 1024 * 1024

MXU_DTYPE = jnp.bfloat16
F32 = jnp.float32
I32 = jnp.int32

IW_LANE = 96
NEG_BIG = -1e30
LOG2E = 1.4426950408889634
ATT_ROWS = 128
SELECT_ROWS = 128
INT_MIN = -(2 ** 31)
KEY_NEG_INF = INT_MIN + 0x00800000

OFF_AQ = 0
OFF_AK = OFF_AQ + A_WIDTH
OFF_AV = OFF_AK + KV_WIDTH
OFF_IQ = OFF_AV + KV_WIDTH
OFF_IKW = OFF_IQ + IQ_WIDTH
OFF_BQ = OFF_IKW + LANES
OFF_BF = OFF_BQ + B_WIDTH
OFF_BI = OFF_BF + B_WIDTH
OFF_BG = OFF_BI + B_WIDTH
OFF_CQ = OFF_BG + B_WIDTH
OFF_GATES = OFF_CQ + C_WIDTH


def _sigmoid(x):
    return 1.0 / (1.0 + jnp.exp(-x))


def _dot(a, b):
    return jnp.dot(a, b, preferred_element_type=F32)


def _dot_nt(a, b):
    return lax.dot_general(a, b, (((1,), (1,)), ((), ())), preferred_element_type=F32)


def _rms(x, gain):
    return x * lax.rsqrt(jnp.mean(x * x, axis=-1, keepdims=True) + EPS) * gain


def _resident(shape):
    zeros = (0,) * len(shape)
    return pl.BlockSpec(shape, lambda *_: zeros, pipeline_mode=pl.Buffered(1))


def _rope(y, cos, sin_up, sin_down, half):
    return y * cos + pltpu.roll(y, half, 1) * sin_up + pltpu.roll(y, LANES - half, 1) * sin_down


def _in_proj_kernel(x_ref, gain_ref, lb_ref, w_ref, ca_ref, sau_ref, sad_ref, ci_ref, siu_ref, sid_ref,
                    aq_ref, k32_ref, v32_ref, k16_ref, v16_ref, iq_ref, ikw_ref, ika_ref, ikb_ref,
                    hq_ref, hk_ref, hlf_ref, hv_ref, hg_ref, cq_ref, gt_ref):
    hb = _rms(x_ref[...], gain_ref[...]).astype(MXU_DTYPE)

    def proj(c0, width):
        return _dot(hb, w_ref[:, c0:c0 + width])

    ca, sau, sad = ca_ref[...], sau_ref[...], sad_ref[...]
    ci, siu, sid = ci_ref[...], siu_ref[...], sid_ref[...]

    y = proj(OFF_AQ, A_WIDTH)
    for h in range(A_HEADS):
        sl = slice(h * LANES, (h + 1) * LANES)
        aq_ref[:, sl] = (_rope(y[:, sl], ca, sau, sad, ROT_DIM // 2) * ((HEAD_DIM ** -0.5) * LOG2E)).astype(MXU_DTYPE)

    y = proj(OFF_AK, KV_WIDTH)
    for h in range(A_KV_HEADS):
        sl = slice(h * LANES, (h + 1) * LANES)
        r = _rope(y[:, sl], ca, sau, sad, ROT_DIM // 2)
        k32_ref[:, sl] = r
        k16_ref[:, sl] = r.astype(MXU_DTYPE)

    y = proj(OFF_AV, KV_WIDTH)
    v32_ref[...] = y
    v16_ref[...] = y.astype(MXU_DTYPE)

    y = proj(OFF_IQ, IQ_WIDTH)
    for j in range(IQ_WIDTH // LANES):
        sl = slice(j * LANES, (j + 1) * LANES)
        iq_ref[:, sl] = _rope(y[:, sl], ci, siu, sid, IDX_ROT_DIM // 2).astype(MXU_DTYPE)

    r = _rope(proj(OFF_IKW, LANES), ci, siu, sid, IDX_ROT_DIM // 2)
    lane = lax.broadcasted_iota(I32, r.shape, 1)
    ikw_ref[...] = jnp.where(lane < IDX_DIM, r, r * ((IDX_DIM ** -0.5) * (IDX_HEADS ** -0.5)))
    ik_low = jnp.where(lane < IDX_DIM, r, 0.0)
    ika_ref[...] = ik_low.astype(MXU_DTYPE)
    ikb_ref[...] = pltpu.roll(ik_low, IDX_DIM, 1).astype(MXU_DTYPE)

    y = proj(OFF_BQ, B_WIDTH)
    hq_ref[...] = (y * _sigmoid(y)).astype(MXU_DTYPE)
    lb = lb_ref[...]
    f = lb + (1.0 - lb) * _sigmoid(proj(OFF_BF, B_WIDTH))
    hk_ref[...] = (1.0 - f).astype(MXU_DTYPE)
    hlf_ref[...] = jnp.log(f)
    hv_ref[...] = proj(OFF_BI, B_WIDTH).astype(MXU_DTYPE)
    y = proj(OFF_BG, B_WIDTH)
    hg_ref[...] = (y * _sigmoid(y)).astype(MXU_DTYPE)

    cq_ref[...] = (proj(OFF_CQ, C_WIDTH) * (C_HEAD_DIM ** -0.5)).astype(MXU_DTYPE)

    n_gate = gt_ref.shape[1]
    for c0 in range(0, n_gate, B_WIDTH):
        gt_ref[:, c0:c0 + B_WIDTH] = _sigmoid(proj(OFF_GATES + c0, B_WIDTH)).astype(MXU_DTYPE)


def _rope_tables(pos, rot_dim, head_dim):
    half = rot_dim // 2
    inv_freq = ROPE_THETA ** (-jnp.arange(half, dtype=F32) / half)
    ang = pos.astype(F32)[:, None] * inv_freq[None, :]
    cos, sin = jnp.cos(ang), jnp.sin(ang)
    lane = np.arange(LANES) % head_dim
    idx = lane % half
    first, second = lane < half, (lane >= half) & (lane < rot_dim)
    cos_t = jnp.where(first | second, cos[:, idx], 1.0)
    sin_up = jnp.where(second, sin[:, idx], 0.0)
    sin_down = jnp.where(first, -sin[:, idx], 0.0)
    return cos_t, sin_up, sin_down


def _in_proj(x, pos, gain, lower_bound, w_packed, row_tile):
    bsz, t, d = x.shape
    n = bsz * t
    tm = row_tile
    assert n % tm == 0 and (t % tm == 0 or tm % t == 0)
    tables = _rope_tables(pos, ROT_DIM, HEAD_DIM) + _rope_tables(pos, IDX_ROT_DIM, IDX_DIM)
    table_rows = max(t, tm)
    tables = [jnp.tile(tb, (table_rows // t, 1)) for tb in tables]
    n_table_blocks = table_rows // tm
    n_gate = N_BRANCHES * d
    w_width = w_packed.shape[1]

    def rows(width):
        return pl.BlockSpec((tm, width), lambda i: (i, 0))

    table_spec = pl.BlockSpec((tm, LANES), lambda i: (i % n_table_blocks, 0))
    widths_dtypes = [
        (A_WIDTH, MXU_DTYPE), (KV_WIDTH, F32), (KV_WIDTH, F32), (KV_WIDTH, MXU_DTYPE), (KV_WIDTH, MXU_DTYPE),
        (IQ_WIDTH, MXU_DTYPE), (LANES, F32), (LANES, MXU_DTYPE), (LANES, MXU_DTYPE),
        (B_WIDTH, MXU_DTYPE), (B_WIDTH, MXU_DTYPE), (B_WIDTH, F32), (B_WIDTH, MXU_DTYPE), (B_WIDTH, MXU_DTYPE),
        (C_WIDTH, MXU_DTYPE), (n_gate, MXU_DTYPE)]
    outs = pl.pallas_call(
        _in_proj_kernel,
        grid=(n // tm,),
        in_specs=[rows(d), _resident((1, d)), _resident((1, B_WIDTH)), _resident((d, w_width))] + [table_spec] * 6,
        out_specs=[rows(w) for w, _ in widths_dtypes],
        out_shape=[jax.ShapeDtypeStruct((n, w), dt) for w, dt in widths_dtypes],
        compiler_params=pltpu.CompilerParams(dimension_semantics=("arbitrary",), vmem_limit_bytes=VMEM_LIMIT_BYTES),
        name="in_proj",
    )(x.reshape(n, d), gain.reshape(1, d), lower_bound.reshape(1, B_WIDTH), w_packed, *tables)
    return [o.reshape(bsz, t, o.shape[-1]) for o in outs]


def _pack_w_in(w, d):
    widths = (A_WIDTH, KV_WIDTH, KV_WIDTH, IQ_WIDTH, IDX_DIM, IDX_HEADS, B_WIDTH, B_WIDTH, B_WIDTH, B_WIDTH,
              C_WIDTH, N_BRANCHES * d)
    points = [int(s) for s in np.cumsum(widths)[:-1]]
    a_q, a_k, a_v, i_q, i_k, i_w, b_q, b_f, b_i, b_g, c_q, gates = jnp.split(w, points, axis=-1)
    ikw = jnp.concatenate([i_k, jnp.zeros((d, IW_LANE - IDX_DIM), w.dtype), i_w,
                           jnp.zeros((d, LANES - IW_LANE - IDX_HEADS), w.dtype)], axis=-1)
    return jnp.concatenate([a_q, a_k, a_v, i_q, ikw, b_q, b_f, b_i, b_g, c_q, gates], axis=-1).astype(MXU_DTYPE)


def _sortable_key(score):
    bits = pltpu.bitcast(score, I32)
    return jnp.where(bits < 0, (bits ^ 0x7FFFFFFF) + 1, bits)


def _lane_tiled(x, n_tiles):
    return x if n_tiles == 1 else jnp.concatenate([x] * n_tiles, axis=1)


def _dsa_kernel(aq_ref, iq_ref, ikw_ref, ika_ref, ikb_ref, k_ref, v_ref, ya_ref,
                key_sc, thr_sc, m_sc, l_sc, acc_sc, *, tq, tk, key_off, topk, row_sub):
    qi = pl.program_id(1)
    n_lane_tiles = tk // LANES
    n_vis = key_off + (qi + 1) * tq
    n_kb = lax.div(n_vis + (tk - 1), tk)
    row = qi * tq + lax.broadcasted_iota(I32, (tq, 1), 0)
    limit = key_off + (lax.shift_right_logical(row, CHUNK.bit_length() - 1) + 1) * CHUNK

    iw = ikw_ref[0, :, IW_LANE:IW_LANE + IDX_HEADS]

    def score_block(kb, carry):
        k0 = pl.multiple_of(kb * tk, tk)
        ika = ika_ref[0, pl.ds(k0, tk), :]
        ikb = ikb_ref[0, pl.ds(k0, tk), :]
        acc = jnp.zeros((tq, tk), F32)
        for j in range(IQ_WIDTH // LANES):
            iqt = iq_ref[0, :, j * LANES:(j + 1) * LANES]
            acc = acc + jnp.maximum(_dot_nt(iqt, ika), 0.0) * iw[:, 2 * j:2 * j + 1]
            acc = acc + jnp.maximum(_dot_nt(iqt, ikb), 0.0) * iw[:, 2 * j + 1:2 * j + 2]
        col = k0 + lax.broadcasted_iota(I32, (tq, tk), 1)
        key_sc[kb] = jnp.where(col < limit, _sortable_key(acc), KEY_NEG_INF)
        return carry

    lax.fori_loop(0, n_kb, score_block, 0)

    def count_rows(rows, pred):
        def body(kb, acc):
            blk = key_sc[kb, rows, :]
            for c in range(n_lane_tiles):
                acc = acc + jnp.where(pred(blk[:, c * LANES:(c + 1) * LANES]), 1, 0)
            return acc
        acc = lax.fori_loop(0, n_kb, body, jnp.zeros((row_sub, LANES), I32))
        return jnp.sum(acc.astype(F32), axis=1, keepdims=True)

    for rs in range(tq // row_sub):
        rows = pl.ds(rs * row_sub, row_sub)

        def key_bit(it, u):
            trial = u | jnp.left_shift(jnp.int32(1), 31 - it)
            cand = trial ^ INT_MIN
            n_ge = count_rows(rows, lambda k: k >= cand)
            return jnp.where(n_ge >= topk, trial, u)

        u = lax.fori_loop(0, 32, key_bit, jnp.zeros((row_sub, LANES), I32))
        thr = u ^ INT_MIN
        n_gt = count_rows(rows, lambda k: k > thr)
        n_ge = count_rows(rows, lambda k: k >= thr)
        has_tie = jnp.where((n_ge > topk) & (thr[:, :1] > KEY_NEG_INF), 1.0, 0.0)

        @pl.when(jnp.max(has_tie) > 0.0)
        def _():
            need = _lane_tiled(jnp.broadcast_to(topk - n_gt, (row_sub, LANES)), n_lane_tiles)
            thr_t = _lane_tiled(thr, n_lane_tiles)
            ri = lax.broadcasted_iota(I32, (tk, tk), 0)
            ci = lax.broadcasted_iota(I32, (tk, tk), 1)
            upper = jnp.where(ri <= ci, 1.0, 0.0).astype(MXU_DTYPE)
            ones = jnp.ones((tk, LANES), MXU_DTYPE)

            def demote(kb, seen):
                blk = key_sc[kb, rows, :]
                tie = blk == thr_t
                t16 = jnp.where(tie, 1.0, 0.0).astype(MXU_DTYPE)
                rank = _dot(t16, upper) + _lane_tiled(seen, n_lane_tiles)
                key_sc[kb, rows, :] = jnp.where(tie & (rank > need), thr_t - 1, blk)
                return seen + _dot(t16, ones)

            lax.fori_loop(0, n_kb, demote, jnp.zeros((row_sub, LANES), F32))

        thr_sc[rows, :] = jnp.maximum(thr, KEY_NEG_INF + 1)

    m_sc[...] = jnp.full(m_sc.shape, NEG_BIG, F32)
    l_sc[...] = jnp.zeros(l_sc.shape, F32)
    acc_sc[...] = jnp.zeros(acc_sc.shape, F32)
    group = A_HEADS // A_KV_HEADS

    ones_blk = jnp.ones((tk, LANES), MXU_DTYPE)
    row_att = min(tq, ATT_ROWS)

    def attend_block(kb, carry):
        k0 = pl.multiple_of(kb * tk, tk)
        bias_sc = key_sc.at[kb]
        bias = jnp.where(bias_sc[...] >= _lane_tiled(thr_sc[...], n_lane_tiles), 0.0, NEG_BIG)
        bias_sc[...] = pltpu.bitcast(bias, I32)
        for g in range(A_KV_HEADS):
            kblk = k_ref[0, pl.ds(k0, tk), g * LANES:(g + 1) * LANES]
            vaug = jnp.concatenate([v_ref[0, pl.ds(k0, tk), g * LANES:(g + 1) * LANES], ones_blk], axis=1)
            for r in range(group):
                h = g * group + r
                for rt in range(tq // row_att):
                    rows = pl.ds(rt * row_att, row_att)
                    q = aq_ref[0, rows, h * LANES:(h + 1) * LANES]
                    s = _dot_nt(q, kblk) + pltpu.bitcast(bias_sc[rows, :], F32)
                    m_prev = m_sc[h, rows, :]
                    m_new = jnp.maximum(m_prev, jnp.max(s, axis=1, keepdims=True))
                    alpha = jnp.exp2(m_prev - m_new)
                    p = jnp.exp2(s - _lane_tiled(m_new, n_lane_tiles))
                    pv = _dot(p.astype(MXU_DTYPE), vaug)
                    acc_sc[h, rows, :] = alpha * acc_sc[h, rows, :] + pv[:, :LANES]
                    l_sc[h, rows, :] = alpha * l_sc[h, rows, :] + pv[:, LANES:]
                    m_sc[h, rows, :] = m_new
        return carry

    lax.fori_loop(0, n_kb, attend_block, 0)
    for h in range(A_HEADS):
        ya_ref[0, :, h * LANES:(h + 1) * LANES] = (acc_sc[h] / l_sc[h]).astype(ya_ref.dtype)


def _dsa(aq, iq, ikw, ika, ikb, k16, v16, *, n_valid_keys, q_tile, key_tile):
    bsz, t, _ = aq.shape
    lp = k16.shape[1]
    tq, tk = q_tile, key_tile
    assert t % tq == 0 and tq % CHUNK == 0 and lp % tk == 0 and tk % LANES == 0 and n_valid_keys <= lp
    topk = min(TOPK_MAX, n_valid_keys // 4)
    assert tk >= topk
    row_sub = min(tq, SELECT_ROWS)
    kernel = functools.partial(
        _dsa_kernel, tq=tq, tk=tk, key_off=n_valid_keys - t, topk=topk, row_sub=row_sub)

    def q_spec(width):
        return pl.BlockSpec((1, tq, width), lambda b, i: (b, i, 0))

    def key_spec(width):
        return pl.BlockSpec((1, lp, width), lambda b, i: (b, 0, 0), pipeline_mode=pl.Buffered(1))

    return pl.pallas_call(
        kernel,
        grid=(bsz, t // tq),
        in_specs=[q_spec(A_WIDTH), q_spec(IQ_WIDTH), q_spec(LANES), key_spec(LANES), key_spec(LANES),
                  key_spec(KV_WIDTH), key_spec(KV_WIDTH)],
        out_specs=q_spec(A_WIDTH),
        out_shape=jax.ShapeDtypeStruct((bsz, t, A_WIDTH), MXU_DTYPE),
        scratch_shapes=[pltpu.VMEM((lp // tk, tq, tk), I32), pltpu.VMEM((tq, LANES), I32),
                        pltpu.VMEM((A_HEADS, tq, LANES), F32), pltpu.VMEM((A_HEADS, tq, LANES), F32),
                        pltpu.VMEM((A_HEADS, tq, LANES), F32)],
        compiler_params=pltpu.CompilerParams(dimension_semantics=("arbitrary", "arbitrary"),
                                             vmem_limit_bytes=VMEM_LIMIT_BYTES),
        name="dsa",
    )(aq, iq, ikw, ika, ikb, k16, v16)


SUB = SUBLANES
N_SUB = CHUNK // SUB
INTER_ROWS = SUB * (N_SUB * (N_SUB - 1) // 2)


def _split3(x):
    hi = x.astype(MXU_DTYPE)
    r1 = x - hi.astype(F32)
    mid = r1.astype(MXU_DTYPE)
    lo = (r1 - mid.astype(F32)).astype(MXU_DTYPE)
    return hi, mid, lo


def _hgrn_kernel(q_ref, k_ref, lf_ref, v_ref, g_ref, gain_ref, s0_ref, yb_ref, sout_ref, st_sc):
    c = pl.program_id(1)

    @pl.when(c == 0)
    def _():
        for h in range(B_HEADS):
            st_sc[h] = s0_ref[0, h].T

    q = q_ref[0].astype(F32)
    k = k_ref[0].astype(F32)
    v16 = v_ref[0]
    lf = lf_ref[0]

    ti = lax.broadcasted_iota(I32, (CHUNK, CHUNK), 0)
    si = lax.broadcasted_iota(I32, (CHUNK, CHUNK), 1)
    tri = jnp.where(si <= ti, 1.0, 0.0).astype(MXU_DTYPE)
    hi, mid, lo = _split3(lf)
    b = _dot(tri, hi) + _dot(tri, mid) + _dot(tri, lo)

    def row_bcast(x, r, n):
        return jnp.broadcast_to(x[r:r + 1, :], (n, x.shape[1]))

    b_start = jnp.concatenate(
        [jnp.zeros((SUB, B_WIDTH), F32)] + [row_bcast(b, j * SUB - 1, SUB) for j in range(1, N_SUB)], axis=0)
    q_hat = (q * jnp.exp(b - b_start)).astype(MXU_DTYPE)
    q_state = (q * jnp.exp(b)).astype(MXU_DTYPE)
    b_end = b[CHUNK - 1:CHUNK, :]
    k_end = k * jnp.exp(b_end - b)

    k_hat = jnp.concatenate(
        [k[:i * SUB] * jnp.exp(row_bcast(b, i * SUB - 1, i * SUB) - b[:i * SUB]) for i in range(1, N_SUB)],
        axis=0).astype(MXU_DTYPE)
    v_hat = jnp.concatenate([v16[:i * SUB] for i in range(1, N_SUB)], axis=0)

    q3 = q.reshape(N_SUB, SUB, B_WIDTH)
    k3 = k.reshape(N_SUB, SUB, B_WIDTH)
    b3 = b.reshape(N_SUB, SUB, B_WIDTH)
    intra = []
    for s in range(SUB):
        kp = jnp.broadcast_to(k3[:, s:s + 1, :], q3.shape)
        bp = jnp.broadcast_to(b3[:, s:s + 1, :], q3.shape)
        x = q3 * kp * jnp.exp(jnp.minimum(b3 - bp, 0.0))
        intra.append(x.reshape(CHUNK, B_WIDTH).astype(MXU_DTYPE))

    pr = lax.broadcasted_iota(I32, (SUB * LANES, CHUNK), 0)
    pc = lax.broadcasted_iota(I32, (SUB * LANES, CHUNK), 1)
    place = jnp.where(lax.shift_right_logical(pr, LANES.bit_length() - 1) == (pc & (SUB - 1)), 1.0, 0.0
                      ).astype(MXU_DTYPE)
    t_sub = lax.shift_right_logical(ti, SUB.bit_length() - 1)
    intra_ok = (t_sub == lax.shift_right_logical(si, SUB.bit_length() - 1)) & (si <= ti)
    tt = lax.broadcasted_iota(I32, (CHUNK, INTER_ROWS), 0)
    cc = lax.broadcasted_iota(I32, (CHUNK, INTER_ROWS), 1)
    seg = jnp.ones((CHUNK, INTER_ROWS), I32)
    for i in range(2, N_SUB):
        seg = seg + jnp.where(cc >= SUB * (i * (i - 1) // 2), 1, 0)
    inter_ok = lax.shift_right_logical(tt, SUB.bit_length() - 1) == seg

    gain = gain_ref[...]
    for h in range(B_HEADS):
        sl = slice(h * LANES, (h + 1) * LANES)
        st = st_sc[h]
        x_cat = jnp.concatenate([x[:, sl] for x in intra], axis=1)
        p_intra = jnp.where(intra_ok, _dot(x_cat, place), 0.0).astype(MXU_DTYPE)
        p_inter = jnp.where(inter_ok, _dot_nt(q_hat[:, sl], k_hat[:, sl]), 0.0).astype(MXU_DTYPE)
        o = (_dot(p_inter, v_hat[:, sl]) + _dot(p_intra, v16[:, sl])
             + _dot_nt(q_state[:, sl], st.astype(MXU_DTYPE)))
        st_new = st * jnp.exp(b_end[:, sl]) + _dot(v16[:, sl].astype(F32).T.astype(MXU_DTYPE),
                                                    k_end[:, sl].astype(MXU_DTYPE))
        st_sc[h] = st_new
        y = _rms(o, gain[:, sl]) * g_ref[0, :, sl].astype(F32)
        yb_ref[0, :, sl] = y.astype(yb_ref.dtype)

    @pl.when(c == pl.num_programs(1) - 1)
    def _():
        for h in range(B_HEADS):
            sout_ref[0, h] = st_sc[h].T


def _hgrn(hq, hk, hlf, hv, hg, norm_gain, state0):
    bsz, t, _ = hq.shape
    assert t % CHUNK == 0

    def seq_spec():
        return pl.BlockSpec((1, CHUNK, B_WIDTH), lambda b, c: (b, c, 0))

    state_spec = pl.BlockSpec((1, B_HEADS, B_KEY_DIM, B_VAL_DIM), lambda b, c: (b, 0, 0, 0))
    return pl.pallas_call(
        _hgrn_kernel,
        grid=(bsz, t // CHUNK),
        in_specs=[seq_spec()] * 5 + [pl.BlockSpec((1, B_WIDTH), lambda b, c: (0, 0)), state_spec],
        out_specs=[seq_spec(), state_spec],
        out_shape=[jax.ShapeDtypeStruct((bsz, t, B_WIDTH), MXU_DTYPE),
                   jax.ShapeDtypeStruct((bsz, B_HEADS, B_KEY_DIM, B_VAL_DIM), F32)],
        scratch_shapes=[pltpu.VMEM((B_HEADS, B_VAL_DIM, B_KEY_DIM), F32)],
        compiler_params=pltpu.CompilerParams(dimension_semantics=("arbitrary", "arbitrary"),
                                             vmem_limit_bytes=VMEM_LIMIT_BYTES),
        name="hgrn",
    )(hq, hk, hlf, hv, hg, norm_gain.reshape(1, B_WIDTH), state0)


def _memory_kv_kernel(mem_ref, gain_ref, w_ref, k32_ref, v32_ref, k16_ref, v16_ref):
    hb = _rms(mem_ref[0], gain_ref[...]).astype(MXU_DTYPE)
    mk = _dot(hb, w_ref[:, :C_WIDTH])
    mv = _dot(hb, w_ref[:, C_WIDTH:])
    k32_ref[0] = mk
    v32_ref[0] = mv
    k16_ref[0] = mk.astype(MXU_DTYPE)
    v16_ref[0] = mv.astype(MXU_DTYPE)


def _memory_kv(mem, gain, w16):
    bsz, n, d = mem.shape
    spec = pl.BlockSpec((1, n, C_WIDTH), lambda b: (b, 0, 0))
    return pl.pallas_call(
        _memory_kv_kernel,
        grid=(bsz,),
        in_specs=[pl.BlockSpec((1, n, d), lambda b: (b, 0, 0)), _resident((1, d)), _resident((d, 2 * C_WIDTH))],
        out_specs=[spec] * 4,
        out_shape=[jax.ShapeDtypeStruct((bsz, n, C_WIDTH), dt) for dt in (F32, F32, MXU_DTYPE, MXU_DTYPE)],
        compiler_params=pltpu.CompilerParams(dimension_semantics=("arbitrary",), vmem_limit_bytes=VMEM_LIMIT_BYTES),
        name="memory_kv",
    )(mem, gain.reshape(1, d), w16)


def _merge_ffn_kernel(x_ref, ya_ref, yb_ref, cq_ref, mk_ref, mv_ref, gt_ref, wa_ref, wb_ref, wc_ref, wo_ref,
                      g_mix_ref, g_pre_ref, g_post_ref, wu_ref, wd_ref, out_ref):
    d = x_ref.shape[-1]
    heads = []
    for h in range(C_HEADS):
        sl = slice(h * C_HEAD_DIM, (h + 1) * C_HEAD_DIM)
        s = _dot_nt(cq_ref[0, :, sl], mk_ref[0, :, sl])
        p = jnp.exp(s - jnp.max(s, axis=-1, keepdims=True))
        p = p / jnp.sum(p, axis=-1, keepdims=True)
        heads.append(_dot(p.astype(MXU_DTYPE), mv_ref[0, :, sl]).astype(MXU_DTYPE))
    y_c = jnp.concatenate(heads, axis=1)

    merged = (gt_ref[0, :, 0:d].astype(F32) * _dot(ya_ref[0], wa_ref[...])
              + gt_ref[0, :, d:2 * d].astype(F32) * _dot(yb_ref[0], wb_ref[...])
              + gt_ref[0, :, 2 * d:3 * d].astype(F32) * _dot(y_c, wc_ref[...]))
    x1 = x_ref[0] + _rms(_dot(merged.astype(MXU_DTYPE), wo_ref[...]), g_mix_ref[...])
    u = jnp.maximum(_dot(_rms(x1, g_pre_ref[...]).astype(MXU_DTYPE), wu_ref[...]), 0.0)
    down = _dot((u * u).astype(MXU_DTYPE), wd_ref[...])
    out_ref[0] = x1 + _rms(down, g_post_ref[...])


def _merge_ffn(x, ya, yb, cq, mk16, mv16, gates, weights, gains, row_tile):
    bsz, t, d = x.shape
    tm = row_tile
    assert t % tm == 0
    n_mem = mk16.shape[1]

    def rows(width):
        return pl.BlockSpec((1, tm, width), lambda b, i: (b, i, 0))

    mem_spec = pl.BlockSpec((1, n_mem, C_WIDTH), lambda b, i: (b, 0, 0))
    return pl.pallas_call(
        _merge_ffn_kernel,
        grid=(bsz, t // tm),
        in_specs=[rows(d), rows(A_WIDTH), rows(B_WIDTH), rows(C_WIDTH), mem_spec, mem_spec, rows(N_BRANCHES * d)]
        + [_resident(w.shape) for w in weights[:4]] + [_resident((1, d))] * 3
        + [_resident(w.shape) for w in weights[4:]],
        out_specs=rows(d),
        out_shape=jax.ShapeDtypeStruct((bsz, t, d), F32),
        compiler_params=pltpu.CompilerParams(dimension_semantics=("arbitrary", "arbitrary"),
                                             vmem_limit_bytes=VMEM_LIMIT_BYTES),
        name="merge_ffn",
    )(x, ya, yb, cq, mk16, mv16, gates, *weights[:4], *[g.reshape(1, d) for g in gains], *weights[4:])


def _layer(x, pos, lower_bound, lp, *, cache, state0, mem16, row_tile, q_tile, key_tile):
    bsz, t, d = x.shape
    (aq, k32, v32, k16, v16, iq, ikw, ika, ikb, hq, hk, hlf, hv, hg, cq, gates) = _in_proj(
        x, pos, lp["pre_mix_gain"], lower_bound, lp["w_in"], min(row_tile, bsz * t))

    keys = [ika, ikb, k16, v16]
    if cache is not None:
        keys = [jnp.concatenate([c, kn], axis=1) for c, kn in zip(cache, keys)]
    n_keys = keys[0].shape[1]
    pad = (-n_keys) % key_tile
    if pad:
        keys = [jnp.pad(kk, ((0, 0), (0, pad), (0, 0))) for kk in keys]
    ya = _dsa(aq, iq, ikw, *keys, n_valid_keys=n_keys, q_tile=q_tile, key_tile=key_tile)

    yb, s_new = _hgrn(hq, hk, hlf, hv, hg, lp["hgrn_norm_gain"], state0)

    out = _merge_ffn(x, ya, yb, cq, mem16[0], mem16[1], gates,
                     [lp["w_out_a"], lp["w_out_b"], lp["w_out_c"], lp["w_out"], lp["w_up"], lp["w_down"]],
                     [lp["post_mix_gain"], lp["pre_ffn_gain"], lp["post_ffn_gain"]], row_tile=min(row_tile, t))
    new_k = k32.reshape(bsz, t, A_KV_HEADS, HEAD_DIM)
    new_v = v32.reshape(bsz, t, A_KV_HEADS, HEAD_DIM)
    return out, (new_k, new_v, ikw[..., :IDX_DIM], s_new)


def _tiles(t):
    return 256, min(256, t)


def kernel(x_prompt, x_sample, cache_k, cache_v, cache_idx_k, state_hgrn, cache_mem_k, cache_mem_v, mem_prompt, w_in, w_mem_kv, mem_norm_gain, hgrn_lb_logits, hgrn_norm_gain, w_out_a, w_out_b, w_out_c, w_out, pre_mix_gain, post_mix_gain, pre_ffn_gain, post_ffn_gain, w_up, w_down):
    depth = w_in.shape[0]
    d = x_prompt.shape[-1]
    bp, tp, _ = x_prompt.shape
    bs, ts, _ = x_sample.shape
    past = cache_k.shape[2]
    lower_bounds = jnp.cumsum(jax.nn.softmax(hgrn_lb_logits.astype(F32), axis=0), axis=0)
    pos_p = jnp.arange(tp, dtype=I32)
    pos_s = past + jnp.arange(ts, dtype=I32)
    bf = lambda a: a.astype(MXU_DTYPE)

    xp, xs = x_prompt, x_sample
    outs_p, outs_s = [], []
    for l in range(depth):
        lp = {
            "w_in": _pack_w_in(w_in[l], d), "hgrn_norm_gain": hgrn_norm_gain[l],
            "w_out_a": bf(w_out_a[l]), "w_out_b": bf(w_out_b[l]), "w_out_c": bf(w_out_c[l]), "w_out": bf(w_out[l]),
            "pre_mix_gain": pre_mix_gain[l], "post_mix_gain": post_mix_gain[l], "pre_ffn_gain": pre_ffn_gain[l],
            "post_ffn_gain": post_ffn_gain[l], "w_up": bf(w_up[l]), "w_down": bf(w_down[l]),
        }
        mk32, mv32, mk16, mv16 = _memory_kv(mem_prompt, mem_norm_gain[l], bf(w_mem_kv[l]))
        n_mem = mem_prompt.shape[1]
        row_tile, q_tile = _tiles(tp)
        xp, st_p = _layer(xp, pos_p, lower_bounds[l], lp, cache=None,
                          state0=jnp.zeros((bp, B_HEADS, B_KEY_DIM, B_VAL_DIM), F32), mem16=(mk16, mv16),
                          row_tile=row_tile, q_tile=q_tile, key_tile=512)
        outs_p.append(st_p + (mk32.reshape(bp, n_mem, C_HEADS, C_HEAD_DIM), mv32.reshape(bp, n_mem, C_HEADS, C_HEAD_DIM)))

        cik = cache_idx_k[l]
        zeros_ik = jnp.zeros_like(cik)
        cache = (bf(jnp.concatenate([cik, zeros_ik], axis=-1)), bf(jnp.concatenate([zeros_ik, cik], axis=-1)),
                 bf(cache_k[l].reshape(bs, past, KV_WIDTH)), bf(cache_v[l].reshape(bs, past, KV_WIDTH)))
        mem16 = (bf(cache_mem_k[l].reshape(bs, -1, C_WIDTH)), bf(cache_mem_v[l].reshape(bs, -1, C_WIDTH)))
        row_tile, q_tile = _tiles(ts)
        xs, st_s = _layer(xs, pos_s, lower_bounds[l], lp, cache=cache, state0=state_hgrn[l].astype(F32),
                          mem16=mem16, row_tile=row_tile, q_tile=q_tile, key_tile=384)
        outs_s.append(st_s)

    new_k_p, new_v_p, new_ik_p, new_s_p, new_mk_p, new_mv_p = [jnp.stack(a) for a in zip(*outs_p)]
    new_k_s, new_v_s, new_ik_s, new_s_s = [jnp.stack(a) for a in zip(*outs_s)]
    return (xp, xs, new_k_p, new_v_p, new_ik_p, new_s_p, new_mk_p, new_mv_p, new_k_s, new_v_s, new_ik_s, new_s_s)
```

```python
import functools

import jax
import jax.numpy as jnp
import numpy as np
from jax import lax
from jax.experimental import pallas as pl
from jax.experimental.pallas import tpu as pltpu

CHUNK = 64
N_BRANCHES = 3
A_HEADS, A_KV_HEADS, HEAD_DIM = 6, 2, 128
ROT_DIM = HEAD_DIM // 4
IDX_HEADS, IDX_DIM = 8, 64
IDX_ROT_DIM = IDX_DIM // 4
TOPK_MAX = 256
B_HEADS, B_KEY_DIM, B_VAL_DIM = 6, 128, 128
C_HEADS, C_HEAD_DIM = 4, 128
ROPE_THETA = 500000.0
EPS = 1e-6

A_WIDTH = A_HEADS * HEAD_DIM
KV_WIDTH = A_KV_HEADS * HEAD_DIM
IQ_WIDTH = IDX_HEADS * IDX_DIM
B_WIDTH = B_HEADS * B_KEY_DIM
C_WIDTH = C_HEADS * C_HEAD_DIM

LANES = 128
SUBLANES = 8
VMEM_LIMIT_BYTES = 56 * 1024 * 1024

MXU_DTYPE = jnp.bfloat16
F32 = jnp.float32
I32 = jnp.int32

IW_LANE = 96
NEG_BIG = -1e30
LOG2E = 1.4426950408889634
ATT_ROWS = 128
ATT_COLS = 256
SELECT_ROWS = 256
INT_MIN = -(2 ** 31)
KEY_NEG_INF = INT_MIN + 0x00800000

OFF_AQ = 0
OFF_AK = OFF_AQ + A_WIDTH
OFF_AV = OFF_AK + KV_WIDTH
OFF_IQ = OFF_AV + KV_WIDTH
OFF_IKW = OFF_IQ + IQ_WIDTH
OFF_BQ = OFF_IKW + LANES
OFF_BF = OFF_BQ + B_WIDTH
OFF_BI = OFF_BF + B_WIDTH
OFF_BG = OFF_BI + B_WIDTH
OFF_CQ = OFF_BG + B_WIDTH
OFF_GATES = OFF_CQ + C_WIDTH


def _sigmoid(x):
    return 1.0 / (1.0 + jnp.exp(-x))


def _dot(a, b):
    return jnp.dot(a, b, preferred_element_type=F32)


def _dot_nt(a, b):
    return lax.dot_general(a, b, (((1,), (1,)), ((), ())), preferred_element_type=F32)


def _rms(x, gain):
    return x * lax.rsqrt(jnp.mean(x * x, axis=-1, keepdims=True) + EPS) * gain


def _resident(shape):
    zeros = (0,) * len(shape)
    return pl.BlockSpec(shape, lambda *_: zeros, pipeline_mode=pl.Buffered(1))


def _rope(y, cos, sin_up, sin_down, half):
    return y * cos + pltpu.roll(y, half, 1) * sin_up + pltpu.roll(y, LANES - half, 1) * sin_down


def _in_proj_kernel(x_ref, gain_ref, lb_ref, w_ref, ca_ref, sau_ref, sad_ref, ci_ref, siu_ref, sid_ref,
                    aq_ref, k32_ref, v32_ref, k16_ref, v16_ref, iq_ref, ikw_ref, ika_ref, ikb_ref,
                    hq_ref, hk_ref, hlf_ref, hv_ref, hg_ref, cq_ref, gt_ref):
    hb = _rms(x_ref[...], gain_ref[...]).astype(MXU_DTYPE)

    def proj(c0, width):
        return _dot(hb, w_ref[:, c0:c0 + width])

    ca, sau, sad = ca_ref[...], sau_ref[...], sad_ref[...]
    ci, siu, sid = ci_ref[...], siu_ref[...], sid_ref[...]

    y = proj(OFF_AQ, A_WIDTH)
    for h in range(A_HEADS):
        sl = slice(h * LANES, (h + 1) * LANES)
        aq_ref[:, sl] = (_rope(y[:, sl], ca, sau, sad, ROT_DIM // 2) * ((HEAD_DIM ** -0.5) * LOG2E)).astype(MXU_DTYPE)

    y = proj(OFF_AK, KV_WIDTH)
    for h in range(A_KV_HEADS):
        sl = slice(h * LANES, (h + 1) * LANES)
        r = _rope(y[:, sl], ca, sau, sad, ROT_DIM // 2)
        k32_ref[:, sl] = r
        k16_ref[:, sl] = r.astype(MXU_DTYPE)

    y = proj(OFF_AV, KV_WIDTH)
    v32_ref[...] = y
    v16_ref[...] = y.astype(MXU_DTYPE)

    y = proj(OFF_IQ, IQ_WIDTH)
    for j in range(IQ_WIDTH // LANES):
        sl = slice(j * LANES, (j + 1) * LANES)
        iq_ref[:, sl] = _rope(y[:, sl], ci, siu, sid, IDX_ROT_DIM // 2).astype(MXU_DTYPE)

    r = _rope(proj(OFF_IKW, LANES), ci, siu, sid, IDX_ROT_DIM // 2)
    lane = lax.broadcasted_iota(I32, r.shape, 1)
    ikw_ref[...] = jnp.where(lane < IDX_DIM, r, r * ((IDX_DIM ** -0.5) * (IDX_HEADS ** -0.5)))
    ik_low = jnp.where(lane < IDX_DIM, r, 0.0)
    ika_ref[...] = ik_low.astype(MXU_DTYPE)
    ikb_ref[...] = pltpu.roll(ik_low, IDX_DIM, 1).astype(MXU_DTYPE)

    y = proj(OFF_BQ, B_WIDTH)
    hq_ref[...] = (y * _sigmoid(y)).astype(MXU_DTYPE)
    lb = lb_ref[...]
    f = lb + (1.0 - lb) * _sigmoid(proj(OFF_BF, B_WIDTH))
    hk_ref[...] = (1.0 - f).astype(MXU_DTYPE)
    hlf_ref[...] = jnp.log(f)
    hv_ref[...] = proj(OFF_BI, B_WIDTH).astype(MXU_DTYPE)
    y = proj(OFF_BG, B_WIDTH)
    hg_ref[...] = (y * _sigmoid(y)).astype(MXU_DTYPE)

    cq_ref[...] = (proj(OFF_CQ, C_WIDTH) * (C_HEAD_DIM ** -0.5)).astype(MXU_DTYPE)

    n_gate = gt_ref.shape[1]
    for c0 in range(0, n_gate, B_WIDTH):
        gt_ref[:, c0:c0 + B_WIDTH] = _sigmoid(proj(OFF_GATES + c0, B_WIDTH)).astype(MXU_DTYPE)


def _rope_tables(pos, rot_dim, head_dim):
    half = rot_dim // 2
    inv_freq = ROPE_THETA ** (-jnp.arange(half, dtype=F32) / half)
    ang = pos.astype(F32)[:, None] * inv_freq[None, :]
    cos, sin = jnp.cos(ang), jnp.sin(ang)
    lane = np.arange(LANES) % head_dim
    idx = lane % half
    first, second = lane < half, (lane >= half) & (lane < rot_dim)
    cos_t = jnp.where(first | second, cos[:, idx], 1.0)
    sin_up = jnp.where(second, sin[:, idx], 0.0)
    sin_down = jnp.where(first, -sin[:, idx], 0.0)
    return cos_t, sin_up, sin_down


def _in_proj(x, pos, gain, lower_bound, w_packed, row_tile):
    bsz, t, d = x.shape
    n = bsz * t
    tm = row_tile
    assert n % tm == 0 and (t % tm == 0 or tm % t == 0)
    tables = _rope_tables(pos, ROT_DIM, HEAD_DIM) + _rope_tables(pos, IDX_ROT_DIM, IDX_DIM)
    table_rows = max(t, tm)
    tables = [jnp.tile(tb, (table_rows // t, 1)) for tb in tables]
    n_table_blocks = table_rows // tm
    n_gate = N_BRANCHES * d
    w_width = w_packed.shape[1]

    def rows(width):
        return pl.BlockSpec((tm, width), lambda i: (i, 0))

    table_spec = pl.BlockSpec((tm, LANES), lambda i: (i % n_table_blocks, 0))
    widths_dtypes = [
        (A_WIDTH, MXU_DTYPE), (KV_WIDTH, F32), (KV_WIDTH, F32), (KV_WIDTH, MXU_DTYPE), (KV_WIDTH, MXU_DTYPE),
        (IQ_WIDTH, MXU_DTYPE), (LANES, F32), (LANES, MXU_DTYPE), (LANES, MXU_DTYPE),
        (B_WIDTH, MXU_DTYPE), (B_WIDTH, MXU_DTYPE), (B_WIDTH, F32), (B_WIDTH, MXU_DTYPE), (B_WIDTH, MXU_DTYPE),
        (C_WIDTH, MXU_DTYPE), (n_gate, MXU_DTYPE)]
    outs = pl.pallas_call(
        _in_proj_kernel,
        grid=(n // tm,),
        in_specs=[rows(d), _resident((1, d)), _resident((1, B_WIDTH)), _resident((d, w_width))] + [table_spec] * 6,
        out_specs=[rows(w) for w, _ in widths_dtypes],
        out_shape=[jax.ShapeDtypeStruct((n, w), dt) for w, dt in widths_dtypes],
        compiler_params=pltpu.CompilerParams(dimension_semantics=("arbitrary",), vmem_limit_bytes=VMEM_LIMIT_BYTES),
        name="in_proj",
    )(x.reshape(n, d), gain.reshape(1, d), lower_bound.reshape(1, B_WIDTH), w_packed, *tables)
    return [o.reshape(bsz, t, o.shape[-1]) for o in outs]


def _pack_w_in(w, d):
    widths = (A_WIDTH, KV_WIDTH, KV_WIDTH, IQ_WIDTH, IDX_DIM, IDX_HEADS, B_WIDTH, B_WIDTH, B_WIDTH, B_WIDTH,
              C_WIDTH, N_BRANCHES * d)
    points = [int(s) for s in np.cumsum(widths)[:-1]]
    a_q, a_k, a_v, i_q, i_k, i_w, b_q, b_f, b_i, b_g, c_q, gates = jnp.split(w, points, axis=-1)
    ikw = jnp.concatenate([i_k, jnp.zeros((d, IW_LANE - IDX_DIM), w.dtype), i_w,
                           jnp.zeros((d, LANES - IW_LANE - IDX_HEADS), w.dtype)], axis=-1)
    return jnp.concatenate([a_q, a_k, a_v, i_q, ikw, b_q, b_f, b_i, b_g, c_q, gates], axis=-1).astype(MXU_DTYPE)


def _sortable_key(score):
    bits = pltpu.bitcast(score, I32)
    return jnp.where(bits < 0, (bits ^ 0x7FFFFFFF) + 1, bits)


def _lane_tiled(x, n_tiles):
    return x if n_tiles == 1 else jnp.concatenate([x] * n_tiles, axis=1)


def _dsa_rowmajor_kernel(aq_ref, iq_ref, ikw_ref, ika_ref, ikb_ref, k_ref, v_ref, ya_ref,
                key_sc, thr_sc, m_sc, l_sc, acc_sc, *, tq, tk, key_off, topk, row_sub):
    qi = pl.program_id(1)
    n_lane_tiles = tk // LANES
    n_vis = key_off + (qi + 1) * tq
    n_kb = lax.div(n_vis + (tk - 1), tk)
    row = qi * tq + lax.broadcasted_iota(I32, (tq, 1), 0)
    limit = key_off + (lax.shift_right_logical(row, CHUNK.bit_length() - 1) + 1) * CHUNK

    iw = ikw_ref[0, :, IW_LANE:IW_LANE + IDX_HEADS]

    def score_block(kb, carry):
        k0 = pl.multiple_of(kb * tk, tk)
        ika = ika_ref[0, pl.ds(k0, tk), :]
        ikb = ikb_ref[0, pl.ds(k0, tk), :]
        acc = jnp.zeros((tq, tk), F32)
        for j in range(IQ_WIDTH // LANES):
            iqt = iq_ref[0, :, j * LANES:(j + 1) * LANES]
            acc = acc + jnp.maximum(_dot_nt(iqt, ika), 0.0) * iw[:, 2 * j:2 * j + 1]
            acc = acc + jnp.maximum(_dot_nt(iqt, ikb), 0.0) * iw[:, 2 * j + 1:2 * j + 2]
        col = k0 + lax.broadcasted_iota(I32, (tq, tk), 1)
        key_sc[kb] = jnp.where(col < limit, _sortable_key(acc), KEY_NEG_INF)
        return carry

    lax.fori_loop(0, n_kb, score_block, 0)

    def count_rows(rows, pred):
        def body(kb, acc):
            blk = key_sc[kb, rows, :]
            for c in range(n_lane_tiles):
                acc = acc + jnp.where(pred(blk[:, c * LANES:(c + 1) * LANES]), 1, 0)
            return acc
        acc = lax.fori_loop(0, n_kb, body, jnp.zeros((row_sub, LANES), I32))
        return jnp.sum(acc.astype(F32), axis=1, keepdims=True)

    for rs in range(tq // row_sub):
        rows = pl.ds(rs * row_sub, row_sub)

        def key_bit(it, u):
            trial = u | jnp.left_shift(jnp.int32(1), 31 - it)
            cand = trial ^ INT_MIN
            n_ge = count_rows(rows, lambda k: k >= cand)
            return jnp.where(n_ge >= topk, trial, u)

        u = lax.fori_loop(0, 32, key_bit, jnp.zeros((row_sub, LANES), I32))
        thr = u ^ INT_MIN
        n_gt = count_rows(rows, lambda k: k > thr)
        n_ge = count_rows(rows, lambda k: k >= thr)
        has_tie = jnp.where((n_ge > topk) & (thr[:, :1] > KEY_NEG_INF), 1.0, 0.0)

        @pl.when(jnp.max(has_tie) > 0.0)
        def _():
            need = _lane_tiled(jnp.broadcast_to(topk - n_gt, (row_sub, LANES)), n_lane_tiles)
            thr_t = _lane_tiled(thr, n_lane_tiles)
            ri = lax.broadcasted_iota(I32, (tk, tk), 0)
            ci = lax.broadcasted_iota(I32, (tk, tk), 1)
            upper = jnp.where(ri <= ci, 1.0, 0.0).astype(MXU_DTYPE)
            ones = jnp.ones((tk, LANES), MXU_DTYPE)

            def demote(kb, seen):
                blk = key_sc[kb, rows, :]
                tie = blk == thr_t
                t16 = jnp.where(tie, 1.0, 0.0).astype(MXU_DTYPE)
                rank = _dot(t16, upper) + _lane_tiled(seen, n_lane_tiles)
                key_sc[kb, rows, :] = jnp.where(tie & (rank > need), thr_t - 1, blk)
                return seen + _dot(t16, ones)

            lax.fori_loop(0, n_kb, demote, jnp.zeros((row_sub, LANES), F32))

        thr_sc[rows, :] = jnp.maximum(thr, KEY_NEG_INF + 1)

    m_sc[...] = jnp.full(m_sc.shape, NEG_BIG, F32)
    l_sc[...] = jnp.zeros(l_sc.shape, F32)
    acc_sc[...] = jnp.zeros(acc_sc.shape, F32)
    group = A_HEADS // A_KV_HEADS

    ones_blk = jnp.ones((tk, LANES), MXU_DTYPE)
    row_att = min(tq, ATT_ROWS)

    def attend_block(kb, carry):
        k0 = pl.multiple_of(kb * tk, tk)
        bias_sc = key_sc.at[kb]
        bias = jnp.where(bias_sc[...] >= _lane_tiled(thr_sc[...], n_lane_tiles), 0.0, NEG_BIG)
        bias_sc[...] = pltpu.bitcast(bias, I32)
        for g in range(A_KV_HEADS):
            kblk = k_ref[0, pl.ds(k0, tk), g * LANES:(g + 1) * LANES]
            vaug = jnp.concatenate([v_ref[0, pl.ds(k0, tk), g * LANES:(g + 1) * LANES], ones_blk], axis=1)
            for r in range(group):
                h = g * group + r
                for rt in range(tq // row_att):
                    rows = pl.ds(rt * row_att, row_att)
                    q = aq_ref[0, rows, h * LANES:(h + 1) * LANES]
                    s = _dot_nt(q, kblk) + pltpu.bitcast(bias_sc[rows, :], F32)
                    m_prev = m_sc[h, rows, :]
                    m_new = jnp.maximum(m_prev, jnp.max(s, axis=1, keepdims=True))
                    alpha = jnp.exp2(m_prev - m_new)
                    p = jnp.exp2(s - _lane_tiled(m_new, n_lane_tiles))
                    pv = _dot(p.astype(MXU_DTYPE), vaug)
                    acc_sc[h, rows, :] = alpha * acc_sc[h, rows, :] + pv[:, :LANES]
                    l_sc[h, rows, :] = alpha * l_sc[h, rows, :] + pv[:, LANES:]
                    m_sc[h, rows, :] = m_new
        return carry

    lax.fori_loop(0, n_kb, attend_block, 0)
    for h in range(A_HEADS):
        ya_ref[0, :, h * LANES:(h + 1) * LANES] = (acc_sc[h] / l_sc[h]).astype(ya_ref.dtype)


def _dsa_rowmajor(aq, iq, ikw, ika, ikb, k16, v16, *, n_valid_keys, q_tile, key_tile):
    bsz, t, _ = aq.shape
    lp = k16.shape[1]
    tq, tk = q_tile, key_tile
    assert t % tq == 0 and tq % CHUNK == 0 and lp % tk == 0 and tk % LANES == 0 and n_valid_keys <= lp
    topk = min(TOPK_MAX, n_valid_keys // 4)
    assert tk >= topk
    row_sub = min(tq, SELECT_ROWS)
    kernel = functools.partial(
        _dsa_rowmajor_kernel, tq=tq, tk=tk, key_off=n_valid_keys - t, topk=topk, row_sub=row_sub)

    def q_spec(width):
        return pl.BlockSpec((1, tq, width), lambda b, i: (b, i, 0))

    def key_spec(width):
        return pl.BlockSpec((1, lp, width), lambda b, i: (b, 0, 0), pipeline_mode=pl.Buffered(1))

    return pl.pallas_call(
        kernel,
        grid=(bsz, t // tq),
        in_specs=[q_spec(A_WIDTH), q_spec(IQ_WIDTH), q_spec(LANES), key_spec(LANES), key_spec(LANES),
                  key_spec(KV_WIDTH), key_spec(KV_WIDTH)],
        out_specs=q_spec(A_WIDTH),
        out_shape=jax.ShapeDtypeStruct((bsz, t, A_WIDTH), MXU_DTYPE),
        scratch_shapes=[pltpu.VMEM((lp // tk, tq, tk), I32), pltpu.VMEM((tq, LANES), I32),
                        pltpu.VMEM((A_HEADS, tq, LANES), F32), pltpu.VMEM((A_HEADS, tq, LANES), F32),
                        pltpu.VMEM((A_HEADS, tq, LANES), F32)],
        compiler_params=pltpu.CompilerParams(dimension_semantics=("arbitrary", "arbitrary"),
                                             vmem_limit_bytes=VMEM_LIMIT_BYTES),
        name="dsa",
    )(aq, iq, ikw, ika, ikb, k16, v16)


COUNT_GROUPS = 8
ONES_ROWS = 16


def _dsa_kernel(aq_ref, iq_ref, iw_ref, ika_ref, ikb_ref, k_ref, v_ref, ya_ref,
                key_sc, bias_sc, m_sc, l_sc, acc_sc, *, tq, tk, key_off, topk, q_sub):
    qi = pl.program_id(1)
    n_vis = key_off + (qi + 1) * tq
    n_kb = lax.div(n_vis + (tk - 1), tk)
    q_pos = qi * tq + lax.broadcasted_iota(I32, (1, tq), 1)
    limit = key_off + (lax.shift_right_logical(q_pos, CHUNK.bit_length() - 1) + 1) * CHUNK

    def score_block(kb, carry):
        k0 = pl.multiple_of(kb * tk, tk)
        ika = ika_ref[0, pl.ds(k0, tk), :]
        ikb = ikb_ref[0, pl.ds(k0, tk), :]
        acc = jnp.zeros((tk, tq), F32)
        for j in range(IQ_WIDTH // LANES):
            iqt = iq_ref[0, :, j * LANES:(j + 1) * LANES]
            acc = acc + jnp.maximum(_dot_nt(ika, iqt), 0.0) * iw_ref[0, 2 * j:2 * j + 1, :]
            acc = acc + jnp.maximum(_dot_nt(ikb, iqt), 0.0) * iw_ref[0, 2 * j + 1:2 * j + 2, :]
        key_pos = k0 + lax.broadcasted_iota(I32, (tk, tq), 0)
        key_sc[kb] = jnp.where(key_pos < limit, _sortable_key(acc), KEY_NEG_INF)
        return carry

    lax.fori_loop(0, n_kb, score_block, 0)

    def count(pred):
        step = SUBLANES * COUNT_GROUPS

        def body(kb, acc):
            for r0 in range(0, tk, step):
                acc = acc + jnp.where(pred(key_sc[kb, r0:r0 + step, :]), 1, 0)
            return acc
        acc = lax.fori_loop(0, n_kb, body, jnp.zeros((step, tq), I32))
        return jnp.sum(acc, axis=0, keepdims=True)

    def key_bit(it, u):
        trial = u | jnp.left_shift(jnp.int32(1), 31 - it)
        cand = trial ^ INT_MIN
        return jnp.where(count(lambda k: k >= cand) >= topk, trial, u)

    u = lax.fori_loop(0, 32, key_bit, jnp.zeros((1, tq), I32))
    thr = u ^ INT_MIN
    n_gt = count(lambda k: k > thr)
    n_ge = count(lambda k: k >= thr)
    has_tie = jnp.where((n_ge > topk) & (thr > KEY_NEG_INF), 1.0, 0.0)

    @pl.when(jnp.max(has_tie) > 0.0)
    def _():
        need = (topk - n_gt).astype(F32)
        ri = lax.broadcasted_iota(I32, (tk, tk), 0)
        ci = lax.broadcasted_iota(I32, (tk, tk), 1)
        lower = jnp.where(ci <= ri, 1.0, 0.0).astype(MXU_DTYPE)
        ones = jnp.ones((SUBLANES, tk), MXU_DTYPE)

        def demote(kb, seen):
            blk = key_sc[kb]
            tie = blk == thr
            t16 = jnp.where(tie, 1.0, 0.0).astype(MXU_DTYPE)
            rank = _dot(lower, t16) + seen
            key_sc[kb] = jnp.where(tie & (rank > need), thr - 1, blk)
            return seen + _dot(ones, t16)[0:1, :]

        lax.fori_loop(0, n_kb, demote, jnp.zeros((1, tq), F32))

    thr_sel = jnp.maximum(thr, KEY_NEG_INF + 1)

    m_sc[...] = jnp.full(m_sc.shape, NEG_BIG, F32)
    l_sc[...] = jnp.zeros(l_sc.shape, F32)
    acc_sc[...] = jnp.zeros(acc_sc.shape, F32)
    group = A_HEADS // A_KV_HEADS
    n_lane_tiles = tk // LANES
    ones_blk = jnp.ones((tk, LANES), MXU_DTYPE)

    def attend_block(kb, carry):
        k0 = pl.multiple_of(kb * tk, tk)
        bias_sc[...] = jnp.where(key_sc[kb] >= thr_sel, 0.0, NEG_BIG).T
        for g in range(A_KV_HEADS):
            kblk = k_ref[0, pl.ds(k0, tk), g * LANES:(g + 1) * LANES]
            vaug = jnp.concatenate([v_ref[0, pl.ds(k0, tk), g * LANES:(g + 1) * LANES], ones_blk], axis=1)
            for r in range(group):
                h = g * group + r
                for rt in range(tq // q_sub):
                    rows = pl.ds(rt * q_sub, q_sub)
                    q = aq_ref[0, rows, h * LANES:(h + 1) * LANES]
                    s = _dot_nt(q, kblk) + bias_sc[rows, :]
                    m_prev = m_sc[h, rows, :]
                    m_new = jnp.maximum(m_prev, jnp.max(s, axis=1, keepdims=True))
                    alpha = jnp.exp2(m_prev - m_new)
                    p = jnp.exp2(s - _lane_tiled(m_new, n_lane_tiles))
                    pv = _dot(p.astype(MXU_DTYPE), vaug)
                    acc_sc[h, rows, :] = alpha * acc_sc[h, rows, :] + pv[:, :LANES]
                    l_sc[h, rows, :] = alpha * l_sc[h, rows, :] + pv[:, LANES:]
                    m_sc[h, rows, :] = m_new
        return carry

    lax.fori_loop(0, n_kb, attend_block, 0)
    for h in range(A_HEADS):
        ya_ref[0, :, h * LANES:(h + 1) * LANES] = (acc_sc[h] / l_sc[h]).astype(ya_ref.dtype)


def _dsa(aq, iq, iw_t, ika, ikb, k16, v16, *, key_off, topk, q_tile, key_tile):
    bsz, t, _ = aq.shape
    lp = k16.shape[1]
    tq, tk = q_tile, key_tile
    assert lp % tk == 0
    n_kb_all = lp // tk
    assert t % tq == 0 and tq % LANES == 0 and tk % (SUBLANES * COUNT_GROUPS) == 0 and key_off + t <= lp
    assert tk >= topk
    kernel = functools.partial(_dsa_kernel, tq=tq, tk=tk, key_off=key_off, topk=topk, q_sub=min(tq, ATT_ROWS))

    def q_spec(width):
        return pl.BlockSpec((1, tq, width), lambda b, i: (b, i, 0))

    def key_spec(width):
        return pl.BlockSpec((1, lp, width), lambda b, i: (b, 0, 0), pipeline_mode=pl.Buffered(1))

    return pl.pallas_call(
        kernel,
        grid=(bsz, t // tq),
        in_specs=[q_spec(A_WIDTH), q_spec(IQ_WIDTH), pl.BlockSpec((1, IDX_HEADS, tq), lambda b, i: (b, 0, i)),
                  key_spec(LANES), key_spec(LANES), key_spec(KV_WIDTH), key_spec(KV_WIDTH)],
        out_specs=q_spec(A_WIDTH),
        out_shape=jax.ShapeDtypeStruct((bsz, t, A_WIDTH), MXU_DTYPE),
        scratch_shapes=[pltpu.VMEM((n_kb_all, tk, tq), I32), pltpu.VMEM((tq, tk), F32),
                        pltpu.VMEM((A_HEADS, tq, LANES), F32), pltpu.VMEM((A_HEADS, tq, LANES), F32),
                        pltpu.VMEM((A_HEADS, tq, LANES), F32)],
        compiler_params=pltpu.CompilerParams(dimension_semantics=("arbitrary", "arbitrary"),
                                             vmem_limit_bytes=VMEM_LIMIT_BYTES),
        name="dsa",
    )(aq, iq, iw_t, ika, ikb, k16, v16)


SUB = SUBLANES
N_SUB = CHUNK // SUB
INTER_ROWS = SUB * (N_SUB * (N_SUB - 1) // 2)


def _split3(x):
    hi = x.astype(MXU_DTYPE)
    r1 = x - hi.astype(F32)
    mid = r1.astype(MXU_DTYPE)
    lo = (r1 - mid.astype(F32)).astype(MXU_DTYPE)
    return hi, mid, lo


def _hgrn_kernel(q_ref, k_ref, lf_ref, v_ref, g_ref, gain_ref, s0_ref, yb_ref, sout_ref, st_sc):
    c = pl.program_id(1)

    @pl.when(c == 0)
    def _():
        for h in range(B_HEADS):
            st_sc[h] = s0_ref[0, h].T

    q = q_ref[0].astype(F32)
    k = k_ref[0].astype(F32)
    v16 = v_ref[0]
    lf = lf_ref[0]

    ti = lax.broadcasted_iota(I32, (CHUNK, CHUNK), 0)
    si = lax.broadcasted_iota(I32, (CHUNK, CHUNK), 1)
    tri = jnp.where(si <= ti, 1.0, 0.0).astype(MXU_DTYPE)
    hi, mid, lo = _split3(lf)
    b = _dot(tri, hi) + _dot(tri, mid) + _dot(tri, lo)

    def row_bcast(x, r, n):
        return jnp.broadcast_to(x[r:r + 1, :], (n, x.shape[1]))

    b_start = jnp.concatenate(
        [jnp.zeros((SUB, B_WIDTH), F32)] + [row_bcast(b, j * SUB - 1, SUB) for j in range(1, N_SUB)], axis=0)
    q_hat = (q * jnp.exp(b - b_start)).astype(MXU_DTYPE)
    q_state = (q * jnp.exp(b)).astype(MXU_DTYPE)
    b_end = b[CHUNK - 1:CHUNK, :]
    k_end = k * jnp.exp(b_end - b)

    k_hat = jnp.concatenate(
        [k[:i * SUB] * jnp.exp(row_bcast(b, i * SUB - 1, i * SUB) - b[:i * SUB]) for i in range(1, N_SUB)],
        axis=0).astype(MXU_DTYPE)
    v_hat = jnp.concatenate([v16[:i * SUB] for i in range(1, N_SUB)], axis=0)

    q3 = q.reshape(N_SUB, SUB, B_WIDTH)
    k3 = k.reshape(N_SUB, SUB, B_WIDTH)
    b3 = b.reshape(N_SUB, SUB, B_WIDTH)
    intra = []
    for s in range(SUB):
        kp = jnp.broadcast_to(k3[:, s:s + 1, :], q3.shape)
        bp = jnp.broadcast_to(b3[:, s:s + 1, :], q3.shape)
        x = q3 * kp * jnp.exp(jnp.minimum(b3 - bp, 0.0))
        intra.append(x.reshape(CHUNK, B_WIDTH).astype(MXU_DTYPE))

    pr = lax.broadcasted_iota(I32, (SUB * LANES, CHUNK), 0)
    pc = lax.broadcasted_iota(I32, (SUB * LANES, CHUNK), 1)
    place = jnp.where(lax.shift_right_logical(pr, LANES.bit_length() - 1) == (pc & (SUB - 1)), 1.0, 0.0
                      ).astype(MXU_DTYPE)
    t_sub = lax.shift_right_logical(ti, SUB.bit_length() - 1)
    intra_ok = (t_sub == lax.shift_right_logical(si, SUB.bit_length() - 1)) & (si <= ti)
    tt = lax.broadcasted_iota(I32, (CHUNK, INTER_ROWS), 0)
    cc = lax.broadcasted_iota(I32, (CHUNK, INTER_ROWS), 1)
    seg = jnp.ones((CHUNK, INTER_ROWS), I32)
    for i in range(2, N_SUB):
        seg = seg + jnp.where(cc >= SUB * (i * (i - 1) // 2), 1, 0)
    inter_ok = lax.shift_right_logical(tt, SUB.bit_length() - 1) == seg

    gain = gain_ref[...]
    for h in range(B_HEADS):
        sl = slice(h * LANES, (h + 1) * LANES)
        st = st_sc[h]
        x_cat = jnp.concatenate([x[:, sl] for x in intra], axis=1)
        p_intra = jnp.where(intra_ok, _dot(x_cat, place), 0.0).astype(MXU_DTYPE)
        p_inter = jnp.where(inter_ok, _dot_nt(q_hat[:, sl], k_hat[:, sl]), 0.0).astype(MXU_DTYPE)
        o = (_dot(p_inter, v_hat[:, sl]) + _dot(p_intra, v16[:, sl])
             + _dot_nt(q_state[:, sl], st.astype(MXU_DTYPE)))
        st_new = st * jnp.exp(b_end[:, sl]) + _dot(v16[:, sl].astype(F32).T.astype(MXU_DTYPE),
                                                    k_end[:, sl].astype(MXU_DTYPE))
        st_sc[h] = st_new
        y = _rms(o, gain[:, sl]) * g_ref[0, :, sl].astype(F32)
        yb_ref[0, :, sl] = y.astype(yb_ref.dtype)

    @pl.when(c == pl.num_programs(1) - 1)
    def _():
        for h in range(B_HEADS):
            sout_ref[0, h] = st_sc[h].T


def _hgrn(hq, hk, hlf, hv, hg, norm_gain, state0):
    bsz, t, _ = hq.shape
    assert t % CHUNK == 0

    def seq_spec():
        return pl.BlockSpec((1, CHUNK, B_WIDTH), lambda b, c: (b, c, 0))

    state_spec = pl.BlockSpec((1, B_HEADS, B_KEY_DIM, B_VAL_DIM), lambda b, c: (b, 0, 0, 0))
    return pl.pallas_call(
        _hgrn_kernel,
        grid=(bsz, t // CHUNK),
        in_specs=[seq_spec()] * 5 + [pl.BlockSpec((1, B_WIDTH), lambda b, c: (0, 0)), state_spec],
        out_specs=[seq_spec(), state_spec],
        out_shape=[jax.ShapeDtypeStruct((bsz, t, B_WIDTH), MXU_DTYPE),
                   jax.ShapeDtypeStruct((bsz, B_HEADS, B_KEY_DIM, B_VAL_DIM), F32)],
        scratch_shapes=[pltpu.VMEM((B_HEADS, B_VAL_DIM, B_KEY_DIM), F32)],
        compiler_params=pltpu.CompilerParams(dimension_semantics=("arbitrary", "arbitrary"),
                                             vmem_limit_bytes=VMEM_LIMIT_BYTES),
        name="hgrn",
    )(hq, hk, hlf, hv, hg, norm_gain.reshape(1, B_WIDTH), state0)


def _memory_kv_kernel(mem_ref, gain_ref, w_ref, k32_ref, v32_ref, k16_ref, v16_ref):
    hb = _rms(mem_ref[0], gain_ref[...]).astype(MXU_DTYPE)
    mk = _dot(hb, w_ref[:, :C_WIDTH])
    mv = _dot(hb, w_ref[:, C_WIDTH:])
    k32_ref[0] = mk
    v32_ref[0] = mv
    k16_ref[0] = mk.astype(MXU_DTYPE)
    v16_ref[0] = mv.astype(MXU_DTYPE)


def _memory_kv(mem, gain, w16):
    bsz, n, d = mem.shape
    spec = pl.BlockSpec((1, n, C_WIDTH), lambda b: (b, 0, 0))
    return pl.pallas_call(
        _memory_kv_kernel,
        grid=(bsz,),
        in_specs=[pl.BlockSpec((1, n, d), lambda b: (b, 0, 0)), _resident((1, d)), _resident((d, 2 * C_WIDTH))],
        out_specs=[spec] * 4,
        out_shape=[jax.ShapeDtypeStruct((bsz, n, C_WIDTH), dt) for dt in (F32, F32, MXU_DTYPE, MXU_DTYPE)],
        compiler_params=pltpu.CompilerParams(dimension_semantics=("arbitrary",), vmem_limit_bytes=VMEM_LIMIT_BYTES),
        name="memory_kv",
    )(mem, gain.reshape(1, d), w16)


def _merge_ffn_kernel(x_ref, ya_ref, yb_ref, cq_ref, mk_ref, mv_ref, gt_ref, wa_ref, wb_ref, wc_ref, wo_ref,
                      g_mix_ref, g_pre_ref, g_post_ref, wu_ref, wd_ref, out_ref):
    d = x_ref.shape[-1]
    heads = []
    for h in range(C_HEADS):
        sl = slice(h * C_HEAD_DIM, (h + 1) * C_HEAD_DIM)
        s = _dot_nt(cq_ref[0, :, sl], mk_ref[0, :, sl])
        p = jnp.exp(s - jnp.max(s, axis=-1, keepdims=True))
        p = p / jnp.sum(p, axis=-1, keepdims=True)
        heads.append(_dot(p.astype(MXU_DTYPE), mv_ref[0, :, sl]).astype(MXU_DTYPE))
    y_c = jnp.concatenate(heads, axis=1)

    merged = (gt_ref[0, :, 0:d].astype(F32) * _dot(ya_ref[0], wa_ref[...])
              + gt_ref[0, :, d:2 * d].astype(F32) * _dot(yb_ref[0], wb_ref[...])
              + gt_ref[0, :, 2 * d:3 * d].astype(F32) * _dot(y_c, wc_ref[...]))
    x1 = x_ref[0] + _rms(_dot(merged.astype(MXU_DTYPE), wo_ref[...]), g_mix_ref[...])
    u = jnp.maximum(_dot(_rms(x1, g_pre_ref[...]).astype(MXU_DTYPE), wu_ref[...]), 0.0)
    down = _dot((u * u).astype(MXU_DTYPE), wd_ref[...])
    out_ref[0] = x1 + _rms(down, g_post_ref[...])


def _merge_ffn(x, ya, yb, cq, mk16, mv16, gates, weights, gains, row_tile):
    bsz, t, d = x.shape
    tm = row_tile
    assert t % tm == 0
    n_mem = mk16.shape[1]

    def rows(width):
        return pl.BlockSpec((1, tm, width), lambda b, i: (b, i, 0))

    mem_spec = pl.BlockSpec((1, n_mem, C_WIDTH), lambda b, i: (b, 0, 0))
    return pl.pallas_call(
        _merge_ffn_kernel,
        grid=(bsz, t // tm),
        in_specs=[rows(d), rows(A_WIDTH), rows(B_WIDTH), rows(C_WIDTH), mem_spec, mem_spec, rows(N_BRANCHES * d)]
        + [_resident(w.shape) for w in weights[:4]] + [_resident((1, d))] * 3
        + [_resident(w.shape) for w in weights[4:]],
        out_specs=rows(d),
        out_shape=jax.ShapeDtypeStruct((bsz, t, d), F32),
        compiler_params=pltpu.CompilerParams(dimension_semantics=("arbitrary", "arbitrary"),
                                             vmem_limit_bytes=VMEM_LIMIT_BYTES),
        name="merge_ffn",
    )(x, ya, yb, cq, mk16, mv16, gates, *weights[:4], *[g.reshape(1, d) for g in gains], *weights[4:])


def _layer(x, pos, lower_bound, lp, *, cache, state0, mem16, row_tile, q_tile, key_tile):
    bsz, t, d = x.shape
    (aq, k32, v32, k16, v16, iq, ikw, ika, ikb, hq, hk, hlf, hv, hg, cq, gates) = _in_proj(
        x, pos, lp["pre_mix_gain"], lower_bound, lp["w_in"], min(row_tile, bsz * t))

    keys = [ika, ikb, k16, v16]
    if cache is not None:
        keys = [jnp.concatenate([c, kn], axis=1) for c, kn in zip(cache, keys)]
    n_keys = keys[0].shape[1]
    t_pad = -(-t // q_tile) * q_tile
    key_off = n_keys - t
    n_pad = -(-(key_off + t_pad) // key_tile) * key_tile
    ika, ikb, k16, v16 = [jnp.pad(kk, ((0, 0), (0, n_pad - n_keys), (0, 0))) for kk in keys]
    iw_t = jnp.swapaxes(ikw[..., IW_LANE:IW_LANE + IDX_HEADS], 1, 2)
    q_pad = ((0, 0), (0, t_pad - t), (0, 0))
    ya = _dsa(jnp.pad(aq, q_pad), jnp.pad(iq, q_pad), jnp.pad(iw_t, ((0, 0), (0, 0), (0, t_pad - t))),
              ika, ikb, k16, v16, key_off=key_off, topk=min(TOPK_MAX, n_keys // 4), q_tile=q_tile,
              key_tile=key_tile)[:, :t]

    yb, s_new = _hgrn(hq, hk, hlf, hv, hg, lp["hgrn_norm_gain"], state0)

    out = _merge_ffn(x, ya, yb, cq, mem16[0], mem16[1], gates,
                     [lp["w_out_a"], lp["w_out_b"], lp["w_out_c"], lp["w_out"], lp["w_up"], lp["w_down"]],
                     [lp["post_mix_gain"], lp["pre_ffn_gain"], lp["post_ffn_gain"]], row_tile=min(row_tile, t))
    new_k = k32.reshape(bsz, t, A_KV_HEADS, HEAD_DIM)
    new_v = v32.reshape(bsz, t, A_KV_HEADS, HEAD_DIM)
    return out, (new_k, new_v, ikw[..., :IDX_DIM], s_new)


def _tiles(t):
    return 256, max(LANES, min(256, t))


def kernel(x_prompt, x_sample, cache_k, cache_v, cache_idx_k, state_hgrn, cache_mem_k, cache_mem_v, mem_prompt, w_in, w_mem_kv, mem_norm_gain, hgrn_lb_logits, hgrn_norm_gain, w_out_a, w_out_b, w_out_c, w_out, pre_mix_gain, post_mix_gain, pre_ffn_gain, post_ffn_gain, w_up, w_down):
    depth = w_in.shape[0]
    d = x_prompt.shape[-1]
    bp, tp, _ = x_prompt.shape
    bs, ts, _ = x_sample.shape
    past = cache_k.shape[2]
    lower_bounds = jnp.cumsum(jax.nn.softmax(hgrn_lb_logits.astype(F32), axis=0), axis=0)
    pos_p = jnp.arange(tp, dtype=I32)
    pos_s = past + jnp.arange(ts, dtype=I32)
    bf = lambda a: a.astype(MXU_DTYPE)

    xp, xs = x_prompt, x_sample
    outs_p, outs_s = [], []
    for l in range(depth):
        lp = {
            "w_in": _pack_w_in(w_in[l], d), "hgrn_norm_gain": hgrn_norm_gain[l],
            "w_out_a": bf(w_out_a[l]), "w_out_b": bf(w_out_b[l]), "w_out_c": bf(w_out_c[l]), "w_out": bf(w_out[l]),
            "pre_mix_gain": pre_mix_gain[l], "post_mix_gain": post_mix_gain[l], "pre_ffn_gain": pre_ffn_gain[l],
            "post_ffn_gain": post_ffn_gain[l], "w_up": bf(w_up[l]), "w_down": bf(w_down[l]),
        }
        mk32, mv32, mk16, mv16 = _memory_kv(mem_prompt, mem_norm_gain[l], bf(w_mem_kv[l]))
        n_mem = mem_prompt.shape[1]
        row_tile, q_tile = _tiles(tp)
        xp, st_p = _layer(xp, pos_p, lower_bounds[l], lp, cache=None,
                          state0=jnp.zeros((bp, B_HEADS, B_KEY_DIM, B_VAL_DIM), F32), mem16=(mk16, mv16),
                          row_tile=row_tile, q_tile=q_tile, key_tile=512)
        outs_p.append(st_p + (mk32.reshape(bp, n_mem, C_HEADS, C_HEAD_DIM), mv32.reshape(bp, n_mem, C_HEADS, C_HEAD_DIM)))

        cik = cache_idx_k[l]
        zeros_ik = jnp.zeros_like(cik)
        cache = (bf(jnp.concatenate([cik, zeros_ik], axis=-1)), bf(jnp.concatenate([zeros_ik, cik], axis=-1)),
                 bf(cache_k[l].reshape(bs, past, KV_WIDTH)), bf(cache_v[l].reshape(bs, past, KV_WIDTH)))
        mem16 = (bf(cache_mem_k[l].reshape(bs, -1, C_WIDTH)), bf(cache_mem_v[l].reshape(bs, -1, C_WIDTH)))
        row_tile, q_tile = _tiles(ts)
        xs, st_s = _layer(xs, pos_s, lower_bounds[l], lp, cache=cache, state0=state_hgrn[l].astype(F32),
                          mem16=mem16, row_tile=row_tile, q_tile=q_tile, key_tile=384)
        outs_s.append(st_s)

    new_k_p, new_v_p, new_ik_p, new_s_p, new_mk_p, new_mv_p = [jnp.stack(a) for a in zip(*outs_p)]
    new_k_s, new_v_s, new_ik_s, new_s_s = [jnp.stack(a) for a in zip(*outs_s)]
    return (xp, xs, new_k_p, new_v_p, new_ik_p, new_s_p, new_mk_p, new_mv_p, new_k_s, new_v_s, new_ik_s, new_s_s)
```

```python
import functools

import jax
import jax.numpy as jnp
import numpy as np
from jax import lax
from jax.experimental import pallas as pl
from jax.experimental.pallas import tpu as pltpu

CHUNK = 64
N_BRANCHES = 3
A_HEADS, A_KV_HEADS, HEAD_DIM = 6, 2, 128
ROT_DIM = HEAD_DIM // 4
IDX_HEADS, IDX_DIM = 8, 64
IDX_ROT_DIM = IDX_DIM // 4
TOPK_MAX = 256
B_HEADS, B_KEY_DIM, B_VAL_DIM = 6, 128, 128
C_HEADS, C_HEAD_DIM = 4, 128
ROPE_THETA = 500000.0
EPS = 1e-6

A_WIDTH = A_HEADS * HEAD_DIM
KV_WIDTH = A_KV_HEADS * HEAD_DIM
IQ_WIDTH = IDX_HEADS * IDX_DIM
B_WIDTH = B_HEADS * B_KEY_DIM
C_WIDTH = C_HEADS * C_HEAD_DIM

LANES = 128
SUBLANES = 8
VMEM_LIMIT_BYTES = 56 * 1024 * 1024

MXU_DTYPE = jnp.bfloat16
F32 = jnp.float32
I32 = jnp.int32

IW_LANE = 96
NEG_BIG = -1e30
LOG2E = 1.4426950408889634
ATT_ROWS = 128
INT_MIN = -(2 ** 31)
KEY_NEG_INF = INT_MIN + 0x00800000

OFF_AQ = 0
OFF_AK = OFF_AQ + A_WIDTH
OFF_AV = OFF_AK + KV_WIDTH
OFF_IQ = OFF_AV + KV_WIDTH
OFF_IKW = OFF_IQ + IQ_WIDTH
OFF_BQ = OFF_IKW + LANES
OFF_BF = OFF_BQ + B_WIDTH
OFF_BI = OFF_BF + B_WIDTH
OFF_BG = OFF_BI + B_WIDTH
OFF_CQ = OFF_BG + B_WIDTH
OFF_GATES = OFF_CQ + C_WIDTH


def _sigmoid(x):
    return 1.0 / (1.0 + jnp.exp(-x))


def _dot(a, b):
    return jnp.dot(a, b, preferred_element_type=F32)


def _dot_nt(a, b):
    return lax.dot_general(a, b, (((1,), (1,)), ((), ())), preferred_element_type=F32)


def _rms(x, gain):
    return x * lax.rsqrt(jnp.mean(x * x, axis=-1, keepdims=True) + EPS) * gain


def _resident(shape):
    zeros = (0,) * len(shape)
    return pl.BlockSpec(shape, lambda *_: zeros, pipeline_mode=pl.Buffered(1))


def _rope(y, cos, sin_up, sin_down, half):
    return y * cos + pltpu.roll(y, half, 1) * sin_up + pltpu.roll(y, LANES - half, 1) * sin_down


def _in_proj_kernel(x_ref, gain_ref, lb_ref, w_ref, tri_ref, ca_ref, sau_ref, sad_ref, ci_ref, siu_ref, sid_ref,
                    aq_ref, k32_ref, v32_ref, k16_ref, v16_ref, iq_ref, ikw_ref, ika_ref, ikb_ref,
                    hq_ref, hk_ref, hlf_ref, hv_ref, hg_ref, cq_ref, gt_ref):
    hb = _rms(x_ref[...], gain_ref[...]).astype(MXU_DTYPE)

    def proj(c0, width):
        return _dot(hb, w_ref[:, c0:c0 + width])

    ca, sau, sad = ca_ref[...], sau_ref[...], sad_ref[...]
    ci, siu, sid = ci_ref[...], siu_ref[...], sid_ref[...]

    y = proj(OFF_AQ, A_WIDTH)
    for h in range(A_HEADS):
        sl = slice(h * LANES, (h + 1) * LANES)
        aq_ref[:, sl] = (_rope(y[:, sl], ca, sau, sad, ROT_DIM // 2) * ((HEAD_DIM ** -0.5) * LOG2E)).astype(MXU_DTYPE)

    y = proj(OFF_AK, KV_WIDTH)
    for h in range(A_KV_HEADS):
        sl = slice(h * LANES, (h + 1) * LANES)
        r = _rope(y[:, sl], ca, sau, sad, ROT_DIM // 2)
        k32_ref[:, sl] = r
        k16_ref[:, sl] = r.astype(MXU_DTYPE)

    y = proj(OFF_AV, KV_WIDTH)
    v32_ref[...] = y
    v16_ref[...] = y.astype(MXU_DTYPE)

    y = proj(OFF_IQ, IQ_WIDTH)
    for j in range(IQ_WIDTH // LANES):
        sl = slice(j * LANES, (j + 1) * LANES)
        iq_ref[:, sl] = _rope(y[:, sl], ci, siu, sid, IDX_ROT_DIM // 2).astype(MXU_DTYPE)

    r = _rope(proj(OFF_IKW, LANES), ci, siu, sid, IDX_ROT_DIM // 2)
    lane = lax.broadcasted_iota(I32, r.shape, 1)
    ikw_ref[...] = jnp.where(lane < IDX_DIM, r, r * ((IDX_DIM ** -0.5) * (IDX_HEADS ** -0.5)))
    ik_low = jnp.where(lane < IDX_DIM, r, 0.0)
    ika_ref[...] = ik_low.astype(MXU_DTYPE)
    ikb_ref[...] = pltpu.roll(ik_low, IDX_DIM, 1).astype(MXU_DTYPE)

    y = proj(OFF_BQ, B_WIDTH)
    hq_ref[...] = (y * _sigmoid(y)).astype(MXU_DTYPE)
    lb = lb_ref[...]
    f = lb + (1.0 - lb) * _sigmoid(proj(OFF_BF, B_WIDTH))
    hk_ref[...] = (1.0 - f).astype(MXU_DTYPE)
    hi, mid, lo = _split3(jnp.log(f))
    tri = tri_ref[...]
    hlf_ref[...] = _dot(tri, hi) + _dot(tri, mid) + _dot(tri, lo)
    hv_ref[...] = proj(OFF_BI, B_WIDTH).astype(MXU_DTYPE)
    y = proj(OFF_BG, B_WIDTH)
    hg_ref[...] = (y * _sigmoid(y)).astype(MXU_DTYPE)

    cq_ref[...] = (proj(OFF_CQ, C_WIDTH) * (C_HEAD_DIM ** -0.5)).astype(MXU_DTYPE)

    n_gate = gt_ref.shape[1]
    for c0 in range(0, n_gate, B_WIDTH):
        gt_ref[:, c0:c0 + B_WIDTH] = _sigmoid(proj(OFF_GATES + c0, B_WIDTH)).astype(MXU_DTYPE)


def _rope_tables(pos, rot_dim, head_dim):
    half = rot_dim // 2
    inv_freq = ROPE_THETA ** (-jnp.arange(half, dtype=F32) / half)
    ang = pos.astype(F32)[:, None] * inv_freq[None, :]
    cos, sin = jnp.cos(ang), jnp.sin(ang)
    lane = np.arange(LANES) % head_dim
    idx = lane % half
    first, second = lane < half, (lane >= half) & (lane < rot_dim)
    cos_t = jnp.where(first | second, cos[:, idx], 1.0)
    sin_up = jnp.where(second, sin[:, idx], 0.0)
    sin_down = jnp.where(first, -sin[:, idx], 0.0)
    return cos_t, sin_up, sin_down


def _in_proj(x, pos, gain, lower_bound, w_packed, row_tile):
    bsz, t, d = x.shape
    n = bsz * t
    tm = row_tile
    assert n % tm == 0 and (t % tm == 0 or tm % t == 0)
    tables = _rope_tables(pos, ROT_DIM, HEAD_DIM) + _rope_tables(pos, IDX_ROT_DIM, IDX_DIM)
    table_rows = max(t, tm)
    tables = [jnp.tile(tb, (table_rows // t, 1)) for tb in tables]
    n_table_blocks = table_rows // tm
    n_gate = N_BRANCHES * d
    w_width = w_packed.shape[1]
    assert tm % CHUNK == 0 and t % CHUNK == 0
    r = np.arange(tm)
    chunk_tri = jnp.asarray((r[:, None] // CHUNK == r[None, :] // CHUNK) & (r[None, :] <= r[:, None]), MXU_DTYPE)

    def rows(width):
        return pl.BlockSpec((tm, width), lambda i: (i, 0))

    table_spec = pl.BlockSpec((tm, LANES), lambda i: (i % n_table_blocks, 0))
    widths_dtypes = [
        (A_WIDTH, MXU_DTYPE), (KV_WIDTH, F32), (KV_WIDTH, F32), (KV_WIDTH, MXU_DTYPE), (KV_WIDTH, MXU_DTYPE),
        (IQ_WIDTH, MXU_DTYPE), (LANES, F32), (LANES, MXU_DTYPE), (LANES, MXU_DTYPE),
        (B_WIDTH, MXU_DTYPE), (B_WIDTH, MXU_DTYPE), (B_WIDTH, F32), (B_WIDTH, MXU_DTYPE), (B_WIDTH, MXU_DTYPE),
        (C_WIDTH, MXU_DTYPE), (n_gate, MXU_DTYPE)]
    outs = pl.pallas_call(
        _in_proj_kernel,
        grid=(n // tm,),
        in_specs=[rows(d), _resident((1, d)), _resident((1, B_WIDTH)), _resident((d, w_width)), _resident((tm, tm))]
        + [table_spec] * 6,
        out_specs=[rows(w) for w, _ in widths_dtypes],
        out_shape=[jax.ShapeDtypeStruct((n, w), dt) for w, dt in widths_dtypes],
        compiler_params=pltpu.CompilerParams(dimension_semantics=("arbitrary",), vmem_limit_bytes=VMEM_LIMIT_BYTES),
        name="in_proj",
    )(x.reshape(n, d), gain.reshape(1, d), lower_bound.reshape(1, B_WIDTH), w_packed, chunk_tri, *tables)
    return [o.reshape(bsz, t, o.shape[-1]) for o in outs]


def _pack_w_in(w, d):
    widths = (A_WIDTH, KV_WIDTH, KV_WIDTH, IQ_WIDTH, IDX_DIM, IDX_HEADS, B_WIDTH, B_WIDTH, B_WIDTH, B_WIDTH,
              C_WIDTH, N_BRANCHES * d)
    points = [int(s) for s in np.cumsum(widths)[:-1]]
    a_q, a_k, a_v, i_q, i_k, i_w, b_q, b_f, b_i, b_g, c_q, gates = jnp.split(w, points, axis=-1)
    ikw = jnp.concatenate([i_k, jnp.zeros((d, IW_LANE - IDX_DIM), w.dtype), i_w,
                           jnp.zeros((d, LANES - IW_LANE - IDX_HEADS), w.dtype)], axis=-1)
    return jnp.concatenate([a_q, a_k, a_v, i_q, ikw, b_q, b_f, b_i, b_g, c_q, gates], axis=-1).astype(MXU_DTYPE)


def _sortable_key(score):
    bits = pltpu.bitcast(score, I32)
    return jnp.where(bits < 0, (bits ^ 0x7FFFFFFF) + 1, bits)


def _lane_tiled(x, n_tiles):
    return x if n_tiles == 1 else jnp.concatenate([x] * n_tiles, axis=1)


COUNT_GROUPS = 8


def _dsa_kernel(aq_ref, iq_ref, iw_ref, ika_ref, ikb_ref, k_ref, v_ref, ya_ref,
                key_sc, bias_sc, m_sc, l_sc, acc_sc, *, tq, tk, key_off, topk, q_sub):
    qi = pl.program_id(1)
    n_vis = key_off + (qi + 1) * tq
    n_kb = lax.div(n_vis + (tk - 1), tk)
    q_pos = qi * tq + lax.broadcasted_iota(I32, (1, tq), 1)
    limit = key_off + (lax.shift_right_logical(q_pos, CHUNK.bit_length() - 1) + 1) * CHUNK

    def score_block(kb, carry):
        k0 = pl.multiple_of(kb * tk, tk)
        ika = ika_ref[0, pl.ds(k0, tk), :]
        ikb = ikb_ref[0, pl.ds(k0, tk), :]
        acc = jnp.zeros((tk, tq), F32)
        for j in range(IQ_WIDTH // LANES):
            iqt = iq_ref[0, :, j * LANES:(j + 1) * LANES]
            acc = acc + jnp.maximum(_dot_nt(ika, iqt), 0.0) * iw_ref[0, 2 * j:2 * j + 1, :]
            acc = acc + jnp.maximum(_dot_nt(ikb, iqt), 0.0) * iw_ref[0, 2 * j + 1:2 * j + 2, :]
        key_pos = k0 + lax.broadcasted_iota(I32, (tk, tq), 0)
        key_sc[kb] = jnp.where(key_pos < limit, _sortable_key(acc), KEY_NEG_INF)
        return carry

    lax.fori_loop(0, n_kb, score_block, 0)

    def count(pred):
        step = SUBLANES * COUNT_GROUPS

        def body(kb, acc):
            for r0 in range(0, tk, step):
                acc = acc + jnp.where(pred(key_sc[kb, r0:r0 + step, :]), 1, 0)
            return acc
        acc = lax.fori_loop(0, n_kb, body, jnp.zeros((step, tq), I32))
        return jnp.sum(acc, axis=0, keepdims=True)

    def key_bit(it, u):
        trial = u | jnp.left_shift(jnp.int32(1), 31 - it)
        cand = trial ^ INT_MIN
        return jnp.where(count(lambda k: k >= cand) >= topk, trial, u)

    u = lax.fori_loop(0, 32, key_bit, jnp.zeros((1, tq), I32))
    thr = u ^ INT_MIN
    n_gt = count(lambda k: k > thr)
    n_ge = count(lambda k: k >= thr)
    has_tie = jnp.where((n_ge > topk) & (thr > KEY_NEG_INF), 1.0, 0.0)

    @pl.when(jnp.max(has_tie) > 0.0)
    def _():
        need = (topk - n_gt).astype(F32)
        ri = lax.broadcasted_iota(I32, (tk, tk), 0)
        ci = lax.broadcasted_iota(I32, (tk, tk), 1)
        lower = jnp.where(ci <= ri, 1.0, 0.0).astype(MXU_DTYPE)
        ones = jnp.ones((SUBLANES, tk), MXU_DTYPE)

        def demote(kb, seen):
            blk = key_sc[kb]
            tie = blk == thr
            t16 = jnp.where(tie, 1.0, 0.0).astype(MXU_DTYPE)
            rank = _dot(lower, t16) + seen
            key_sc[kb] = jnp.where(tie & (rank > need), thr - 1, blk)
            return seen + _dot(ones, t16)[0:1, :]

        lax.fori_loop(0, n_kb, demote, jnp.zeros((1, tq), F32))

    thr_sel = jnp.maximum(thr, KEY_NEG_INF + 1)

    m_sc[...] = jnp.full(m_sc.shape, NEG_BIG, F32)
    l_sc[...] = jnp.zeros(l_sc.shape, F32)
    acc_sc[...] = jnp.zeros(acc_sc.shape, F32)
    group = A_HEADS // A_KV_HEADS
    n_lane_tiles = tk // LANES
    ones_blk = jnp.ones((tk, LANES), MXU_DTYPE)

    def attend_block(kb, carry):
        k0 = pl.multiple_of(kb * tk, tk)
        bias_sc[...] = jnp.where(key_sc[kb] >= thr_sel, 0.0, NEG_BIG).T
        for g in range(A_KV_HEADS):
            kblk = k_ref[0, pl.ds(k0, tk), g * LANES:(g + 1) * LANES]
            vaug = jnp.concatenate([v_ref[0, pl.ds(k0, tk), g * LANES:(g + 1) * LANES], ones_blk], axis=1)
            for r in range(group):
                h = g * group + r
                for rt in range(tq // q_sub):
                    rows = pl.ds(rt * q_sub, q_sub)
                    q = aq_ref[0, rows, h * LANES:(h + 1) * LANES]
                    s = _dot_nt(q, kblk) + bias_sc[rows, :]
                    m_prev = m_sc[h, rows, :]
                    m_new = jnp.maximum(m_prev, jnp.max(s, axis=1, keepdims=True))
                    alpha = jnp.exp2(m_prev - m_new)
                    p = jnp.exp2(s - _lane_tiled(m_new, n_lane_tiles))
                    pv = _dot(p.astype(MXU_DTYPE), vaug)
                    acc_sc[h, rows, :] = alpha * acc_sc[h, rows, :] + pv[:, :LANES]
                    l_sc[h, rows, :] = alpha * l_sc[h, rows, :] + pv[:, LANES:]
                    m_sc[h, rows, :] = m_new
        return carry

    lax.fori_loop(0, n_kb, attend_block, 0)
    for h in range(A_HEADS):
        ya_ref[0, :, h * LANES:(h + 1) * LANES] = (acc_sc[h] / l_sc[h]).astype(ya_ref.dtype)


def _dsa(aq, iq, iw_t, ika, ikb, k16, v16, *, key_off, topk, q_tile, key_tile):
    bsz, t, _ = aq.shape
    lp = k16.shape[1]
    tq, tk = q_tile, key_tile
    assert lp % tk == 0
    n_kb_all = lp // tk
    assert t % tq == 0 and tq % LANES == 0 and tk % (SUBLANES * COUNT_GROUPS) == 0 and key_off + t <= lp
    assert tk >= topk
    kernel = functools.partial(_dsa_kernel, tq=tq, tk=tk, key_off=key_off, topk=topk, q_sub=min(tq, ATT_ROWS))

    def q_spec(width):
        return pl.BlockSpec((1, tq, width), lambda b, i: (b, i, 0))

    def key_spec(width):
        return pl.BlockSpec((1, lp, width), lambda b, i: (b, 0, 0), pipeline_mode=pl.Buffered(1))

    return pl.pallas_call(
        kernel,
        grid=(bsz, t // tq),
        in_specs=[q_spec(A_WIDTH), q_spec(IQ_WIDTH), pl.BlockSpec((1, IDX_HEADS, tq), lambda b, i: (b, 0, i)),
                  key_spec(LANES), key_spec(LANES), key_spec(KV_WIDTH), key_spec(KV_WIDTH)],
        out_specs=q_spec(A_WIDTH),
        out_shape=jax.ShapeDtypeStruct((bsz, t, A_WIDTH), MXU_DTYPE),
        scratch_shapes=[pltpu.VMEM((n_kb_all, tk, tq), I32), pltpu.VMEM((tq, tk), F32),
                        pltpu.VMEM((A_HEADS, tq, LANES), F32), pltpu.VMEM((A_HEADS, tq, LANES), F32),
                        pltpu.VMEM((A_HEADS, tq, LANES), F32)],
        compiler_params=pltpu.CompilerParams(dimension_semantics=("arbitrary", "arbitrary"),
                                             vmem_limit_bytes=VMEM_LIMIT_BYTES),
        name="dsa",
    )(aq, iq, iw_t, ika, ikb, k16, v16)


SUB = SUBLANES
HGRN_ROWS = 4
N_SUB = CHUNK // SUB
INTER_ROWS = SUB * (N_SUB * (N_SUB - 1) // 2)


def _split3(x):
    hi = x.astype(MXU_DTYPE)
    r1 = x - hi.astype(F32)
    mid = r1.astype(MXU_DTYPE)
    lo = (r1 - mid.astype(F32)).astype(MXU_DTYPE)
    return hi, mid, lo


def _hgrn_constants():
    t = np.arange(CHUNK)
    place = ((np.arange(SUB * LANES)[:, None] // LANES) == (t[None, :] % SUB)).astype(np.float32)
    intra = ((t[:, None] // SUB == t[None, :] // SUB) & (t[None, :] <= t[:, None])).astype(np.float32)
    seg = np.concatenate([np.full(i * SUB, i) for i in range(1, N_SUB)])
    inter = ((t[:, None] // SUB) == seg[None, :]).astype(np.float32)
    return jnp.asarray(place, MXU_DTYPE), jnp.asarray(intra, F32), jnp.asarray(inter, F32)


def _hgrn_kernel(q_ref, k_ref, b_ref, v_ref, g_ref, gain_ref, s0_ref, place_ref, intra_ref, inter_ref,
                 yb_ref, sout_ref, st_sc):
    c = pl.program_id(1)
    n_rows = q_ref.shape[0]

    @pl.when(c == 0)
    def _():
        for i in range(n_rows):
            for h in range(B_HEADS):
                st_sc[i, h] = s0_ref[i, h].T

    def row_bcast(x, r, n):
        return jnp.broadcast_to(x[r:r + 1, :], (n, x.shape[1]))

    place = place_ref[...]
    gain = gain_ref[...]
    for i in range(n_rows):
        q = q_ref[i].astype(F32)
        k = k_ref[i].astype(F32)
        v16 = v_ref[i]

        b = b_ref[i]

        b_start = jnp.concatenate(
            [jnp.zeros((SUB, B_WIDTH), F32)] + [row_bcast(b, j * SUB - 1, SUB) for j in range(1, N_SUB)], axis=0)
        q_hat = (q * jnp.exp(b - b_start)).astype(MXU_DTYPE)
        q_state = (q * jnp.exp(b)).astype(MXU_DTYPE)
        b_end = b[CHUNK - 1:CHUNK, :]
        k_end = (k * jnp.exp(b_end - b)).astype(MXU_DTYPE)

        k_hat = jnp.concatenate(
            [k[:j * SUB] * jnp.exp(row_bcast(b, j * SUB - 1, j * SUB) - b[:j * SUB]) for j in range(1, N_SUB)],
            axis=0).astype(MXU_DTYPE)
        v_hat = jnp.concatenate([v16[:j * SUB] for j in range(1, N_SUB)], axis=0)

        q3 = q.reshape(N_SUB, SUB, B_WIDTH)
        k3 = k.reshape(N_SUB, SUB, B_WIDTH)
        b3 = b.reshape(N_SUB, SUB, B_WIDTH)
        intra = []
        for s in range(SUB):
            kp = jnp.broadcast_to(k3[:, s:s + 1, :], q3.shape)
            bp = jnp.broadcast_to(b3[:, s:s + 1, :], q3.shape)
            x = q3 * kp * jnp.exp(jnp.minimum(b3 - bp, 0.0))
            intra.append(x.reshape(CHUNK, B_WIDTH).astype(MXU_DTYPE))

        for h in range(B_HEADS):
            sl = slice(h * LANES, (h + 1) * LANES)
            st = st_sc[i, h]
            x_cat = jnp.concatenate([x[:, sl] for x in intra], axis=1)
            p_intra = (_dot(x_cat, place) * intra_ref[...]).astype(MXU_DTYPE)
            p_inter = (_dot_nt(q_hat[:, sl], k_hat[:, sl]) * inter_ref[...]).astype(MXU_DTYPE)
            o = (_dot(p_inter, v_hat[:, sl]) + _dot(p_intra, v16[:, sl])
                 + _dot_nt(q_state[:, sl], st.astype(MXU_DTYPE)))
            st_sc[i, h] = st * jnp.exp(b_end[:, sl]) + _dot(v16[:, sl].astype(F32).T.astype(MXU_DTYPE), k_end[:, sl])
            y = _rms(o, gain[:, sl]) * g_ref[i, :, sl].astype(F32)
            yb_ref[i, :, sl] = y.astype(yb_ref.dtype)

    @pl.when(c == pl.num_programs(1) - 1)
    def _():
        for i in range(n_rows):
            for h in range(B_HEADS):
                sout_ref[i, h] = st_sc[i, h].T


def _hgrn(hq, hk, hlf, hv, hg, norm_gain, state0):
    bsz, t, _ = hq.shape
    assert t % CHUNK == 0 and bsz % HGRN_ROWS == 0
    nb = HGRN_ROWS
    consts = _hgrn_constants()
    seq_spec = pl.BlockSpec((nb, CHUNK, B_WIDTH), lambda b, c: (b, c, 0))
    state_spec = pl.BlockSpec((nb, B_HEADS, B_KEY_DIM, B_VAL_DIM), lambda b, c: (b, 0, 0, 0))
    return pl.pallas_call(
        _hgrn_kernel,
        grid=(bsz // nb, t // CHUNK),
        in_specs=[seq_spec] * 5 + [_resident((1, B_WIDTH)), state_spec] + [_resident(x.shape) for x in consts],
        out_specs=[seq_spec, state_spec],
        out_shape=[jax.ShapeDtypeStruct((bsz, t, B_WIDTH), MXU_DTYPE),
                   jax.ShapeDtypeStruct((bsz, B_HEADS, B_KEY_DIM, B_VAL_DIM), F32)],
        scratch_shapes=[pltpu.VMEM((nb, B_HEADS, B_VAL_DIM, B_KEY_DIM), F32)],
        compiler_params=pltpu.CompilerParams(dimension_semantics=("arbitrary", "arbitrary"),
                                             vmem_limit_bytes=VMEM_LIMIT_BYTES),
        name="hgrn",
    )(hq, hk, hlf, hv, hg, norm_gain.reshape(1, B_WIDTH), state0, *consts)


def _memory_kv_kernel(mem_ref, gain_ref, w_ref, k32_ref, v32_ref, k16_ref, v16_ref):
    hb = _rms(mem_ref[0], gain_ref[...]).astype(MXU_DTYPE)
    mk = _dot(hb, w_ref[:, :C_WIDTH])
    mv = _dot(hb, w_ref[:, C_WIDTH:])
    k32_ref[0] = mk
    v32_ref[0] = mv
    k16_ref[0] = mk.astype(MXU_DTYPE)
    v16_ref[0] = mv.astype(MXU_DTYPE)


def _memory_kv(mem, gain, w16):
    bsz, n, d = mem.shape
    spec = pl.BlockSpec((1, n, C_WIDTH), lambda b: (b, 0, 0))
    return pl.pallas_call(
        _memory_kv_kernel,
        grid=(bsz,),
        in_specs=[pl.BlockSpec((1, n, d), lambda b: (b, 0, 0)), _resident((1, d)), _resident((d, 2 * C_WIDTH))],
        out_specs=[spec] * 4,
        out_shape=[jax.ShapeDtypeStruct((bsz, n, C_WIDTH), dt) for dt in (F32, F32, MXU_DTYPE, MXU_DTYPE)],
        compiler_params=pltpu.CompilerParams(dimension_semantics=("arbitrary",), vmem_limit_bytes=VMEM_LIMIT_BYTES),
        name="memory_kv",
    )(mem, gain.reshape(1, d), w16)


def _merge_ffn_kernel(x_ref, ya_ref, yb_ref, cq_ref, mk_ref, mv_ref, gt_ref, wa_ref, wb_ref, wc_ref, wo_ref,
                      g_mix_ref, g_pre_ref, g_post_ref, wu_ref, wd_ref, out_ref):
    d = x_ref.shape[-1]
    heads = []
    for h in range(C_HEADS):
        sl = slice(h * C_HEAD_DIM, (h + 1) * C_HEAD_DIM)
        s = _dot_nt(cq_ref[0, :, sl], mk_ref[0, :, sl])
        p = jnp.exp(s - jnp.max(s, axis=-1, keepdims=True))
        p = p / jnp.sum(p, axis=-1, keepdims=True)
        heads.append(_dot(p.astype(MXU_DTYPE), mv_ref[0, :, sl]).astype(MXU_DTYPE))
    y_c = jnp.concatenate(heads, axis=1)

    merged = (gt_ref[0, :, 0:d].astype(F32) * _dot(ya_ref[0], wa_ref[...])
              + gt_ref[0, :, d:2 * d].astype(F32) * _dot(yb_ref[0], wb_ref[...])
              + gt_ref[0, :, 2 * d:3 * d].astype(F32) * _dot(y_c, wc_ref[...]))
    x1 = x_ref[0] + _rms(_dot(merged.astype(MXU_DTYPE), wo_ref[...]), g_mix_ref[...])
    u = jnp.maximum(_dot(_rms(x1, g_pre_ref[...]).astype(MXU_DTYPE), wu_ref[...]), 0.0)
    down = _dot((u * u).astype(MXU_DTYPE), wd_ref[...])
    out_ref[0] = x1 + _rms(down, g_post_ref[...])


def _merge_ffn(x, ya, yb, cq, mk16, mv16, gates, weights, gains, row_tile):
    bsz, t, d = x.shape
    tm = row_tile
    assert t % tm == 0
    n_mem = mk16.shape[1]

    def rows(width):
        return pl.BlockSpec((1, tm, width), lambda b, i: (b, i, 0))

    mem_spec = pl.BlockSpec((1, n_mem, C_WIDTH), lambda b, i: (b, 0, 0))
    return pl.pallas_call(
        _merge_ffn_kernel,
        grid=(bsz, t // tm),
        in_specs=[rows(d), rows(A_WIDTH), rows(B_WIDTH), rows(C_WIDTH), mem_spec, mem_spec, rows(N_BRANCHES * d)]
        + [_resident(w.shape) for w in weights[:4]] + [_resident((1, d))] * 3
        + [_resident(w.shape) for w in weights[4:]],
        out_specs=rows(d),
        out_shape=jax.ShapeDtypeStruct((bsz, t, d), F32),
        compiler_params=pltpu.CompilerParams(dimension_semantics=("arbitrary", "arbitrary"),
                                             vmem_limit_bytes=VMEM_LIMIT_BYTES),
        name="merge_ffn",
    )(x, ya, yb, cq, mk16, mv16, gates, *weights[:4], *[g.reshape(1, d) for g in gains], *weights[4:])


def _layer(x, pos, lower_bound, lp, *, cache, state0, mem16, row_tile, q_tile, key_tile):
    bsz, t, d = x.shape
    (aq, k32, v32, k16, v16, iq, ikw, ika, ikb, hq, hk, hlf, hv, hg, cq, gates) = _in_proj(
        x, pos, lp["pre_mix_gain"], lower_bound, lp["w_in"], min(row_tile, bsz * t))

    keys = [ika, ikb, k16, v16]
    if cache is not None:
        keys = [jnp.concatenate([c, kn], axis=1) for c, kn in zip(cache, keys)]
    n_keys = keys[0].shape[1]
    t_pad = -(-t // q_tile) * q_tile
    key_off = n_keys - t
    n_pad = -(-(key_off + t_pad) // key_tile) * key_tile
    ika, ikb, k16, v16 = [jnp.pad(kk, ((0, 0), (0, n_pad - n_keys), (0, 0))) for kk in keys]
    iw_t = jnp.swapaxes(ikw[..., IW_LANE:IW_LANE + IDX_HEADS], 1, 2)
    q_pad = ((0, 0), (0, t_pad - t), (0, 0))
    ya = _dsa(jnp.pad(aq, q_pad), jnp.pad(iq, q_pad), jnp.pad(iw_t, ((0, 0), (0, 0), (0, t_pad - t))),
              ika, ikb, k16, v16, key_off=key_off, topk=min(TOPK_MAX, n_keys // 4), q_tile=q_tile,
              key_tile=key_tile)[:, :t]

    yb, s_new = _hgrn(hq, hk, hlf, hv, hg, lp["hgrn_norm_gain"], state0)

    out = _merge_ffn(x, ya, yb, cq, mem16[0], mem16[1], gates,
                     [lp["w_out_a"], lp["w_out_b"], lp["w_out_c"], lp["w_out"], lp["w_up"], lp["w_down"]],
                     [lp["post_mix_gain"], lp["pre_ffn_gain"], lp["post_ffn_gain"]], row_tile=min(row_tile, t))
    new_k = k32.reshape(bsz, t, A_KV_HEADS, HEAD_DIM)
    new_v = v32.reshape(bsz, t, A_KV_HEADS, HEAD_DIM)
    return out, (new_k, new_v, ikw[..., :IDX_DIM], s_new)


def _tiles(t):
    return 256, max(LANES, min(256, t))


def kernel(x_prompt, x_sample, cache_k, cache_v, cache_idx_k, state_hgrn, cache_mem_k, cache_mem_v, mem_prompt, w_in, w_mem_kv, mem_norm_gain, hgrn_lb_logits, hgrn_norm_gain, w_out_a, w_out_b, w_out_c, w_out, pre_mix_gain, post_mix_gain, pre_ffn_gain, post_ffn_gain, w_up, w_down):
    depth = w_in.shape[0]
    d = x_prompt.shape[-1]
    bp, tp, _ = x_prompt.shape
    bs, ts, _ = x_sample.shape
    past = cache_k.shape[2]
    lower_bounds = jnp.cumsum(jax.nn.softmax(hgrn_lb_logits.astype(F32), axis=0), axis=0)
    pos_p = jnp.arange(tp, dtype=I32)
    pos_s = past + jnp.arange(ts, dtype=I32)
    bf = lambda a: a.astype(MXU_DTYPE)

    xp, xs = x_prompt, x_sample
    outs_p, outs_s = [], []
    for l in range(depth):
        lp = {
            "w_in": _pack_w_in(w_in[l], d), "hgrn_norm_gain": hgrn_norm_gain[l],
            "w_out_a": bf(w_out_a[l]), "w_out_b": bf(w_out_b[l]), "w_out_c": bf(w_out_c[l]), "w_out": bf(w_out[l]),
            "pre_mix_gain": pre_mix_gain[l], "post_mix_gain": post_mix_gain[l], "pre_ffn_gain": pre_ffn_gain[l],
            "post_ffn_gain": post_ffn_gain[l], "w_up": bf(w_up[l]), "w_down": bf(w_down[l]),
        }
        mk32, mv32, mk16, mv16 = _memory_kv(mem_prompt, mem_norm_gain[l], bf(w_mem_kv[l]))
        n_mem = mem_prompt.shape[1]
        row_tile, q_tile = _tiles(tp)
        xp, st_p = _layer(xp, pos_p, lower_bounds[l], lp, cache=None,
                          state0=jnp.zeros((bp, B_HEADS, B_KEY_DIM, B_VAL_DIM), F32), mem16=(mk16, mv16),
                          row_tile=row_tile, q_tile=q_tile, key_tile=512)
        outs_p.append(st_p + (mk32.reshape(bp, n_mem, C_HEADS, C_HEAD_DIM), mv32.reshape(bp, n_mem, C_HEADS, C_HEAD_DIM)))

        cik = cache_idx_k[l]
        zeros_ik = jnp.zeros_like(cik)
        cache = (bf(jnp.concatenate([cik, zeros_ik], axis=-1)), bf(jnp.concatenate([zeros_ik, cik], axis=-1)),
                 bf(cache_k[l].reshape(bs, past, KV_WIDTH)), bf(cache_v[l].reshape(bs, past, KV_WIDTH)))
        mem16 = (bf(cache_mem_k[l].reshape(bs, -1, C_WIDTH)), bf(cache_mem_v[l].reshape(bs, -1, C_WIDTH)))
        row_tile, q_tile = _tiles(ts)
        xs, st_s = _layer(xs, pos_s, lower_bounds[l], lp, cache=cache, state0=state_hgrn[l].astype(F32),
                          mem16=mem16, row_tile=row_tile, q_tile=q_tile, key_tile=384)
        outs_s.append(st_s)

    new_k_p, new_v_p, new_ik_p, new_s_p, new_mk_p, new_mv_p = [jnp.stack(a) for a in zip(*outs_p)]
    new_k_s, new_v_s, new_ik_s, new_s_s = [jnp.stack(a) for a in zip(*outs_s)]
    return (xp, xs, new_k_p, new_v_p, new_ik_p, new_s_p, new_mk_p, new_mv_p, new_k_s, new_v_s, new_ik_s, new_s_s)
```

```python
import functools

import jax
import jax.numpy as jnp
import numpy as np
from jax import lax
from jax.experimental import pallas as pl
from jax.experimental.pallas import tpu as pltpu

CHUNK = 64
N_BRANCHES = 3
A_HEADS, A_KV_HEADS, HEAD_DIM = 6, 2, 128
ROT_DIM = HEAD_DIM // 4
IDX_HEADS, IDX_DIM = 8, 64
IDX_ROT_DIM = IDX_DIM // 4
TOPK_MAX = 256
B_HEADS, B_KEY_DIM, B_VAL_DIM = 6, 128, 128
C_HEADS, C_HEAD_DIM = 4, 128
ROPE_THETA = 500000.0
EPS = 1e-6

A_WIDTH = A_HEADS * HEAD_DIM
KV_WIDTH = A_KV_HEADS * HEAD_DIM
IQ_WIDTH = IDX_HEADS * IDX_DIM
B_WIDTH = B_HEADS * B_KEY_DIM
C_WIDTH = C_HEADS * C_HEAD_DIM

LANES = 128
SUBLANES = 8
VMEM_LIMIT_BYTES = 56 * 1024 * 1024

MXU_DTYPE = jnp.bfloat16
F32 = jnp.float32
I32 = jnp.int32

IW_LANE = 96
NEG_BIG = -1e30
LOG2E = 1.4426950408889634
ATT_ROWS = 128
INT_MIN = -(2 ** 31)
KEY_NEG_INF = INT_MIN + 0x00800000

OFF_AQ = 0
OFF_AK = OFF_AQ + A_WIDTH
OFF_AV = OFF_AK + KV_WIDTH
OFF_IQ = OFF_AV + KV_WIDTH
OFF_IKW = OFF_IQ + IQ_WIDTH
OFF_BQ = OFF_IKW + LANES
OFF_BF = OFF_BQ + B_WIDTH
OFF_BI = OFF_BF + B_WIDTH
OFF_BG = OFF_BI + B_WIDTH
OFF_CQ = OFF_BG + B_WIDTH
OFF_GATES = OFF_CQ + C_WIDTH


def _sigmoid(x):
    return 1.0 / (1.0 + jnp.exp(-x))


def _dot(a, b):
    return jnp.dot(a, b, preferred_element_type=F32)


def _dot_nt(a, b):
    return lax.dot_general(a, b, (((1,), (1,)), ((), ())), preferred_element_type=F32)


def _rms(x, gain):
    return x * lax.rsqrt(jnp.mean(x * x, axis=-1, keepdims=True) + EPS) * gain


def _resident(shape):
    zeros = (0,) * len(shape)
    return pl.BlockSpec(shape, lambda *_: zeros, pipeline_mode=pl.Buffered(1))


def _rope(y, cos, sin_up, sin_down, half):
    return y * cos + pltpu.roll(y, half, 1) * sin_up + pltpu.roll(y, LANES - half, 1) * sin_down


def _in_proj_kernel(x_ref, gain_ref, lb_ref, w_ref, tri_ref, ca_ref, sau_ref, sad_ref, ci_ref, siu_ref, sid_ref,
                    aq_ref, k32_ref, v32_ref, k16_ref, v16_ref, iq_ref, ikw_ref, ika_ref, ikb_ref,
                    hq_ref, hk_ref, hlf_ref, hv_ref, hg_ref, cq_ref, gt_ref):
    hb = _rms(x_ref[...], gain_ref[...]).astype(MXU_DTYPE)

    def proj(c0, width):
        return _dot(hb, w_ref[:, c0:c0 + width])

    ca, sau, sad = ca_ref[...], sau_ref[...], sad_ref[...]
    ci, siu, sid = ci_ref[...], siu_ref[...], sid_ref[...]

    y = proj(OFF_AQ, A_WIDTH)
    for h in range(A_HEADS):
        sl = slice(h * LANES, (h + 1) * LANES)
        aq_ref[:, sl] = (_rope(y[:, sl], ca, sau, sad, ROT_DIM // 2) * ((HEAD_DIM ** -0.5) * LOG2E)).astype(MXU_DTYPE)

    y = proj(OFF_AK, KV_WIDTH)
    for h in range(A_KV_HEADS):
        sl = slice(h * LANES, (h + 1) * LANES)
        r = _rope(y[:, sl], ca, sau, sad, ROT_DIM // 2)
        k32_ref[:, sl] = r
        k16_ref[:, sl] = r.astype(MXU_DTYPE)

    y = proj(OFF_AV, KV_WIDTH)
    v32_ref[...] = y
    v16_ref[...] = y.astype(MXU_DTYPE)

    y = proj(OFF_IQ, IQ_WIDTH)
    for j in range(IQ_WIDTH // LANES):
        sl = slice(j * LANES, (j + 1) * LANES)
        iq_ref[:, sl] = _rope(y[:, sl], ci, siu, sid, IDX_ROT_DIM // 2).astype(MXU_DTYPE)

    r = _rope(proj(OFF_IKW, LANES), ci, siu, sid, IDX_ROT_DIM // 2)
    lane = lax.broadcasted_iota(I32, r.shape, 1)
    ikw_ref[...] = jnp.where(lane < IDX_DIM, r, r * ((IDX_DIM ** -0.5) * (IDX_HEADS ** -0.5)))
    ik_low = jnp.where(lane < IDX_DIM, r, 0.0)
    ika_ref[...] = ik_low.astype(MXU_DTYPE)
    ikb_ref[...] = pltpu.roll(ik_low, IDX_DIM, 1).astype(MXU_DTYPE)

    y = proj(OFF_BQ, B_WIDTH)
    hq_ref[...] = (y * _sigmoid(y)).astype(MXU_DTYPE)
    lb = lb_ref[...]
    f = lb + (1.0 - lb) * _sigmoid(proj(OFF_BF, B_WIDTH))
    hk_ref[...] = (1.0 - f).astype(MXU_DTYPE)
    hi, mid, lo = _split3(jnp.log(f))
    tri = tri_ref[...]
    hlf_ref[...] = _dot(tri, hi) + _dot(tri, mid) + _dot(tri, lo)
    hv_ref[...] = proj(OFF_BI, B_WIDTH).astype(MXU_DTYPE)
    y = proj(OFF_BG, B_WIDTH)
    hg_ref[...] = (y * _sigmoid(y)).astype(MXU_DTYPE)

    cq_ref[...] = (proj(OFF_CQ, C_WIDTH) * (C_HEAD_DIM ** -0.5)).astype(MXU_DTYPE)

    n_gate = gt_ref.shape[1]
    for c0 in range(0, n_gate, B_WIDTH):
        gt_ref[:, c0:c0 + B_WIDTH] = _sigmoid(proj(OFF_GATES + c0, B_WIDTH)).astype(MXU_DTYPE)


def _rope_tables(pos, rot_dim, head_dim):
    half = rot_dim // 2
    inv_freq = ROPE_THETA ** (-jnp.arange(half, dtype=F32) / half)
    ang = pos.astype(F32)[:, None] * inv_freq[None, :]
    cos, sin = jnp.cos(ang), jnp.sin(ang)
    lane = np.arange(LANES) % head_dim
    idx = lane % half
    first, second = lane < half, (lane >= half) & (lane < rot_dim)
    cos_t = jnp.where(first | second, cos[:, idx], 1.0)
    sin_up = jnp.where(second, sin[:, idx], 0.0)
    sin_down = jnp.where(first, -sin[:, idx], 0.0)
    return cos_t, sin_up, sin_down


def _in_proj(x, pos, gain, lower_bound, w_packed, row_tile):
    bsz, t, d = x.shape
    n = bsz * t
    tm = row_tile
    assert n % tm == 0 and (t % tm == 0 or tm % t == 0)
    tables = _rope_tables(pos, ROT_DIM, HEAD_DIM) + _rope_tables(pos, IDX_ROT_DIM, IDX_DIM)
    table_rows = max(t, tm)
    tables = [jnp.tile(tb, (table_rows // t, 1)) for tb in tables]
    n_table_blocks = table_rows // tm
    n_gate = N_BRANCHES * d
    w_width = w_packed.shape[1]
    assert tm % CHUNK == 0 and t % CHUNK == 0
    r = np.arange(tm)
    chunk_tri = jnp.asarray((r[:, None] // CHUNK == r[None, :] // CHUNK) & (r[None, :] <= r[:, None]), MXU_DTYPE)

    def rows(width):
        return pl.BlockSpec((tm, width), lambda i: (i, 0))

    table_spec = pl.BlockSpec((tm, LANES), lambda i: (i % n_table_blocks, 0))
    widths_dtypes = [
        (A_WIDTH, MXU_DTYPE), (KV_WIDTH, F32), (KV_WIDTH, F32), (KV_WIDTH, MXU_DTYPE), (KV_WIDTH, MXU_DTYPE),
        (IQ_WIDTH, MXU_DTYPE), (LANES, F32), (LANES, MXU_DTYPE), (LANES, MXU_DTYPE),
        (B_WIDTH, MXU_DTYPE), (B_WIDTH, MXU_DTYPE), (B_WIDTH, F32), (B_WIDTH, MXU_DTYPE), (B_WIDTH, MXU_DTYPE),
        (C_WIDTH, MXU_DTYPE), (n_gate, MXU_DTYPE)]
    outs = pl.pallas_call(
        _in_proj_kernel,
        grid=(n // tm,),
        in_specs=[rows(d), _resident((1, d)), _resident((1, B_WIDTH)), _resident((d, w_width)), _resident((tm, tm))]
        + [table_spec] * 6,
        out_specs=[rows(w) for w, _ in widths_dtypes],
        out_shape=[jax.ShapeDtypeStruct((n, w), dt) for w, dt in widths_dtypes],
        compiler_params=pltpu.CompilerParams(dimension_semantics=("arbitrary",), vmem_limit_bytes=VMEM_LIMIT_BYTES),
        name="in_proj",
    )(x.reshape(n, d), gain.reshape(1, d), lower_bound.reshape(1, B_WIDTH), w_packed, chunk_tri, *tables)
    return [o.reshape(bsz, t, o.shape[-1]) for o in outs]


def _pack_w_in(w, d):
    widths = (A_WIDTH, KV_WIDTH, KV_WIDTH, IQ_WIDTH, IDX_DIM, IDX_HEADS, B_WIDTH, B_WIDTH, B_WIDTH, B_WIDTH,
              C_WIDTH, N_BRANCHES * d)
    points = [int(s) for s in np.cumsum(widths)[:-1]]
    a_q, a_k, a_v, i_q, i_k, i_w, b_q, b_f, b_i, b_g, c_q, gates = jnp.split(w, points, axis=-1)
    ikw = jnp.concatenate([i_k, jnp.zeros((d, IW_LANE - IDX_DIM), w.dtype), i_w,
                           jnp.zeros((d, LANES - IW_LANE - IDX_HEADS), w.dtype)], axis=-1)
    return jnp.concatenate([a_q, a_k, a_v, i_q, ikw, b_q, b_f, b_i, b_g, c_q, gates], axis=-1).astype(MXU_DTYPE)


def _sortable_key(score):
    bits = pltpu.bitcast(score, I32)
    return jnp.where(bits < 0, (bits ^ 0x7FFFFFFF) + 1, bits)


def _lane_tiled(x, n_tiles):
    return x if n_tiles == 1 else jnp.concatenate([x] * n_tiles, axis=1)


COUNT_GROUPS = 8
COUNT_GROUP = 8
COUNT_BY_MATMUL = 6


def _dsa_kernel(aq_ref, iq_ref, iw_ref, ika_ref, ikb_ref, k_ref, v_ref, ya_ref,
                key_sc, bias_sc, m_sc, l_sc, acc_sc, *, tq, tk, key_off, topk, q_sub):
    qi = pl.program_id(1)
    n_vis = key_off + (qi + 1) * tq
    n_kb = lax.div(n_vis + (tk - 1), tk)
    q_pos = qi * tq + lax.broadcasted_iota(I32, (1, tq), 1)
    limit = key_off + (lax.shift_right_logical(q_pos, CHUNK.bit_length() - 1) + 1) * CHUNK

    def for_each_block(body):
        def pair(j, carry):
            body(2 * j, 0)
            body(2 * j + 1, 1)
            return carry

        lax.fori_loop(0, lax.shift_right_logical(n_kb, 1), pair, 0)

        @pl.when((n_kb & 1) == 1)
        def _():
            body(n_kb - 1, 0)

    def score_block(kb, slot):
        k0 = pl.multiple_of(kb * tk, tk)
        ika = ika_ref[0, pl.ds(k0, tk), :]
        ikb = ikb_ref[0, pl.ds(k0, tk), :]
        acc = jnp.zeros((tk, tq), F32)
        for j in range(IQ_WIDTH // LANES):
            iqt = iq_ref[0, :, j * LANES:(j + 1) * LANES]
            acc = acc + jnp.maximum(_dot_nt(ika, iqt), 0.0) * iw_ref[0, 2 * j:2 * j + 1, :]
            acc = acc + jnp.maximum(_dot_nt(ikb, iqt), 0.0) * iw_ref[0, 2 * j + 1:2 * j + 2, :]
        key_pos = k0 + lax.broadcasted_iota(I32, (tk, tq), 0)
        key_sc[kb] = jnp.where(key_pos < limit, _sortable_key(acc), KEY_NEG_INF)

    for_each_block(score_block)

    ones_rows = jnp.ones((2 * SUBLANES, tk), MXU_DTYPE)

    def count(pred):
        step = SUBLANES * COUNT_GROUPS

        def hits(kb):
            return _dot(ones_rows, jnp.where(pred(key_sc[kb]), 1.0, 0.0).astype(MXU_DTYPE))

        def one_block(kb, acc):
            for r0 in range(0, tk, step):
                acc = acc + jnp.where(pred(key_sc[kb, r0:r0 + step, :]), 1, 0)
            return acc

        def group(j, carry):
            by_matmul, by_adds = carry
            kb = COUNT_GROUP * j
            parts = [hits(kb + i) for i in range(COUNT_BY_MATMUL)]
            while len(parts) > 1:
                parts = [a + b for a, b in zip(parts[0::2], parts[1::2])] + parts[len(parts) & ~1:]
            for i in range(COUNT_BY_MATMUL, COUNT_GROUP):
                by_adds = one_block(kb + i, by_adds)
            return by_matmul + parts[0], by_adds

        n_groups = lax.div(n_kb, COUNT_GROUP)
        zeros = (jnp.zeros((2 * SUBLANES, tq), F32), jnp.zeros((step, tq), I32))
        by_matmul, by_adds = lax.fori_loop(0, n_groups, group, zeros)
        by_adds = lax.fori_loop(COUNT_GROUP * n_groups, n_kb, one_block, by_adds)
        return by_matmul[0:1, :].astype(I32) + jnp.sum(by_adds, axis=0, keepdims=True)

    def key_bit(it, u):
        trial = u | jnp.left_shift(jnp.int32(1), 31 - it)
        cand = trial ^ INT_MIN
        return jnp.where(count(lambda k: k >= cand) >= topk, trial, u)

    u = lax.fori_loop(0, 32, key_bit, jnp.zeros((1, tq), I32))
    thr = u ^ INT_MIN
    n_gt = count(lambda k: k > thr)
    n_ge = count(lambda k: k >= thr)
    has_tie = jnp.where((n_ge > topk) & (thr > KEY_NEG_INF), 1.0, 0.0)

    @pl.when(jnp.max(has_tie) > 0.0)
    def _():
        need = (topk - n_gt).astype(F32)
        ri = lax.broadcasted_iota(I32, (tk, tk), 0)
        ci = lax.broadcasted_iota(I32, (tk, tk), 1)
        lower = jnp.where(ci <= ri, 1.0, 0.0).astype(MXU_DTYPE)
        ones = jnp.ones((SUBLANES, tk), MXU_DTYPE)

        def demote(kb, seen):
            blk = key_sc[kb]
            tie = blk == thr
            t16 = jnp.where(tie, 1.0, 0.0).astype(MXU_DTYPE)
            rank = _dot(lower, t16) + seen
            key_sc[kb] = jnp.where(tie & (rank > need), thr - 1, blk)
            return seen + _dot(ones, t16)[0:1, :]

        lax.fori_loop(0, n_kb, demote, jnp.zeros((1, tq), F32))

    thr_sel = jnp.maximum(thr, KEY_NEG_INF + 1)

    m_sc[...] = jnp.full(m_sc.shape, NEG_BIG, F32)
    l_sc[...] = jnp.zeros(l_sc.shape, F32)
    acc_sc[...] = jnp.zeros(acc_sc.shape, F32)
    group = A_HEADS // A_KV_HEADS
    n_lane_tiles = tk // LANES
    ones_blk = jnp.ones((tk, LANES), MXU_DTYPE)

    def attend_block(kb, slot):
        k0 = pl.multiple_of(kb * tk, tk)
        bias_sc[slot] = jnp.where(key_sc[kb] >= thr_sel, 0.0, NEG_BIG).T
        for g in range(A_KV_HEADS):
            kblk = k_ref[0, pl.ds(k0, tk), g * LANES:(g + 1) * LANES]
            vaug = jnp.concatenate([v_ref[0, pl.ds(k0, tk), g * LANES:(g + 1) * LANES], ones_blk], axis=1)
            for r in range(group):
                h = g * group + r
                for rt in range(tq // q_sub):
                    rows = pl.ds(rt * q_sub, q_sub)
                    q = aq_ref[0, rows, h * LANES:(h + 1) * LANES]
                    s = _dot_nt(q, kblk) + bias_sc[slot, rows, :]
                    m_prev = m_sc[h, rows, :]
                    m_new = jnp.maximum(m_prev, jnp.max(s, axis=1, keepdims=True))
                    alpha = jnp.exp2(m_prev - m_new)
                    p = jnp.exp2(s - _lane_tiled(m_new, n_lane_tiles))
                    pv = _dot(p.astype(MXU_DTYPE), vaug)
                    acc_sc[h, rows, :] = alpha * acc_sc[h, rows, :] + pv[:, :LANES]
                    l_sc[h, rows, :] = alpha * l_sc[h, rows, :] + pv[:, LANES:]
                    m_sc[h, rows, :] = m_new
    for_each_block(attend_block)
    for h in range(A_HEADS):
        ya_ref[0, :, h * LANES:(h + 1) * LANES] = (acc_sc[h] / l_sc[h]).astype(ya_ref.dtype)


def _dsa(aq, iq, iw_t, ika, ikb, k16, v16, *, key_off, topk, q_tile, key_tile):
    bsz, t, _ = aq.shape
    lp = k16.shape[1]
    tq, tk = q_tile, key_tile
    assert lp % tk == 0
    n_kb_all = lp // tk
    assert t % tq == 0 and tq % LANES == 0 and tk % (SUBLANES * COUNT_GROUPS) == 0 and key_off + t <= lp
    assert tk >= topk
    kernel = functools.partial(_dsa_kernel, tq=tq, tk=tk, key_off=key_off, topk=topk, q_sub=min(tq, ATT_ROWS))

    def q_spec(width):
        return pl.BlockSpec((1, tq, width), lambda b, i: (b, i, 0))

    def key_spec(width):
        return pl.BlockSpec((1, lp, width), lambda b, i: (b, 0, 0), pipeline_mode=pl.Buffered(1))

    return pl.pallas_call(
        kernel,
        grid=(bsz, t // tq),
        in_specs=[q_spec(A_WIDTH), q_spec(IQ_WIDTH), pl.BlockSpec((1, IDX_HEADS, tq), lambda b, i: (b, 0, i)),
                  key_spec(LANES), key_spec(LANES), key_spec(KV_WIDTH), key_spec(KV_WIDTH)],
        out_specs=q_spec(A_WIDTH),
        out_shape=jax.ShapeDtypeStruct((bsz, t, A_WIDTH), MXU_DTYPE),
        scratch_shapes=[pltpu.VMEM((n_kb_all, tk, tq), I32), pltpu.VMEM((2, tq, tk), F32),
                        pltpu.VMEM((A_HEADS, tq, LANES), F32), pltpu.VMEM((A_HEADS, tq, LANES), F32),
                        pltpu.VMEM((A_HEADS, tq, LANES), F32)],
        compiler_params=pltpu.CompilerParams(dimension_semantics=("arbitrary", "arbitrary"),
                                             vmem_limit_bytes=VMEM_LIMIT_BYTES),
        name="dsa",
    )(aq, iq, iw_t, ika, ikb, k16, v16)


SUB = SUBLANES
HGRN_ROWS = 4
N_SUB = CHUNK // SUB
INTER_ROWS = SUB * (N_SUB * (N_SUB - 1) // 2)


def _split3(x):
    hi = x.astype(MXU_DTYPE)
    r1 = x - hi.astype(F32)
    mid = r1.astype(MXU_DTYPE)
    lo = (r1 - mid.astype(F32)).astype(MXU_DTYPE)
    return hi, mid, lo


def _hgrn_constants():
    t = np.arange(CHUNK)
    place = ((np.arange(SUB * LANES)[:, None] // LANES) == (t[None, :] % SUB)).astype(np.float32)
    intra = ((t[:, None] // SUB == t[None, :] // SUB) & (t[None, :] <= t[:, None])).astype(np.float32)
    seg = np.concatenate([np.full(i * SUB, i) for i in range(1, N_SUB)])
    inter = ((t[:, None] // SUB) == seg[None, :]).astype(np.float32)
    return jnp.asarray(place, MXU_DTYPE), jnp.asarray(intra, F32), jnp.asarray(inter, F32)


def _hgrn_kernel(q_ref, k_ref, b_ref, v_ref, g_ref, gain_ref, s0_ref, place_ref, intra_ref, inter_ref,
                 yb_ref, sout_ref, st_sc):
    c = pl.program_id(1)
    n_rows = q_ref.shape[0]

    @pl.when(c == 0)
    def _():
        for i in range(n_rows):
            for h in range(B_HEADS):
                st_sc[i, h] = s0_ref[i, h].T

    def row_bcast(x, r, n):
        return jnp.broadcast_to(x[r:r + 1, :], (n, x.shape[1]))

    place = place_ref[...]
    gain = gain_ref[...]
    for i in range(n_rows):
        q = q_ref[i].astype(F32)
        k = k_ref[i].astype(F32)
        v16 = v_ref[i]

        b = b_ref[i]

        b_start = jnp.concatenate(
            [jnp.zeros((SUB, B_WIDTH), F32)] + [row_bcast(b, j * SUB - 1, SUB) for j in range(1, N_SUB)], axis=0)
        q_hat = (q * jnp.exp(b - b_start)).astype(MXU_DTYPE)
        q_state = (q * jnp.exp(b)).astype(MXU_DTYPE)
        b_end = b[CHUNK - 1:CHUNK, :]
        k_end = (k * jnp.exp(b_end - b)).astype(MXU_DTYPE)

        k_hat = jnp.concatenate(
            [k[:j * SUB] * jnp.exp(row_bcast(b, j * SUB - 1, j * SUB) - b[:j * SUB]) for j in range(1, N_SUB)],
            axis=0).astype(MXU_DTYPE)
        v_hat = jnp.concatenate([v16[:j * SUB] for j in range(1, N_SUB)], axis=0)

        q3 = q.reshape(N_SUB, SUB, B_WIDTH)
        k3 = k.reshape(N_SUB, SUB, B_WIDTH)
        b3 = b.reshape(N_SUB, SUB, B_WIDTH)
        intra = []
        for s in range(SUB):
            kp = jnp.broadcast_to(k3[:, s:s + 1, :], q3.shape)
            bp = jnp.broadcast_to(b3[:, s:s + 1, :], q3.shape)
            x = q3 * kp * jnp.exp(jnp.minimum(b3 - bp, 0.0))
            intra.append(x.reshape(CHUNK, B_WIDTH).astype(MXU_DTYPE))

        for h in range(B_HEADS):
            sl = slice(h * LANES, (h + 1) * LANES)
            st = st_sc[i, h]
            x_cat = jnp.concatenate([x[:, sl] for x in intra], axis=1)
            p_intra = (_dot(x_cat, place) * intra_ref[...]).astype(MXU_DTYPE)
            p_inter = (_dot_nt(q_hat[:, sl], k_hat[:, sl]) * inter_ref[...]).astype(MXU_DTYPE)
            o = (_dot(p_inter, v_hat[:, sl]) + _dot(p_intra, v16[:, sl])
                 + _dot_nt(q_state[:, sl], st.astype(MXU_DTYPE)))
            st_sc[i, h] = st * jnp.exp(b_end[:, sl]) + _dot(v16[:, sl].astype(F32).T.astype(MXU_DTYPE), k_end[:, sl])
            y = _rms(o, gain[:, sl]) * g_ref[i, :, sl].astype(F32)
            yb_ref[i, :, sl] = y.astype(yb_ref.dtype)

    @pl.when(c == pl.num_programs(1) - 1)
    def _():
        for i in range(n_rows):
            for h in range(B_HEADS):
                sout_ref[i, h] = st_sc[i, h].T


def _hgrn(hq, hk, hlf, hv, hg, norm_gain, state0):
    bsz, t, _ = hq.shape
    assert t % CHUNK == 0 and bsz % HGRN_ROWS == 0
    nb = HGRN_ROWS
    consts = _hgrn_constants()
    seq_spec = pl.BlockSpec((nb, CHUNK, B_WIDTH), lambda b, c: (b, c, 0))
    state_spec = pl.BlockSpec((nb, B_HEADS, B_KEY_DIM, B_VAL_DIM), lambda b, c: (b, 0, 0, 0))
    return pl.pallas_call(
        _hgrn_kernel,
        grid=(bsz // nb, t // CHUNK),
        in_specs=[seq_spec] * 5 + [_resident((1, B_WIDTH)), state_spec] + [_resident(x.shape) for x in consts],
        out_specs=[seq_spec, state_spec],
        out_shape=[jax.ShapeDtypeStruct((bsz, t, B_WIDTH), MXU_DTYPE),
                   jax.ShapeDtypeStruct((bsz, B_HEADS, B_KEY_DIM, B_VAL_DIM), F32)],
        scratch_shapes=[pltpu.VMEM((nb, B_HEADS, B_VAL_DIM, B_KEY_DIM), F32)],
        compiler_params=pltpu.CompilerParams(dimension_semantics=("arbitrary", "arbitrary"),
                                             vmem_limit_bytes=VMEM_LIMIT_BYTES),
        name="hgrn",
    )(hq, hk, hlf, hv, hg, norm_gain.reshape(1, B_WIDTH), state0, *consts)


def _memory_kv_kernel(mem_ref, gain_ref, w_ref, k32_ref, v32_ref, k16_ref, v16_ref):
    hb = _rms(mem_ref[0], gain_ref[...]).astype(MXU_DTYPE)
    mk = _dot(hb, w_ref[:, :C_WIDTH])
    mv = _dot(hb, w_ref[:, C_WIDTH:])
    k32_ref[0] = mk
    v32_ref[0] = mv
    k16_ref[0] = mk.astype(MXU_DTYPE)
    v16_ref[0] = mv.astype(MXU_DTYPE)


def _memory_kv(mem, gain, w16):
    bsz, n, d = mem.shape
    spec = pl.BlockSpec((1, n, C_WIDTH), lambda b: (b, 0, 0))
    return pl.pallas_call(
        _memory_kv_kernel,
        grid=(bsz,),
        in_specs=[pl.BlockSpec((1, n, d), lambda b: (b, 0, 0)), _resident((1, d)), _resident((d, 2 * C_WIDTH))],
        out_specs=[spec] * 4,
        out_shape=[jax.ShapeDtypeStruct((bsz, n, C_WIDTH), dt) for dt in (F32, F32, MXU_DTYPE, MXU_DTYPE)],
        compiler_params=pltpu.CompilerParams(dimension_semantics=("arbitrary",), vmem_limit_bytes=VMEM_LIMIT_BYTES),
        name="memory_kv",
    )(mem, gain.reshape(1, d), w16)


def _merge_ffn_kernel(x_ref, ya_ref, yb_ref, cq_ref, mk_ref, mv_ref, gt_ref, wa_ref, wb_ref, wc_ref, wo_ref,
                      g_mix_ref, g_pre_ref, g_post_ref, wu_ref, wd_ref, out_ref):
    d = x_ref.shape[-1]
    heads = []
    for h in range(C_HEADS):
        sl = slice(h * C_HEAD_DIM, (h + 1) * C_HEAD_DIM)
        s = _dot_nt(cq_ref[0, :, sl], mk_ref[0, :, sl])
        p = jnp.exp(s - jnp.max(s, axis=-1, keepdims=True))
        p = p / jnp.sum(p, axis=-1, keepdims=True)
        heads.append(_dot(p.astype(MXU_DTYPE), mv_ref[0, :, sl]).astype(MXU_DTYPE))
    y_c = jnp.concatenate(heads, axis=1)

    merged = (gt_ref[0, :, 0:d].astype(F32) * _dot(ya_ref[0], wa_ref[...])
              + gt_ref[0, :, d:2 * d].astype(F32) * _dot(yb_ref[0], wb_ref[...])
              + gt_ref[0, :, 2 * d:3 * d].astype(F32) * _dot(y_c, wc_ref[...]))
    x1 = x_ref[0] + _rms(_dot(merged.astype(MXU_DTYPE), wo_ref[...]), g_mix_ref[...])
    u = jnp.maximum(_dot(_rms(x1, g_pre_ref[...]).astype(MXU_DTYPE), wu_ref[...]), 0.0)
    down = _dot((u * u).astype(MXU_DTYPE), wd_ref[...])
    out_ref[0] = x1 + _rms(down, g_post_ref[...])


def _merge_ffn(x, ya, yb, cq, mk16, mv16, gates, weights, gains, row_tile):
    bsz, t, d = x.shape
    tm = row_tile
    assert t % tm == 0
    n_mem = mk16.shape[1]

    def rows(width):
        return pl.BlockSpec((1, tm, width), lambda b, i: (b, i, 0))

    mem_spec = pl.BlockSpec((1, n_mem, C_WIDTH), lambda b, i: (b, 0, 0))
    return pl.pallas_call(
        _merge_ffn_kernel,
        grid=(bsz, t // tm),
        in_specs=[rows(d), rows(A_WIDTH), rows(B_WIDTH), rows(C_WIDTH), mem_spec, mem_spec, rows(N_BRANCHES * d)]
        + [_resident(w.shape) for w in weights[:4]] + [_resident((1, d))] * 3
        + [_resident(w.shape) for w in weights[4:]],
        out_specs=rows(d),
        out_shape=jax.ShapeDtypeStruct((bsz, t, d), F32),
        compiler_params=pltpu.CompilerParams(dimension_semantics=("arbitrary", "arbitrary"),
                                             vmem_limit_bytes=VMEM_LIMIT_BYTES),
        name="merge_ffn",
    )(x, ya, yb, cq, mk16, mv16, gates, *weights[:4], *[g.reshape(1, d) for g in gains], *weights[4:])


def _layer(x, pos, lower_bound, lp, *, cache, state0, mem16, row_tile, q_tile, key_tile):
    bsz, t, d = x.shape
    (aq, k32, v32, k16, v16, iq, ikw, ika, ikb, hq, hk, hlf, hv, hg, cq, gates) = _in_proj(
        x, pos, lp["pre_mix_gain"], lower_bound, lp["w_in"], min(row_tile, bsz * t))

    keys = [ika, ikb, k16, v16]
    if cache is not None:
        keys = [jnp.concatenate([c, kn], axis=1) for c, kn in zip(cache, keys)]
    n_keys = keys[0].shape[1]
    t_pad = -(-t // q_tile) * q_tile
    key_off = n_keys - t
    n_pad = -(-(key_off + t_pad) // key_tile) * key_tile
    ika, ikb, k16, v16 = [jnp.pad(kk, ((0, 0), (0, n_pad - n_keys), (0, 0))) for kk in keys]
    iw_t = jnp.swapaxes(ikw[..., IW_LANE:IW_LANE + IDX_HEADS], 1, 2)
    q_pad = ((0, 0), (0, t_pad - t), (0, 0))
    ya = _dsa(jnp.pad(aq, q_pad), jnp.pad(iq, q_pad), jnp.pad(iw_t, ((0, 0), (0, 0), (0, t_pad - t))),
              ika, ikb, k16, v16, key_off=key_off, topk=min(TOPK_MAX, n_keys // 4), q_tile=q_tile,
              key_tile=key_tile)[:, :t]

    yb, s_new = _hgrn(hq, hk, hlf, hv, hg, lp["hgrn_norm_gain"], state0)

    out = _merge_ffn(x, ya, yb, cq, mem16[0], mem16[1], gates,
                     [lp["w_out_a"], lp["w_out_b"], lp["w_out_c"], lp["w_out"], lp["w_up"], lp["w_down"]],
                     [lp["post_mix_gain"], lp["pre_ffn_gain"], lp["post_ffn_gain"]], row_tile=min(MERGE_ROWS, t))
    new_k = k32.reshape(bsz, t, A_KV_HEADS, HEAD_DIM)
    new_v = v32.reshape(bsz, t, A_KV_HEADS, HEAD_DIM)
    return out, (new_k, new_v, ikw[..., :IDX_DIM], s_new)


MERGE_ROWS = 512


def _tiles(t):
    return 256, max(LANES, min(256, t))


def kernel(x_prompt, x_sample, cache_k, cache_v, cache_idx_k, state_hgrn, cache_mem_k, cache_mem_v, mem_prompt, w_in, w_mem_kv, mem_norm_gain, hgrn_lb_logits, hgrn_norm_gain, w_out_a, w_out_b, w_out_c, w_out, pre_mix_gain, post_mix_gain, pre_ffn_gain, post_ffn_gain, w_up, w_down):
    depth = w_in.shape[0]
    d = x_prompt.shape[-1]
    bp, tp, _ = x_prompt.shape
    bs, ts, _ = x_sample.shape
    past = cache_k.shape[2]
    lower_bounds = jnp.cumsum(jax.nn.softmax(hgrn_lb_logits.astype(F32), axis=0), axis=0)
    pos_p = jnp.arange(tp, dtype=I32)
    pos_s = past + jnp.arange(ts, dtype=I32)
    bf = lambda a: a.astype(MXU_DTYPE)

    xp, xs = x_prompt, x_sample
    outs_p, outs_s = [], []
    for l in range(depth):
        lp = {
            "w_in": _pack_w_in(w_in[l], d), "hgrn_norm_gain": hgrn_norm_gain[l],
            "w_out_a": bf(w_out_a[l]), "w_out_b": bf(w_out_b[l]), "w_out_c": bf(w_out_c[l]), "w_out": bf(w_out[l]),
            "pre_mix_gain": pre_mix_gain[l], "post_mix_gain": post_mix_gain[l], "pre_ffn_gain": pre_ffn_gain[l],
            "post_ffn_gain": post_ffn_gain[l], "w_up": bf(w_up[l]), "w_down": bf(w_down[l]),
        }
        mk32, mv32, mk16, mv16 = _memory_kv(mem_prompt, mem_norm_gain[l], bf(w_mem_kv[l]))
        n_mem = mem_prompt.shape[1]
        row_tile, q_tile = _tiles(tp)
        xp, st_p = _layer(xp, pos_p, lower_bounds[l], lp, cache=None,
                          state0=jnp.zeros((bp, B_HEADS, B_KEY_DIM, B_VAL_DIM), F32), mem16=(mk16, mv16),
                          row_tile=row_tile, q_tile=q_tile, key_tile=512)
        outs_p.append(st_p + (mk32.reshape(bp, n_mem, C_HEADS, C_HEAD_DIM), mv32.reshape(bp, n_mem, C_HEADS, C_HEAD_DIM)))

        cik = cache_idx_k[l]
        zeros_ik = jnp.zeros_like(cik)
        cache = (bf(jnp.concatenate([cik, zeros_ik], axis=-1)), bf(jnp.concatenate([zeros_ik, cik], axis=-1)),
                 bf(cache_k[l].reshape(bs, past, KV_WIDTH)), bf(cache_v[l].reshape(bs, past, KV_WIDTH)))
        mem16 = (bf(cache_mem_k[l].reshape(bs, -1, C_WIDTH)), bf(cache_mem_v[l].reshape(bs, -1, C_WIDTH)))
        row_tile, q_tile = _tiles(ts)
        xs, st_s = _layer(xs, pos_s, lower_bounds[l], lp, cache=cache, state0=state_hgrn[l].astype(F32),
                          mem16=mem16, row_tile=row_tile, q_tile=q_tile, key_tile=384)
        outs_s.append(st_s)

    new_k_p, new_v_p, new_ik_p, new_s_p, new_mk_p, new_mv_p = [jnp.stack(a) for a in zip(*outs_p)]
    new_k_s, new_v_s, new_ik_s, new_s_s = [jnp.stack(a) for a in zip(*outs_s)]
    return (xp, xs, new_k_p, new_v_p, new_ik_p, new_s_p, new_mk_p, new_mv_p, new_k_s, new_v_s, new_ik_s, new_s_s)
```

```python
import functools

import jax
import jax.numpy as jnp
import numpy as np
from jax import lax
from jax.experimental import pallas as pl
from jax.experimental.pallas import tpu as pltpu

CHUNK = 64
N_BRANCHES = 3
A_HEADS, A_KV_HEADS, HEAD_DIM = 6, 2, 128
ROT_DIM = HEAD_DIM // 4
IDX_HEADS, IDX_DIM = 8, 64
IDX_ROT_DIM = IDX_DIM // 4
TOPK_MAX = 256
B_HEADS, B_KEY_DIM, B_VAL_DIM = 6, 128, 128
C_HEADS, C_HEAD_DIM = 4, 128
ROPE_THETA = 500000.0
EPS = 1e-6

A_WIDTH = A_HEADS * HEAD_DIM
KV_WIDTH = A_KV_HEADS * HEAD_DIM
IQ_WIDTH = IDX_HEADS * IDX_DIM
B_WIDTH = B_HEADS * B_KEY_DIM
C_WIDTH = C_HEADS * C_HEAD_DIM

LANES = 128
SUBLANES = 8
VMEM_LIMIT_BYTES = 56 * 1024 * 1024

MXU_DTYPE = jnp.bfloat16
F32 = jnp.float32
I32 = jnp.int32

IW_LANE = 96
NEG_BIG = -1e30
LOG2E = 1.4426950408889634
ATT_ROWS = 128
INT_MIN = -(2 ** 31)
KEY_NEG_INF = INT_MIN + 0x00800000

OFF_AQ = 0
OFF_AK = OFF_AQ + A_WIDTH
OFF_AV = OFF_AK + KV_WIDTH
OFF_IQ = OFF_AV + KV_WIDTH
OFF_IKW = OFF_IQ + IQ_WIDTH
OFF_BQ = OFF_IKW + LANES
OFF_BF = OFF_BQ + B_WIDTH
OFF_BI = OFF_BF + B_WIDTH
OFF_BG = OFF_BI + B_WIDTH
OFF_CQ = OFF_BG + B_WIDTH
OFF_GATES = OFF_CQ + C_WIDTH


def _sigmoid(x):
    return 1.0 / (1.0 + jnp.exp(-x))


def _dot(a, b):
    return jnp.dot(a, b, preferred_element_type=F32)


def _dot_nt(a, b):
    return lax.dot_general(a, b, (((1,), (1,)), ((), ())), preferred_element_type=F32)


def _rms(x, gain):
    return x * lax.rsqrt(jnp.mean(x * x, axis=-1, keepdims=True) + EPS) * gain


def _resident(shape):
    zeros = (0,) * len(shape)
    return pl.BlockSpec(shape, lambda *_: zeros, pipeline_mode=pl.Buffered(1))


def _rope(y, cos, sin_up, sin_down, half):
    return y * cos + pltpu.roll(y, half, 1) * sin_up + pltpu.roll(y, LANES - half, 1) * sin_down


def _in_proj_kernel(x_ref, gain_ref, lb_ref, w_ref, tri_ref, ca_ref, sau_ref, sad_ref, ci_ref, siu_ref, sid_ref,
                    aq_ref, k32_ref, v32_ref, k16_ref, v16_ref, iq_ref, ikw_ref, ika_ref, ikb_ref,
                    hq_ref, hk_ref, hlf_ref, hv_ref, hg_ref, cq_ref, gt_ref):
    hb = _rms(x_ref[...], gain_ref[...]).astype(MXU_DTYPE)

    def proj(c0, width):
        return _dot(hb, w_ref[:, c0:c0 + width])

    ca, sau, sad = ca_ref[...], sau_ref[...], sad_ref[...]
    ci, siu, sid = ci_ref[...], siu_ref[...], sid_ref[...]

    y = proj(OFF_AQ, A_WIDTH)
    for h in range(A_HEADS):
        sl = slice(h * LANES, (h + 1) * LANES)
        aq_ref[:, sl] = (_rope(y[:, sl], ca, sau, sad, ROT_DIM // 2) * ((HEAD_DIM ** -0.5) * LOG2E)).astype(MXU_DTYPE)

    y = proj(OFF_AK, KV_WIDTH)
    for h in range(A_KV_HEADS):
        sl = slice(h * LANES, (h + 1) * LANES)
        r = _rope(y[:, sl], ca, sau, sad, ROT_DIM // 2)
        k32_ref[:, sl] = r
        k16_ref[:, sl] = r.astype(MXU_DTYPE)

    y = proj(OFF_AV, KV_WIDTH)
    v32_ref[...] = y
    v16_ref[...] = y.astype(MXU_DTYPE)

    y = proj(OFF_IQ, IQ_WIDTH)
    for j in range(IQ_WIDTH // LANES):
        sl = slice(j * LANES, (j + 1) * LANES)
        iq_ref[:, sl] = _rope(y[:, sl], ci, siu, sid, IDX_ROT_DIM // 2).astype(MXU_DTYPE)

    r = _rope(proj(OFF_IKW, LANES), ci, siu, sid, IDX_ROT_DIM // 2)
    lane = lax.broadcasted_iota(I32, r.shape, 1)
    ikw_ref[...] = jnp.where(lane < IDX_DIM, r, r * ((IDX_DIM ** -0.5) * (IDX_HEADS ** -0.5)))
    ik_low = jnp.where(lane < IDX_DIM, r, 0.0)
    ika_ref[...] = ik_low.astype(MXU_DTYPE)
    ikb_ref[...] = pltpu.roll(ik_low, IDX_DIM, 1).astype(MXU_DTYPE)

    y = proj(OFF_BQ, B_WIDTH)
    hq_ref[...] = (y * _sigmoid(y)).astype(MXU_DTYPE)
    lb = lb_ref[...]
    f = lb + (1.0 - lb) * _sigmoid(proj(OFF_BF, B_WIDTH))
    hk_ref[...] = (1.0 - f).astype(MXU_DTYPE)
    hi, mid, lo = _split3(jnp.log(f))
    tri = tri_ref[...]
    hlf_ref[...] = _dot(tri, hi) + _dot(tri, mid) + _dot(tri, lo)
    hv_ref[...] = proj(OFF_BI, B_WIDTH).astype(MXU_DTYPE)
    y = proj(OFF_BG, B_WIDTH)
    hg_ref[...] = (y * _sigmoid(y)).astype(MXU_DTYPE)

    cq_ref[...] = (proj(OFF_CQ, C_WIDTH) * (C_HEAD_DIM ** -0.5)).astype(MXU_DTYPE)

    n_gate = gt_ref.shape[1]
    for c0 in range(0, n_gate, B_WIDTH):
        gt_ref[:, c0:c0 + B_WIDTH] = _sigmoid(proj(OFF_GATES + c0, B_WIDTH)).astype(MXU_DTYPE)


def _rope_tables(pos, rot_dim, head_dim):
    half = rot_dim // 2
    inv_freq = ROPE_THETA ** (-jnp.arange(half, dtype=F32) / half)
    ang = pos.astype(F32)[:, None] * inv_freq[None, :]
    cos, sin = jnp.cos(ang), jnp.sin(ang)
    lane = np.arange(LANES) % head_dim
    idx = lane % half
    first, second = lane < half, (lane >= half) & (lane < rot_dim)
    cos_t = jnp.where(first | second, cos[:, idx], 1.0)
    sin_up = jnp.where(second, sin[:, idx], 0.0)
    sin_down = jnp.where(first, -sin[:, idx], 0.0)
    return cos_t, sin_up, sin_down


def _in_proj(x, pos, gain, lower_bound, w_packed, row_tile):
    bsz, t, d = x.shape
    n = bsz * t
    tm = row_tile
    assert n % tm == 0 and (t % tm == 0 or tm % t == 0)
    tables = _rope_tables(pos, ROT_DIM, HEAD_DIM) + _rope_tables(pos, IDX_ROT_DIM, IDX_DIM)
    table_rows = max(t, tm)
    tables = [jnp.tile(tb, (table_rows // t, 1)) for tb in tables]
    n_table_blocks = table_rows // tm
    n_gate = N_BRANCHES * d
    w_width = w_packed.shape[1]
    assert tm % CHUNK == 0 and t % CHUNK == 0
    r = np.arange(tm)
    chunk_tri = jnp.asarray((r[:, None] // CHUNK == r[None, :] // CHUNK) & (r[None, :] <= r[:, None]), MXU_DTYPE)

    def rows(width):
        return pl.BlockSpec((tm, width), lambda i: (i, 0))

    table_spec = pl.BlockSpec((tm, LANES), lambda i: (i % n_table_blocks, 0))
    widths_dtypes = [
        (A_WIDTH, MXU_DTYPE), (KV_WIDTH, F32), (KV_WIDTH, F32), (KV_WIDTH, MXU_DTYPE), (KV_WIDTH, MXU_DTYPE),
        (IQ_WIDTH, MXU_DTYPE), (LANES, F32), (LANES, MXU_DTYPE), (LANES, MXU_DTYPE),
        (B_WIDTH, MXU_DTYPE), (B_WIDTH, MXU_DTYPE), (B_WIDTH, F32), (B_WIDTH, MXU_DTYPE), (B_WIDTH, MXU_DTYPE),
        (C_WIDTH, MXU_DTYPE), (n_gate, MXU_DTYPE)]
    outs = pl.pallas_call(
        _in_proj_kernel,
        grid=(n // tm,),
        in_specs=[rows(d), _resident((1, d)), _resident((1, B_WIDTH)), _resident((d, w_width)), _resident((tm, tm))]
        + [table_spec] * 6,
        out_specs=[rows(w) for w, _ in widths_dtypes],
        out_shape=[jax.ShapeDtypeStruct((n, w), dt) for w, dt in widths_dtypes],
        compiler_params=pltpu.CompilerParams(dimension_semantics=("arbitrary",), vmem_limit_bytes=VMEM_LIMIT_BYTES),
        name="in_proj",
    )(x.reshape(n, d), gain.reshape(1, d), lower_bound.reshape(1, B_WIDTH), w_packed, chunk_tri, *tables)
    return [o.reshape(bsz, t, o.shape[-1]) for o in outs]


def _pack_w_in(w, d):
    widths = (A_WIDTH, KV_WIDTH, KV_WIDTH, IQ_WIDTH, IDX_DIM, IDX_HEADS, B_WIDTH, B_WIDTH, B_WIDTH, B_WIDTH,
              C_WIDTH, N_BRANCHES * d)
    points = [int(s) for s in np.cumsum(widths)[:-1]]
    a_q, a_k, a_v, i_q, i_k, i_w, b_q, b_f, b_i, b_g, c_q, gates = jnp.split(w, points, axis=-1)
    ikw = jnp.concatenate([i_k, jnp.zeros((d, IW_LANE - IDX_DIM), w.dtype), i_w,
                           jnp.zeros((d, LANES - IW_LANE - IDX_HEADS), w.dtype)], axis=-1)
    return jnp.concatenate([a_q, a_k, a_v, i_q, ikw, b_q, b_f, b_i, b_g, c_q, gates], axis=-1).astype(MXU_DTYPE)


def _sortable_key(score):
    bits = pltpu.bitcast(score, I32)
    return jnp.where(bits < 0, (bits ^ 0x7FFFFFFF) + 1, bits)


def _lane_tiled(x, n_tiles):
    return x if n_tiles == 1 else jnp.concatenate([x] * n_tiles, axis=1)


COUNT_GROUPS = 8
COUNT_GROUP = 8
COUNT_BY_MATMUL = 6
TIE_ROWS = 128


def _dsa_kernel(aq_ref, iq_ref, iw_ref, ika_ref, ikb_ref, k_ref, v_ref, ya_ref,
                key_sc, bias_sc, m_sc, l_sc, acc_sc, *, tq, tk, key_off, topk, q_sub):
    qi = pl.program_id(1)
    n_vis = key_off + (qi + 1) * tq
    n_kb = lax.div(n_vis + (tk - 1), tk)
    q_pos = qi * tq + lax.broadcasted_iota(I32, (1, tq), 1)
    limit = key_off + (lax.shift_right_logical(q_pos, CHUNK.bit_length() - 1) + 1) * CHUNK

    def for_each_block(body):
        def pair(j, carry):
            body(2 * j, 0)
            body(2 * j + 1, 1)
            return carry

        lax.fori_loop(0, lax.shift_right_logical(n_kb, 1), pair, 0)

        @pl.when((n_kb & 1) == 1)
        def _():
            body(n_kb - 1, 0)

    def score_block(kb, slot):
        k0 = pl.multiple_of(kb * tk, tk)
        ika = ika_ref[0, pl.ds(k0, tk), :]
        ikb = ikb_ref[0, pl.ds(k0, tk), :]
        acc = jnp.zeros((tk, tq), F32)
        for j in range(IQ_WIDTH // LANES):
            iqt = iq_ref[0, :, j * LANES:(j + 1) * LANES]
            acc = acc + jnp.maximum(_dot_nt(ika, iqt), 0.0) * iw_ref[0, 2 * j:2 * j + 1, :]
            acc = acc + jnp.maximum(_dot_nt(ikb, iqt), 0.0) * iw_ref[0, 2 * j + 1:2 * j + 2, :]
        key_pos = k0 + lax.broadcasted_iota(I32, (tk, tq), 0)
        key_sc[kb] = jnp.where(key_pos < limit, _sortable_key(acc), KEY_NEG_INF)

    for_each_block(score_block)

    ones_rows = jnp.ones((2 * SUBLANES, tk), MXU_DTYPE)

    def count(pred):
        step = SUBLANES * COUNT_GROUPS

        def hits(kb):
            return _dot(ones_rows, jnp.where(pred(key_sc[kb]), 1.0, 0.0).astype(MXU_DTYPE))

        def one_block(kb, acc):
            for r0 in range(0, tk, step):
                acc = acc + jnp.where(pred(key_sc[kb, r0:r0 + step, :]), 1, 0)
            return acc

        def group(j, carry):
            by_matmul, by_adds = carry
            kb = COUNT_GROUP * j
            parts = [hits(kb + i) for i in range(COUNT_BY_MATMUL)]
            while len(parts) > 1:
                parts = [a + b for a, b in zip(parts[0::2], parts[1::2])] + parts[len(parts) & ~1:]
            for i in range(COUNT_BY_MATMUL, COUNT_GROUP):
                by_adds = one_block(kb + i, by_adds)
            return by_matmul + parts[0], by_adds

        n_groups = lax.div(n_kb, COUNT_GROUP)
        zeros = (jnp.zeros((2 * SUBLANES, tq), F32), jnp.zeros((step, tq), I32))
        by_matmul, by_adds = lax.fori_loop(0, n_groups, group, zeros)
        by_adds = lax.fori_loop(COUNT_GROUP * n_groups, n_kb, one_block, by_adds)
        return by_matmul[0:1, :].astype(I32) + jnp.sum(by_adds, axis=0, keepdims=True)

    def key_bit(it, u):
        trial = u | jnp.left_shift(jnp.int32(1), 31 - it)
        cand = trial ^ INT_MIN
        return jnp.where(count(lambda k: k >= cand) >= topk, trial, u)

    u = lax.fori_loop(0, 32, key_bit, jnp.zeros((1, tq), I32))
    thr = u ^ INT_MIN
    n_gt = count(lambda k: k > thr)
    n_ge = count(lambda k: k >= thr)
    has_tie = jnp.where((n_ge > topk) & (thr > KEY_NEG_INF), 1.0, 0.0)

    @pl.when(jnp.max(has_tie) > 0.0)
    def _():
        need = (topk - n_gt).astype(F32)
        ri = lax.broadcasted_iota(I32, (TIE_ROWS, TIE_ROWS), 0)
        ci = lax.broadcasted_iota(I32, (TIE_ROWS, TIE_ROWS), 1)
        lower = jnp.where(ci <= ri, 1.0, 0.0).astype(MXU_DTYPE)

        def demote(kb, seen):
            pieces = []
            for r0 in range(0, tk, TIE_ROWS):
                blk = key_sc[kb, r0:r0 + TIE_ROWS, :]
                tie = blk == thr
                pieces.append((r0, blk, tie, _dot(lower, jnp.where(tie, 1.0, 0.0).astype(MXU_DTYPE))))
            for r0, blk, tie, prefix in pieces:
                rank = prefix + seen
                key_sc[kb, r0:r0 + TIE_ROWS, :] = jnp.where(tie & (rank > need), thr - 1, blk)
                seen = rank[TIE_ROWS - 1:TIE_ROWS, :]
            return seen

        lax.fori_loop(0, n_kb, demote, jnp.zeros((1, tq), F32))

    thr_sel = jnp.maximum(thr, KEY_NEG_INF + 1)

    m_sc[...] = jnp.full(m_sc.shape, NEG_BIG, F32)
    l_sc[...] = jnp.zeros(l_sc.shape, F32)
    acc_sc[...] = jnp.zeros(acc_sc.shape, F32)
    group = A_HEADS // A_KV_HEADS
    n_lane_tiles = tk // LANES
    ones_blk = jnp.ones((tk, LANES), MXU_DTYPE)

    def attend_block(kb, slot):
        k0 = pl.multiple_of(kb * tk, tk)
        bias_sc[slot] = jnp.where(key_sc[kb] >= thr_sel, 0.0, NEG_BIG).T
        for g in range(A_KV_HEADS):
            kblk = k_ref[0, pl.ds(k0, tk), g * LANES:(g + 1) * LANES]
            vaug = jnp.concatenate([v_ref[0, pl.ds(k0, tk), g * LANES:(g + 1) * LANES], ones_blk], axis=1)
            for r in range(group):
                h = g * group + r
                for rt in range(tq // q_sub):
                    rows = pl.ds(rt * q_sub, q_sub)
                    q = aq_ref[0, rows, h * LANES:(h + 1) * LANES]
                    s = _dot_nt(q, kblk) + bias_sc[slot, rows, :]
                    m_prev = m_sc[h, rows, :]
                    m_new = jnp.maximum(m_prev, jnp.max(s, axis=1, keepdims=True))
                    alpha = jnp.exp2(m_prev - m_new)
                    p = jnp.exp2(s - _lane_tiled(m_new, n_lane_tiles))
                    pv = _dot(p.astype(MXU_DTYPE), vaug)
                    acc_sc[h, rows, :] = alpha * acc_sc[h, rows, :] + pv[:, :LANES]
                    l_sc[h, rows, :] = alpha * l_sc[h, rows, :] + pv[:, LANES:]
                    m_sc[h, rows, :] = m_new
    for_each_block(attend_block)
    for h in range(A_HEADS):
        ya_ref[0, :, h * LANES:(h + 1) * LANES] = (acc_sc[h] / l_sc[h]).astype(ya_ref.dtype)


def _dsa(aq, iq, iw_t, ika, ikb, k16, v16, *, key_off, topk, q_tile, key_tile):
    bsz, t, _ = aq.shape
    lp = k16.shape[1]
    tq, tk = q_tile, key_tile
    assert lp % tk == 0
    n_kb_all = lp // tk
    assert t % tq == 0 and tq % LANES == 0 and tk % (SUBLANES * COUNT_GROUPS) == 0 and key_off + t <= lp
    assert tk >= topk
    kernel = functools.partial(_dsa_kernel, tq=tq, tk=tk, key_off=key_off, topk=topk, q_sub=min(tq, ATT_ROWS))

    def q_spec(width):
        return pl.BlockSpec((1, tq, width), lambda b, i: (b, i, 0))

    def key_spec(width):
        return pl.BlockSpec((1, lp, width), lambda b, i: (b, 0, 0), pipeline_mode=pl.Buffered(1))

    return pl.pallas_call(
        kernel,
        grid=(bsz, t // tq),
        in_specs=[q_spec(A_WIDTH), q_spec(IQ_WIDTH), pl.BlockSpec((1, IDX_HEADS, tq), lambda b, i: (b, 0, i)),
                  key_spec(LANES), key_spec(LANES), key_spec(KV_WIDTH), key_spec(KV_WIDTH)],
        out_specs=q_spec(A_WIDTH),
        out_shape=jax.ShapeDtypeStruct((bsz, t, A_WIDTH), MXU_DTYPE),
        scratch_shapes=[pltpu.VMEM((n_kb_all, tk, tq), I32), pltpu.VMEM((2, tq, tk), F32),
                        pltpu.VMEM((A_HEADS, tq, LANES), F32), pltpu.VMEM((A_HEADS, tq, LANES), F32),
                        pltpu.VMEM((A_HEADS, tq, LANES), F32)],
        compiler_params=pltpu.CompilerParams(dimension_semantics=("arbitrary", "arbitrary"),
                                             vmem_limit_bytes=VMEM_LIMIT_BYTES),
        name="dsa",
    )(aq, iq, iw_t, ika, ikb, k16, v16)


SUB = SUBLANES
HGRN_ROWS = 4
HGRN_GROUP_HEADS = 2
N_SUB = CHUNK // SUB
INTER_ROWS = SUB * (N_SUB * (N_SUB - 1) // 2)


def _split3(x):
    hi = x.astype(MXU_DTYPE)
    r1 = x - hi.astype(F32)
    mid = r1.astype(MXU_DTYPE)
    lo = (r1 - mid.astype(F32)).astype(MXU_DTYPE)
    return hi, mid, lo


def _hgrn_constants():
    t = np.arange(CHUNK)
    place = ((np.arange(SUB * LANES)[:, None] // LANES) == (t[None, :] % SUB)).astype(np.float32)
    intra = ((t[:, None] // SUB == t[None, :] // SUB) & (t[None, :] <= t[:, None])).astype(np.float32)
    seg = np.concatenate([np.full(i * SUB, i) for i in range(1, N_SUB)])
    inter = ((t[:, None] // SUB) == seg[None, :]).astype(np.float32)
    return jnp.asarray(place, MXU_DTYPE), jnp.asarray(intra, F32), jnp.asarray(inter, F32)


def _hgrn_kernel(q_ref, k_ref, b_ref, v_ref, g_ref, gain_ref, s0_ref, place_ref, intra_ref, inter_ref,
                 yb_ref, sout_ref, st_sc):
    c = pl.program_id(1)
    n_rows = q_ref.shape[0]

    @pl.when(c == 0)
    def _():
        for i in range(n_rows):
            for h in range(B_HEADS):
                st_sc[i, h] = s0_ref[i, h].T

    def row_bcast(x, r, n):
        return jnp.broadcast_to(x[r:r + 1, :], (n, x.shape[1]))

    place = place_ref[...]

    def head_group(i, g):
        width = HGRN_GROUP_HEADS * LANES
        lanes = slice(g * width, (g + 1) * width)
        q = q_ref[i, :, lanes].astype(F32)
        k = k_ref[i, :, lanes].astype(F32)
        v16 = v_ref[i, :, lanes]
        b = b_ref[i, :, lanes]

        b_start = jnp.concatenate(
            [jnp.zeros((SUB, width), F32)] + [row_bcast(b, j * SUB - 1, SUB) for j in range(1, N_SUB)], axis=0)
        q_hat = (q * jnp.exp(b - b_start)).astype(MXU_DTYPE)
        q_state = (q * jnp.exp(b)).astype(MXU_DTYPE)
        b_end = b[CHUNK - 1:CHUNK, :]
        k_end = (k * jnp.exp(b_end - b)).astype(MXU_DTYPE)

        k_hat = jnp.concatenate(
            [k[:j * SUB] * jnp.exp(row_bcast(b, j * SUB - 1, j * SUB) - b[:j * SUB]) for j in range(1, N_SUB)],
            axis=0).astype(MXU_DTYPE)
        v_hat = jnp.concatenate([v16[:j * SUB] for j in range(1, N_SUB)], axis=0)

        q3 = q.reshape(N_SUB, SUB, width)
        k3 = k.reshape(N_SUB, SUB, width)
        b3 = b.reshape(N_SUB, SUB, width)
        intra = []
        for s in range(SUB):
            kp = jnp.broadcast_to(k3[:, s:s + 1, :], q3.shape)
            bp = jnp.broadcast_to(b3[:, s:s + 1, :], q3.shape)
            x = q3 * kp * jnp.exp(jnp.minimum(b3 - bp, 0.0))
            intra.append(x.reshape(CHUNK, width).astype(MXU_DTYPE))

        for hh in range(HGRN_GROUP_HEADS):
            h = g * HGRN_GROUP_HEADS + hh
            sl = slice(hh * LANES, (hh + 1) * LANES)
            out_lanes = slice(h * LANES, (h + 1) * LANES)
            st = st_sc[i, h]
            x_cat = jnp.concatenate([x[:, sl] for x in intra], axis=1)
            p_intra = (_dot(x_cat, place) * intra_ref[...]).astype(MXU_DTYPE)
            p_inter = (_dot_nt(q_hat[:, sl], k_hat[:, sl]) * inter_ref[...]).astype(MXU_DTYPE)
            o = (_dot(p_inter, v_hat[:, sl]) + _dot(p_intra, v16[:, sl])
                 + _dot_nt(q_state[:, sl], st.astype(MXU_DTYPE)))
            st_sc[i, h] = st * jnp.exp(b_end[:, sl]) + _dot(v16[:, sl].astype(F32).T.astype(MXU_DTYPE), k_end[:, sl])
            y = _rms(o, gain_ref[:, out_lanes]) * g_ref[i, :, out_lanes].astype(F32)
            yb_ref[i, :, out_lanes] = y.astype(yb_ref.dtype)

    for i in range(n_rows):
        for g in range(B_HEADS // HGRN_GROUP_HEADS):
            head_group(i, g)

    @pl.when(c == pl.num_programs(1) - 1)
    def _():
        for i in range(n_rows):
            for h in range(B_HEADS):
                sout_ref[i, h] = st_sc[i, h].T


def _hgrn(hq, hk, hlf, hv, hg, norm_gain, state0):
    bsz, t, _ = hq.shape
    assert t % CHUNK == 0 and bsz % HGRN_ROWS == 0
    nb = HGRN_ROWS
    consts = _hgrn_constants()
    seq_spec = pl.BlockSpec((nb, CHUNK, B_WIDTH), lambda b, c: (b, c, 0))
    state_spec = pl.BlockSpec((nb, B_HEADS, B_KEY_DIM, B_VAL_DIM), lambda b, c: (b, 0, 0, 0))
    return pl.pallas_call(
        _hgrn_kernel,
        grid=(bsz // nb, t // CHUNK),
        in_specs=[seq_spec] * 5 + [_resident((1, B_WIDTH)), state_spec] + [_resident(x.shape) for x in consts],
        out_specs=[seq_spec, state_spec],
        out_shape=[jax.ShapeDtypeStruct((bsz, t, B_WIDTH), MXU_DTYPE),
                   jax.ShapeDtypeStruct((bsz, B_HEADS, B_KEY_DIM, B_VAL_DIM), F32)],
        scratch_shapes=[pltpu.VMEM((nb, B_HEADS, B_VAL_DIM, B_KEY_DIM), F32)],
        compiler_params=pltpu.CompilerParams(dimension_semantics=("arbitrary", "arbitrary"),
                                             vmem_limit_bytes=VMEM_LIMIT_BYTES),
        name="hgrn",
    )(hq, hk, hlf, hv, hg, norm_gain.reshape(1, B_WIDTH), state0, *consts)


def _memory_kv_kernel(mem_ref, gain_ref, w_ref, k32_ref, v32_ref, k16_ref, v16_ref):
    hb = _rms(mem_ref[0], gain_ref[...]).astype(MXU_DTYPE)
    mk = _dot(hb, w_ref[:, :C_WIDTH])
    mv = _dot(hb, w_ref[:, C_WIDTH:])
    k32_ref[0] = mk
    v32_ref[0] = mv
    k16_ref[0] = mk.astype(MXU_DTYPE)
    v16_ref[0] = mv.astype(MXU_DTYPE)


def _memory_kv(mem, gain, w16):
    bsz, n, d = mem.shape
    spec = pl.BlockSpec((1, n, C_WIDTH), lambda b: (b, 0, 0))
    return pl.pallas_call(
        _memory_kv_kernel,
        grid=(bsz,),
        in_specs=[pl.BlockSpec((1, n, d), lambda b: (b, 0, 0)), _resident((1, d)), _resident((d, 2 * C_WIDTH))],
        out_specs=[spec] * 4,
        out_shape=[jax.ShapeDtypeStruct((bsz, n, C_WIDTH), dt) for dt in (F32, F32, MXU_DTYPE, MXU_DTYPE)],
        compiler_params=pltpu.CompilerParams(dimension_semantics=("arbitrary",), vmem_limit_bytes=VMEM_LIMIT_BYTES),
        name="memory_kv",
    )(mem, gain.reshape(1, d), w16)


def _merge_ffn_kernel(x_ref, ya_ref, yb_ref, cq_ref, mk_ref, mv_ref, gt_ref, wa_ref, wb_ref, wc_ref, wo_ref,
                      g_mix_ref, g_pre_ref, g_post_ref, wu_ref, wd_ref, out_ref):
    d = x_ref.shape[-1]
    heads = []
    for h in range(C_HEADS):
        sl = slice(h * C_HEAD_DIM, (h + 1) * C_HEAD_DIM)
        s = _dot_nt(cq_ref[0, :, sl], mk_ref[0, :, sl])
        p = jnp.exp(s - jnp.max(s, axis=-1, keepdims=True))
        p = p / jnp.sum(p, axis=-1, keepdims=True)
        heads.append(_dot(p.astype(MXU_DTYPE), mv_ref[0, :, sl]).astype(MXU_DTYPE))
    y_c = jnp.concatenate(heads, axis=1)

    merged = (gt_ref[0, :, 0:d].astype(F32) * _dot(ya_ref[0], wa_ref[...])
              + gt_ref[0, :, d:2 * d].astype(F32) * _dot(yb_ref[0], wb_ref[...])
              + gt_ref[0, :, 2 * d:3 * d].astype(F32) * _dot(y_c, wc_ref[...]))
    x1 = x_ref[0] + _rms(_dot(merged.astype(MXU_DTYPE), wo_ref[...]), g_mix_ref[...])
    u = jnp.maximum(_dot(_rms(x1, g_pre_ref[...]).astype(MXU_DTYPE), wu_ref[...]), 0.0)
    down = _dot((u * u).astype(MXU_DTYPE), wd_ref[...])
    out_ref[0] = x1 + _rms(down, g_post_ref[...])


def _merge_ffn(x, ya, yb, cq, mk16, mv16, gates, weights, gains, row_tile):
    bsz, t, d = x.shape
    tm = row_tile
    assert t % tm == 0
    n_mem = mk16.shape[1]

    def rows(width):
        return pl.BlockSpec((1, tm, width), lambda b, i: (b, i, 0))

    mem_spec = pl.BlockSpec((1, n_mem, C_WIDTH), lambda b, i: (b, 0, 0))
    return pl.pallas_call(
        _merge_ffn_kernel,
        grid=(bsz, t // tm),
        in_specs=[rows(d), rows(A_WIDTH), rows(B_WIDTH), rows(C_WIDTH), mem_spec, mem_spec, rows(N_BRANCHES * d)]
        + [_resident(w.shape) for w in weights[:4]] + [_resident((1, d))] * 3
        + [_resident(w.shape) for w in weights[4:]],
        out_specs=rows(d),
        out_shape=jax.ShapeDtypeStruct((bsz, t, d), F32),
        compiler_params=pltpu.CompilerParams(dimension_semantics=("arbitrary", "arbitrary"),
                                             vmem_limit_bytes=VMEM_LIMIT_BYTES),
        name="merge_ffn",
    )(x, ya, yb, cq, mk16, mv16, gates, *weights[:4], *[g.reshape(1, d) for g in gains], *weights[4:])


def _layer(x, pos, lower_bound, lp, *, cache, state0, mem16, row_tile, q_tile, key_tile):
    bsz, t, d = x.shape
    (aq, k32, v32, k16, v16, iq, ikw, ika, ikb, hq, hk, hlf, hv, hg, cq, gates) = _in_proj(
        x, pos, lp["pre_mix_gain"], lower_bound, lp["w_in"], min(row_tile, bsz * t))

    keys = [ika, ikb, k16, v16]
    if cache is not None:
        keys = [jnp.concatenate([c, kn], axis=1) for c, kn in zip(cache, keys)]
    n_keys = keys[0].shape[1]
    t_pad = -(-t // q_tile) * q_tile
    key_off = n_keys - t
    n_pad = -(-(key_off + t_pad) // key_tile) * key_tile
    ika, ikb, k16, v16 = [jnp.pad(kk, ((0, 0), (0, n_pad - n_keys), (0, 0))) for kk in keys]
    iw_t = jnp.swapaxes(ikw[..., IW_LANE:IW_LANE + IDX_HEADS], 1, 2)
    q_pad = ((0, 0), (0, t_pad - t), (0, 0))
    ya = _dsa(jnp.pad(aq, q_pad), jnp.pad(iq, q_pad), jnp.pad(iw_t, ((0, 0), (0, 0), (0, t_pad - t))),
              ika, ikb, k16, v16, key_off=key_off, topk=min(TOPK_MAX, n_keys // 4), q_tile=q_tile,
              key_tile=key_tile)[:, :t]

    yb, s_new = _hgrn(hq, hk, hlf, hv, hg, lp["hgrn_norm_gain"], state0)

    out = _merge_ffn(x, ya, yb, cq, mem16[0], mem16[1], gates,
                     [lp["w_out_a"], lp["w_out_b"], lp["w_out_c"], lp["w_out"], lp["w_up"], lp["w_down"]],
                     [lp["post_mix_gain"], lp["pre_ffn_gain"], lp["post_ffn_gain"]], row_tile=min(MERGE_ROWS, t))
    new_k = k32.reshape(bsz, t, A_KV_HEADS, HEAD_DIM)
    new_v = v32.reshape(bsz, t, A_KV_HEADS, HEAD_DIM)
    return out, (new_k, new_v, ikw[..., :IDX_DIM], s_new)


MERGE_ROWS = 512


def _tiles(t):
    return 256, max(LANES, min(256, t))


def kernel(x_prompt, x_sample, cache_k, cache_v, cache_idx_k, state_hgrn, cache_mem_k, cache_mem_v, mem_prompt, w_in, w_mem_kv, mem_norm_gain, hgrn_lb_logits, hgrn_norm_gain, w_out_a, w_out_b, w_out_c, w_out, pre_mix_gain, post_mix_gain, pre_ffn_gain, post_ffn_gain, w_up, w_down):
    depth = w_in.shape[0]
    d = x_prompt.shape[-1]
    bp, tp, _ = x_prompt.shape
    bs, ts, _ = x_sample.shape
    past = cache_k.shape[2]
    lower_bounds = jnp.cumsum(jax.nn.softmax(hgrn_lb_logits.astype(F32), axis=0), axis=0)
    pos_p = jnp.arange(tp, dtype=I32)
    pos_s = past + jnp.arange(ts, dtype=I32)
    bf = lambda a: a.astype(MXU_DTYPE)

    xp, xs = x_prompt, x_sample
    outs_p, outs_s = [], []
    for l in range(depth):
        lp = {
            "w_in": _pack_w_in(w_in[l], d), "hgrn_norm_gain": hgrn_norm_gain[l],
            "w_out_a": bf(w_out_a[l]), "w_out_b": bf(w_out_b[l]), "w_out_c": bf(w_out_c[l]), "w_out": bf(w_out[l]),
            "pre_mix_gain": pre_mix_gain[l], "post_mix_gain": post_mix_gain[l], "pre_ffn_gain": pre_ffn_gain[l],
            "post_ffn_gain": post_ffn_gain[l], "w_up": bf(w_up[l]), "w_down": bf(w_down[l]),
        }
        mk32, mv32, mk16, mv16 = _memory_kv(mem_prompt, mem_norm_gain[l], bf(w_mem_kv[l]))
        n_mem = mem_prompt.shape[1]
        row_tile, q_tile = _tiles(tp)
        xp, st_p = _layer(xp, pos_p, lower_bounds[l], lp, cache=None,
                          state0=jnp.zeros((bp, B_HEADS, B_KEY_DIM, B_VAL_DIM), F32), mem16=(mk16, mv16),
                          row_tile=row_tile, q_tile=q_tile, key_tile=512)
        outs_p.append(st_p + (mk32.reshape(bp, n_mem, C_HEADS, C_HEAD_DIM), mv32.reshape(bp, n_mem, C_HEADS, C_HEAD_DIM)))

        cik = cache_idx_k[l]
        zeros_ik = jnp.zeros_like(cik)
        cache = (bf(jnp.concatenate([cik, zeros_ik], axis=-1)), bf(jnp.concatenate([zeros_ik, cik], axis=-1)),
                 bf(cache_k[l].reshape(bs, past, KV_WIDTH)), bf(cache_v[l].reshape(bs, past, KV_WIDTH)))
        mem16 = (bf(cache_mem_k[l].reshape(bs, -1, C_WIDTH)), bf(cache_mem_v[l].reshape(bs, -1, C_WIDTH)))
        row_tile, q_tile = _tiles(ts)
        xs, st_s = _layer(xs, pos_s, lower_bounds[l], lp, cache=cache, state0=state_hgrn[l].astype(F32),
                          mem16=mem16, row_tile=row_tile, q_tile=q_tile, key_tile=384)
        outs_s.append(st_s)

    new_k_p, new_v_p, new_ik_p, new_s_p, new_mk_p, new_mv_p = [jnp.stack(a) for a in zip(*outs_p)]
    new_k_s, new_v_s, new_ik_s, new_s_s = [jnp.stack(a) for a in zip(*outs_s)]
    return (xp, xs, new_k_p, new_v_p, new_ik_p, new_s_p, new_mk_p, new_mv_p, new_k_s, new_v_s, new_ik_s, new_s_s)
```

```python
import functools

import jax
import jax.numpy as jnp
import numpy as np
from jax import lax
from jax.experimental import pallas as pl
from jax.experimental.pallas import tpu as pltpu

CHUNK = 64
N_BRANCHES = 3
A_HEADS, A_KV_HEADS, HEAD_DIM = 6, 2, 128
ROT_DIM = HEAD_DIM // 4
IDX_HEADS, IDX_DIM = 8, 64
IDX_ROT_DIM = IDX_DIM // 4
TOPK_MAX = 256
B_HEADS, B_KEY_DIM, B_VAL_DIM = 6, 128, 128
C_HEADS, C_HEAD_DIM = 4, 128
ROPE_THETA = 500000.0
EPS = 1e-6

A_WIDTH = A_HEADS * HEAD_DIM
KV_WIDTH = A_KV_HEADS * HEAD_DIM
IQ_WIDTH = IDX_HEADS * IDX_DIM
B_WIDTH = B_HEADS * B_KEY_DIM
C_WIDTH = C_HEADS * C_HEAD_DIM

LANES = 128
SUBLANES = 8
VMEM_LIMIT_BYTES = 56 * 1024 * 1024

MXU_DTYPE = jnp.bfloat16
F32 = jnp.float32
I32 = jnp.int32

IW_LANE = 96
NEG_BIG = -1e30
LOG2E = 1.4426950408889634
ATT_ROWS = 128
INT_MIN = -(2 ** 31)
F32_LOWEST = -3.4028234663852886e38

OFF_AQ = 0
OFF_AK = OFF_AQ + A_WIDTH
OFF_AV = OFF_AK + KV_WIDTH
OFF_IQ = OFF_AV + KV_WIDTH
OFF_IKW = OFF_IQ + IQ_WIDTH
OFF_BQ = OFF_IKW + LANES
OFF_BF = OFF_BQ + B_WIDTH
OFF_BI = OFF_BF + B_WIDTH
OFF_BG = OFF_BI + B_WIDTH
OFF_CQ = OFF_BG + B_WIDTH
OFF_GATES = OFF_CQ + C_WIDTH


def _sigmoid(x):
    return 1.0 / (1.0 + jnp.exp(-x))


def _dot(a, b):
    return jnp.dot(a, b, preferred_element_type=F32)


def _dot_nt(a, b):
    return lax.dot_general(a, b, (((1,), (1,)), ((), ())), preferred_element_type=F32)


def _rms(x, gain):
    return x * lax.rsqrt(jnp.mean(x * x, axis=-1, keepdims=True) + EPS) * gain


def _resident(shape):
    zeros = (0,) * len(shape)
    return pl.BlockSpec(shape, lambda *_: zeros, pipeline_mode=pl.Buffered(1))


def _rope(y, cos, sin_up, sin_down, half):
    return y * cos + pltpu.roll(y, half, 1) * sin_up + pltpu.roll(y, LANES - half, 1) * sin_down


def _in_proj_kernel(x_ref, gain_ref, lb_ref, w_ref, tri_ref, ca_ref, sau_ref, sad_ref, ci_ref, siu_ref, sid_ref,
                    aq_ref, k32_ref, v32_ref, k16_ref, v16_ref, iq_ref, ikw_ref, ika_ref, ikb_ref,
                    hq_ref, hk_ref, hlf_ref, hv_ref, hg_ref, cq_ref, gt_ref):
    hb = _rms(x_ref[...], gain_ref[...]).astype(MXU_DTYPE)

    def proj(c0, width):
        return _dot(hb, w_ref[:, c0:c0 + width])

    ca, sau, sad = ca_ref[...], sau_ref[...], sad_ref[...]
    ci, siu, sid = ci_ref[...], siu_ref[...], sid_ref[...]

    y = proj(OFF_AQ, A_WIDTH)
    for h in range(A_HEADS):
        sl = slice(h * LANES, (h + 1) * LANES)
        aq_ref[:, sl] = (_rope(y[:, sl], ca, sau, sad, ROT_DIM // 2) * ((HEAD_DIM ** -0.5) * LOG2E)).astype(MXU_DTYPE)

    y = proj(OFF_AK, KV_WIDTH)
    for h in range(A_KV_HEADS):
        sl = slice(h * LANES, (h + 1) * LANES)
        r = _rope(y[:, sl], ca, sau, sad, ROT_DIM // 2)
        k32_ref[:, sl] = r
        k16_ref[:, sl] = r.astype(MXU_DTYPE)

    y = proj(OFF_AV, KV_WIDTH)
    v32_ref[...] = y
    v16_ref[...] = y.astype(MXU_DTYPE)

    y = proj(OFF_IQ, IQ_WIDTH)
    for j in range(IQ_WIDTH // LANES):
        sl = slice(j * LANES, (j + 1) * LANES)
        iq_ref[:, sl] = _rope(y[:, sl], ci, siu, sid, IDX_ROT_DIM // 2).astype(MXU_DTYPE)

    r = _rope(proj(OFF_IKW, LANES), ci, siu, sid, IDX_ROT_DIM // 2)
    lane = lax.broadcasted_iota(I32, r.shape, 1)
    ikw_ref[...] = jnp.where(lane < IDX_DIM, r, r * ((IDX_DIM ** -0.5) * (IDX_HEADS ** -0.5)))
    ik_low = jnp.where(lane < IDX_DIM, r, 0.0)
    ika_ref[...] = ik_low.astype(MXU_DTYPE)
    ikb_ref[...] = pltpu.roll(ik_low, IDX_DIM, 1).astype(MXU_DTYPE)

    y = proj(OFF_BQ, B_WIDTH)
    hq_ref[...] = (y * _sigmoid(y)).astype(MXU_DTYPE)
    lb = lb_ref[...]
    f = lb + (1.0 - lb) * _sigmoid(proj(OFF_BF, B_WIDTH))
    hk_ref[...] = (1.0 - f).astype(MXU_DTYPE)
    hi, mid, lo = _split3(jnp.log(f))
    tri = tri_ref[...]
    hlf_ref[...] = _dot(tri, hi) + _dot(tri, mid) + _dot(tri, lo)
    hv_ref[...] = proj(OFF_BI, B_WIDTH).astype(MXU_DTYPE)
    y = proj(OFF_BG, B_WIDTH)
    hg_ref[...] = (y * _sigmoid(y)).astype(MXU_DTYPE)

    cq_ref[...] = (proj(OFF_CQ, C_WIDTH) * (C_HEAD_DIM ** -0.5)).astype(MXU_DTYPE)

    n_gate = gt_ref.shape[1]
    for c0 in range(0, n_gate, B_WIDTH):
        gt_ref[:, c0:c0 + B_WIDTH] = _sigmoid(proj(OFF_GATES + c0, B_WIDTH)).astype(MXU_DTYPE)


def _rope_tables(pos, rot_dim, head_dim):
    half = rot_dim // 2
    inv_freq = ROPE_THETA ** (-jnp.arange(half, dtype=F32) / half)
    ang = pos.astype(F32)[:, None] * inv_freq[None, :]
    cos, sin = jnp.cos(ang), jnp.sin(ang)
    lane = np.arange(LANES) % head_dim
    idx = lane % half
    first, second = lane < half, (lane >= half) & (lane < rot_dim)
    cos_t = jnp.where(first | second, cos[:, idx], 1.0)
    sin_up = jnp.where(second, sin[:, idx], 0.0)
    sin_down = jnp.where(first, -sin[:, idx], 0.0)
    return cos_t, sin_up, sin_down


def _in_proj(x, pos, gain, lower_bound, w_packed, row_tile):
    bsz, t, d = x.shape
    n = bsz * t
    tm = row_tile
    assert n % tm == 0 and (t % tm == 0 or tm % t == 0)
    tables = _rope_tables(pos, ROT_DIM, HEAD_DIM) + _rope_tables(pos, IDX_ROT_DIM, IDX_DIM)
    table_rows = max(t, tm)
    tables = [jnp.tile(tb, (table_rows // t, 1)) for tb in tables]
    n_table_blocks = table_rows // tm
    n_gate = N_BRANCHES * d
    w_width = w_packed.shape[1]
    assert tm % CHUNK == 0 and t % CHUNK == 0
    r = np.arange(tm)
    chunk_tri = jnp.asarray((r[:, None] // CHUNK == r[None, :] // CHUNK) & (r[None, :] <= r[:, None]), MXU_DTYPE)

    def rows(width):
        return pl.BlockSpec((tm, width), lambda i: (i, 0))

    table_spec = pl.BlockSpec((tm, LANES), lambda i: (i % n_table_blocks, 0))
    widths_dtypes = [
        (A_WIDTH, MXU_DTYPE), (KV_WIDTH, F32), (KV_WIDTH, F32), (KV_WIDTH, MXU_DTYPE), (KV_WIDTH, MXU_DTYPE),
        (IQ_WIDTH, MXU_DTYPE), (LANES, F32), (LANES, MXU_DTYPE), (LANES, MXU_DTYPE),
        (B_WIDTH, MXU_DTYPE), (B_WIDTH, MXU_DTYPE), (B_WIDTH, F32), (B_WIDTH, MXU_DTYPE), (B_WIDTH, MXU_DTYPE),
        (C_WIDTH, MXU_DTYPE), (n_gate, MXU_DTYPE)]
    outs = pl.pallas_call(
        _in_proj_kernel,
        grid=(n // tm,),
        in_specs=[rows(d), _resident((1, d)), _resident((1, B_WIDTH)), _resident((d, w_width)), _resident((tm, tm))]
        + [table_spec] * 6,
        out_specs=[rows(w) for w, _ in widths_dtypes],
        out_shape=[jax.ShapeDtypeStruct((n, w), dt) for w, dt in widths_dtypes],
        compiler_params=pltpu.CompilerParams(dimension_semantics=("arbitrary",), vmem_limit_bytes=VMEM_LIMIT_BYTES),
        name="in_proj",
    )(x.reshape(n, d), gain.reshape(1, d), lower_bound.reshape(1, B_WIDTH), w_packed, chunk_tri, *tables)
    return [o.reshape(bsz, t, o.shape[-1]) for o in outs]


def _pack_w_in(w, d):
    widths = (A_WIDTH, KV_WIDTH, KV_WIDTH, IQ_WIDTH, IDX_DIM, IDX_HEADS, B_WIDTH, B_WIDTH, B_WIDTH, B_WIDTH,
              C_WIDTH, N_BRANCHES * d)
    points = [int(s) for s in np.cumsum(widths)[:-1]]
    a_q, a_k, a_v, i_q, i_k, i_w, b_q, b_f, b_i, b_g, c_q, gates = jnp.split(w, points, axis=-1)
    ikw = jnp.concatenate([i_k, jnp.zeros((d, IW_LANE - IDX_DIM), w.dtype), i_w,
                           jnp.zeros((d, LANES - IW_LANE - IDX_HEADS), w.dtype)], axis=-1)
    return jnp.concatenate([a_q, a_k, a_v, i_q, ikw, b_q, b_f, b_i, b_g, c_q, gates], axis=-1).astype(MXU_DTYPE)


def _float_of_key(key):
    return pltpu.bitcast(jnp.where(key < 0, (key - 1) ^ 0x7FFFFFFF, key), F32)


def _lane_tiled(x, n_tiles):
    return x if n_tiles == 1 else jnp.concatenate([x] * n_tiles, axis=1)


COUNT_GROUPS = 8
COUNT_GROUP = 8
COUNT_BY_MATMUL = 6
TIE_ROWS = 128


def _dsa_kernel(aq_ref, iq_ref, iw_ref, ika_ref, ikb_ref, k_ref, v_ref, ya_ref,
                key_sc, bias_sc, m_sc, l_sc, acc_sc, *, tq, tk, key_off, topk, q_sub):
    qi = pl.program_id(1)
    n_vis = key_off + (qi + 1) * tq
    n_kb = lax.div(n_vis + (tk - 1), tk)
    q_pos = qi * tq + lax.broadcasted_iota(I32, (1, tq), 1)
    limit = key_off + (lax.shift_right_logical(q_pos, CHUNK.bit_length() - 1) + 1) * CHUNK

    def for_each_block(body):
        def pair(j, carry):
            body(2 * j, 0)
            body(2 * j + 1, 1)
            return carry

        lax.fori_loop(0, lax.shift_right_logical(n_kb, 1), pair, 0)

        @pl.when((n_kb & 1) == 1)
        def _():
            body(n_kb - 1, 0)

    def score_block(kb, slot):
        k0 = pl.multiple_of(kb * tk, tk)
        ika = ika_ref[0, pl.ds(k0, tk), :]
        ikb = ikb_ref[0, pl.ds(k0, tk), :]
        acc = jnp.zeros((tk, tq), F32)
        for j in range(IQ_WIDTH // LANES):
            iqt = iq_ref[0, :, j * LANES:(j + 1) * LANES]
            acc = acc + jnp.maximum(_dot_nt(ika, iqt), 0.0) * iw_ref[0, 2 * j:2 * j + 1, :]
            acc = acc + jnp.maximum(_dot_nt(ikb, iqt), 0.0) * iw_ref[0, 2 * j + 1:2 * j + 2, :]
        key_pos = k0 + lax.broadcasted_iota(I32, (tk, tq), 0)
        key_sc[kb] = jnp.where(key_pos < limit, acc, -jnp.inf)

    for_each_block(score_block)

    ones_rows = jnp.ones((2 * SUBLANES, tk), MXU_DTYPE)

    def count(pred):
        step = SUBLANES * COUNT_GROUPS

        def hits(kb):
            return _dot(ones_rows, jnp.where(pred(key_sc[kb]), 1.0, 0.0).astype(MXU_DTYPE))

        def one_block(kb, acc):
            for r0 in range(0, tk, step):
                acc = acc + jnp.where(pred(key_sc[kb, r0:r0 + step, :]), 1, 0)
            return acc

        def group(j, carry):
            by_matmul, by_adds = carry
            kb = COUNT_GROUP * j
            parts = [hits(kb + i) for i in range(COUNT_BY_MATMUL)]
            while len(parts) > 1:
                parts = [a + b for a, b in zip(parts[0::2], parts[1::2])] + parts[len(parts) & ~1:]
            for i in range(COUNT_BY_MATMUL, COUNT_GROUP):
                by_adds = one_block(kb + i, by_adds)
            return by_matmul + parts[0], by_adds

        n_groups = lax.div(n_kb, COUNT_GROUP)
        zeros = (jnp.zeros((2 * SUBLANES, tq), F32), jnp.zeros((step, tq), I32))
        by_matmul, by_adds = lax.fori_loop(0, n_groups, group, zeros)
        by_adds = lax.fori_loop(COUNT_GROUP * n_groups, n_kb, one_block, by_adds)
        return by_matmul[0:1, :].astype(I32) + jnp.sum(by_adds, axis=0, keepdims=True)

    def key_bit(it, u):
        trial = u | jnp.left_shift(jnp.int32(1), 31 - it)
        cand = _float_of_key(trial ^ INT_MIN)
        return jnp.where(count(lambda s: s >= cand) >= topk, trial, u)

    u = lax.fori_loop(0, 32, key_bit, jnp.zeros((1, tq), I32))
    thr = _float_of_key(u ^ INT_MIN)
    n_gt = count(lambda s: s > thr)
    n_ge = count(lambda s: s >= thr)
    has_tie = jnp.where((n_ge > topk) & (thr > -jnp.inf), 1.0, 0.0)

    @pl.when(jnp.max(has_tie) > 0.0)
    def _():
        need = (topk - n_gt).astype(F32)
        ri = lax.broadcasted_iota(I32, (TIE_ROWS, TIE_ROWS), 0)
        ci = lax.broadcasted_iota(I32, (TIE_ROWS, TIE_ROWS), 1)
        lower = jnp.where(ci <= ri, 1.0, 0.0).astype(MXU_DTYPE)

        def demote(kb, seen):
            pieces = []
            for r0 in range(0, tk, TIE_ROWS):
                blk = key_sc[kb, r0:r0 + TIE_ROWS, :]
                tie = blk == thr
                pieces.append((r0, blk, tie, _dot(lower, jnp.where(tie, 1.0, 0.0).astype(MXU_DTYPE))))
            for r0, blk, tie, prefix in pieces:
                rank = prefix + seen
                key_sc[kb, r0:r0 + TIE_ROWS, :] = jnp.where(tie & (rank > need), -jnp.inf, blk)
                seen = rank[TIE_ROWS - 1:TIE_ROWS, :]
            return seen

        lax.fori_loop(0, n_kb, demote, jnp.zeros((1, tq), F32))

    thr_sel = jnp.maximum(thr, F32_LOWEST)

    m_sc[...] = jnp.full(m_sc.shape, NEG_BIG, F32)
    l_sc[...] = jnp.zeros(l_sc.shape, F32)
    acc_sc[...] = jnp.zeros(acc_sc.shape, F32)
    group = A_HEADS // A_KV_HEADS
    n_lane_tiles = tk // LANES
    ones_blk = jnp.ones((tk, LANES), MXU_DTYPE)

    def attend_block(kb, slot):
        k0 = pl.multiple_of(kb * tk, tk)
        bias_sc[slot] = jnp.where(key_sc[kb] >= thr_sel, 0.0, NEG_BIG).T
        for g in range(A_KV_HEADS):
            kblk = k_ref[0, pl.ds(k0, tk), g * LANES:(g + 1) * LANES]
            vaug = jnp.concatenate([v_ref[0, pl.ds(k0, tk), g * LANES:(g + 1) * LANES], ones_blk], axis=1)
            for r in range(group):
                h = g * group + r
                for rt in range(tq // q_sub):
                    rows = pl.ds(rt * q_sub, q_sub)
                    q = aq_ref[0, rows, h * LANES:(h + 1) * LANES]
                    s = _dot_nt(q, kblk) + bias_sc[slot, rows, :]
                    m_prev = m_sc[h, rows, :]
                    m_new = jnp.maximum(m_prev, jnp.max(s, axis=1, keepdims=True))
                    alpha = jnp.exp2(m_prev - m_new)
                    p = jnp.exp2(s - _lane_tiled(m_new, n_lane_tiles))
                    pv = _dot(p.astype(MXU_DTYPE), vaug)
                    acc_sc[h, rows, :] = alpha * acc_sc[h, rows, :] + pv[:, :LANES]
                    l_sc[h, rows, :] = alpha * l_sc[h, rows, :] + pv[:, LANES:]
                    m_sc[h, rows, :] = m_new
    for_each_block(attend_block)
    for h in range(A_HEADS):
        ya_ref[0, :, h * LANES:(h + 1) * LANES] = (acc_sc[h] / l_sc[h]).astype(ya_ref.dtype)


def _dsa(aq, iq, iw_t, ika, ikb, k16, v16, *, key_off, topk, q_tile, key_tile):
    bsz, t, _ = aq.shape
    lp = k16.shape[1]
    tq, tk = q_tile, key_tile
    assert lp % tk == 0
    n_kb_all = lp // tk
    assert t % tq == 0 and tq % LANES == 0 and tk % (SUBLANES * COUNT_GROUPS) == 0 and key_off + t <= lp
    assert tk >= topk
    kernel = functools.partial(_dsa_kernel, tq=tq, tk=tk, key_off=key_off, topk=topk, q_sub=min(tq, ATT_ROWS))

    def q_spec(width):
        return pl.BlockSpec((1, tq, width), lambda b, i: (b, i, 0))

    def key_spec(width):
        return pl.BlockSpec((1, lp, width), lambda b, i: (b, 0, 0), pipeline_mode=pl.Buffered(1))

    return pl.pallas_call(
        kernel,
        grid=(bsz, t // tq),
        in_specs=[q_spec(A_WIDTH), q_spec(IQ_WIDTH), pl.BlockSpec((1, IDX_HEADS, tq), lambda b, i: (b, 0, i)),
                  key_spec(LANES), key_spec(LANES), key_spec(KV_WIDTH), key_spec(KV_WIDTH)],
        out_specs=q_spec(A_WIDTH),
        out_shape=jax.ShapeDtypeStruct((bsz, t, A_WIDTH), MXU_DTYPE),
        scratch_shapes=[pltpu.VMEM((n_kb_all, tk, tq), F32), pltpu.VMEM((2, tq, tk), F32),
                        pltpu.VMEM((A_HEADS, tq, LANES), F32), pltpu.VMEM((A_HEADS, tq, LANES), F32),
                        pltpu.VMEM((A_HEADS, tq, LANES), F32)],
        compiler_params=pltpu.CompilerParams(dimension_semantics=("arbitrary", "arbitrary"),
                                             vmem_limit_bytes=VMEM_LIMIT_BYTES),
        name="dsa",
    )(aq, iq, iw_t, ika, ikb, k16, v16)


SUB = SUBLANES
HGRN_ROWS = 4
HGRN_GROUP_HEADS = 2
N_SUB = CHUNK // SUB
INTER_ROWS = SUB * (N_SUB * (N_SUB - 1) // 2)


def _split3(x):
    hi = x.astype(MXU_DTYPE)
    r1 = x - hi.astype(F32)
    mid = r1.astype(MXU_DTYPE)
    lo = (r1 - mid.astype(F32)).astype(MXU_DTYPE)
    return hi, mid, lo


def _hgrn_constants():
    t = np.arange(CHUNK)
    place = ((np.arange(SUB * LANES)[:, None] // LANES) == (t[None, :] % SUB)).astype(np.float32)
    intra = ((t[:, None] // SUB == t[None, :] // SUB) & (t[None, :] <= t[:, None])).astype(np.float32)
    seg = np.concatenate([np.full(i * SUB, i) for i in range(1, N_SUB)])
    inter = ((t[:, None] // SUB) == seg[None, :]).astype(np.float32)
    return jnp.asarray(place, MXU_DTYPE), jnp.asarray(intra, F32), jnp.asarray(inter, F32)


def _hgrn_kernel(q_ref, k_ref, b_ref, v_ref, g_ref, gain_ref, s0_ref, place_ref, intra_ref, inter_ref,
                 yb_ref, sout_ref, st_sc):
    c = pl.program_id(1)
    n_rows = q_ref.shape[0]

    @pl.when(c == 0)
    def _():
        for i in range(n_rows):
            for h in range(B_HEADS):
                st_sc[i, h] = s0_ref[i, h].T

    def row_bcast(x, r, n):
        return jnp.broadcast_to(x[r:r + 1, :], (n, x.shape[1]))

    place = place_ref[...]

    def head_group(i, g):
        width = HGRN_GROUP_HEADS * LANES
        lanes = slice(g * width, (g + 1) * width)
        q = q_ref[i, :, lanes].astype(F32)
        k = k_ref[i, :, lanes].astype(F32)
        v16 = v_ref[i, :, lanes]
        b = b_ref[i, :, lanes]

        b_start = jnp.concatenate(
            [jnp.zeros((SUB, width), F32)] + [row_bcast(b, j * SUB - 1, SUB) for j in range(1, N_SUB)], axis=0)
        q_hat = (q * jnp.exp(b - b_start)).astype(MXU_DTYPE)
        q_state = (q * jnp.exp(b)).astype(MXU_DTYPE)
        b_end = b[CHUNK - 1:CHUNK, :]
        k_end = (k * jnp.exp(b_end - b)).astype(MXU_DTYPE)

        k_hat = jnp.concatenate(
            [k[:j * SUB] * jnp.exp(row_bcast(b, j * SUB - 1, j * SUB) - b[:j * SUB]) for j in range(1, N_SUB)],
            axis=0).astype(MXU_DTYPE)
        v_hat = jnp.concatenate([v16[:j * SUB] for j in range(1, N_SUB)], axis=0)

        q3 = q.reshape(N_SUB, SUB, width)
        k3 = k.reshape(N_SUB, SUB, width)
        b3 = b.reshape(N_SUB, SUB, width)
        intra = []
        for s in range(SUB):
            kp = jnp.broadcast_to(k3[:, s:s + 1, :], q3.shape)
            bp = jnp.broadcast_to(b3[:, s:s + 1, :], q3.shape)
            x = q3 * kp * jnp.exp(jnp.minimum(b3 - bp, 0.0))
            intra.append(x.reshape(CHUNK, width).astype(MXU_DTYPE))

        for hh in range(HGRN_GROUP_HEADS):
            h = g * HGRN_GROUP_HEADS + hh
            sl = slice(hh * LANES, (hh + 1) * LANES)
            out_lanes = slice(h * LANES, (h + 1) * LANES)
            st = st_sc[i, h]
            x_cat = jnp.concatenate([x[:, sl] for x in intra], axis=1)
            p_intra = (_dot(x_cat, place) * intra_ref[...]).astype(MXU_DTYPE)
            p_inter = (_dot_nt(q_hat[:, sl], k_hat[:, sl]) * inter_ref[...]).astype(MXU_DTYPE)
            o = (_dot(p_inter, v_hat[:, sl]) + _dot(p_intra, v16[:, sl])
                 + _dot_nt(q_state[:, sl], st.astype(MXU_DTYPE)))
            st_sc[i, h] = st * jnp.exp(b_end[:, sl]) + _dot(v16[:, sl].astype(F32).T.astype(MXU_DTYPE), k_end[:, sl])
            y = _rms(o, gain_ref[:, out_lanes]) * g_ref[i, :, out_lanes].astype(F32)
            yb_ref[i, :, out_lanes] = y.astype(yb_ref.dtype)

    for i in range(n_rows):
        for g in range(B_HEADS // HGRN_GROUP_HEADS):
            head_group(i, g)

    @pl.when(c == pl.num_programs(1) - 1)
    def _():
        for i in range(n_rows):
            for h in range(B_HEADS):
                sout_ref[i, h] = st_sc[i, h].T


def _hgrn(hq, hk, hlf, hv, hg, norm_gain, state0):
    bsz, t, _ = hq.shape
    assert t % CHUNK == 0 and bsz % HGRN_ROWS == 0
    nb = HGRN_ROWS
    consts = _hgrn_constants()
    seq_spec = pl.BlockSpec((nb, CHUNK, B_WIDTH), lambda b, c: (b, c, 0))
    state_spec = pl.BlockSpec((nb, B_HEADS, B_KEY_DIM, B_VAL_DIM), lambda b, c: (b, 0, 0, 0))
    return pl.pallas_call(
        _hgrn_kernel,
        grid=(bsz // nb, t // CHUNK),
        in_specs=[seq_spec] * 5 + [_resident((1, B_WIDTH)), state_spec] + [_resident(x.shape) for x in consts],
        out_specs=[seq_spec, state_spec],
        out_shape=[jax.ShapeDtypeStruct((bsz, t, B_WIDTH), MXU_DTYPE),
                   jax.ShapeDtypeStruct((bsz, B_HEADS, B_KEY_DIM, B_VAL_DIM), F32)],
        scratch_shapes=[pltpu.VMEM((nb, B_HEADS, B_VAL_DIM, B_KEY_DIM), F32)],
        compiler_params=pltpu.CompilerParams(dimension_semantics=("arbitrary", "arbitrary"),
                                             vmem_limit_bytes=VMEM_LIMIT_BYTES),
        name="hgrn",
    )(hq, hk, hlf, hv, hg, norm_gain.reshape(1, B_WIDTH), state0, *consts)


def _memory_kv_kernel(mem_ref, gain_ref, w_ref, k32_ref, v32_ref, k16_ref, v16_ref):
    hb = _rms(mem_ref[0], gain_ref[...]).astype(MXU_DTYPE)
    mk = _dot(hb, w_ref[:, :C_WIDTH])
    mv = _dot(hb, w_ref[:, C_WIDTH:])
    k32_ref[0] = mk
    v32_ref[0] = mv
    k16_ref[0] = mk.astype(MXU_DTYPE)
    v16_ref[0] = mv.astype(MXU_DTYPE)


def _memory_kv(mem, gain, w16):
    bsz, n, d = mem.shape
    spec = pl.BlockSpec((1, n, C_WIDTH), lambda b: (b, 0, 0))
    return pl.pallas_call(
        _memory_kv_kernel,
        grid=(bsz,),
        in_specs=[pl.BlockSpec((1, n, d), lambda b: (b, 0, 0)), _resident((1, d)), _resident((d, 2 * C_WIDTH))],
        out_specs=[spec] * 4,
        out_shape=[jax.ShapeDtypeStruct((bsz, n, C_WIDTH), dt) for dt in (F32, F32, MXU_DTYPE, MXU_DTYPE)],
        compiler_params=pltpu.CompilerParams(dimension_semantics=("arbitrary",), vmem_limit_bytes=VMEM_LIMIT_BYTES),
        name="memory_kv",
    )(mem, gain.reshape(1, d), w16)


def _merge_ffn_kernel(x_ref, ya_ref, yb_ref, cq_ref, mk_ref, mv_ref, gt_ref, wa_ref, wb_ref, wc_ref, wo_ref,
                      g_mix_ref, g_pre_ref, g_post_ref, wu_ref, wd_ref, out_ref):
    d = x_ref.shape[-1]
    heads = []
    for h in range(C_HEADS):
        sl = slice(h * C_HEAD_DIM, (h + 1) * C_HEAD_DIM)
        s = _dot_nt(cq_ref[0, :, sl], mk_ref[0, :, sl])
        p = jnp.exp(s - jnp.max(s, axis=-1, keepdims=True))
        p = p / jnp.sum(p, axis=-1, keepdims=True)
        heads.append(_dot(p.astype(MXU_DTYPE), mv_ref[0, :, sl]).astype(MXU_DTYPE))
    y_c = jnp.concatenate(heads, axis=1)

    merged = (gt_ref[0, :, 0:d].astype(F32) * _dot(ya_ref[0], wa_ref[...])
              + gt_ref[0, :, d:2 * d].astype(F32) * _dot(yb_ref[0], wb_ref[...])
              + gt_ref[0, :, 2 * d:3 * d].astype(F32) * _dot(y_c, wc_ref[...]))
    x1 = x_ref[0] + _rms(_dot(merged.astype(MXU_DTYPE), wo_ref[...]), g_mix_ref[...])
    u = jnp.maximum(_dot(_rms(x1, g_pre_ref[...]).astype(MXU_DTYPE), wu_ref[...]), 0.0)
    down = _dot((u * u).astype(MXU_DTYPE), wd_ref[...])
    out_ref[0] = x1 + _rms(down, g_post_ref[...])


def _merge_ffn(x, ya, yb, cq, mk16, mv16, gates, weights, gains, row_tile):
    bsz, t, d = x.shape
    tm = row_tile
    assert t % tm == 0
    n_mem = mk16.shape[1]

    def rows(width):
        return pl.BlockSpec((1, tm, width), lambda b, i: (b, i, 0))

    mem_spec = pl.BlockSpec((1, n_mem, C_WIDTH), lambda b, i: (b, 0, 0))
    return pl.pallas_call(
        _merge_ffn_kernel,
        grid=(bsz, t // tm),
        in_specs=[rows(d), rows(A_WIDTH), rows(B_WIDTH), rows(C_WIDTH), mem_spec, mem_spec, rows(N_BRANCHES * d)]
        + [_resident(w.shape) for w in weights[:4]] + [_resident((1, d))] * 3
        + [_resident(w.shape) for w in weights[4:]],
        out_specs=rows(d),
        out_shape=jax.ShapeDtypeStruct((bsz, t, d), F32),
        compiler_params=pltpu.CompilerParams(dimension_semantics=("arbitrary", "arbitrary"),
                                             vmem_limit_bytes=VMEM_LIMIT_BYTES),
        name="merge_ffn",
    )(x, ya, yb, cq, mk16, mv16, gates, *weights[:4], *[g.reshape(1, d) for g in gains], *weights[4:])


def _layer(x, pos, lower_bound, lp, *, cache, state0, mem16, row_tile, q_tile, key_tile):
    bsz, t, d = x.shape
    (aq, k32, v32, k16, v16, iq, ikw, ika, ikb, hq, hk, hlf, hv, hg, cq, gates) = _in_proj(
        x, pos, lp["pre_mix_gain"], lower_bound, lp["w_in"], min(row_tile, bsz * t))

    keys = [ika, ikb, k16, v16]
    if cache is not None:
        keys = [jnp.concatenate([c, kn], axis=1) for c, kn in zip(cache, keys)]
    n_keys = keys[0].shape[1]
    t_pad = -(-t // q_tile) * q_tile
    key_off = n_keys - t
    n_pad = -(-(key_off + t_pad) // key_tile) * key_tile
    ika, ikb, k16, v16 = [jnp.pad(kk, ((0, 0), (0, n_pad - n_keys), (0, 0))) for kk in keys]
    iw_t = jnp.swapaxes(ikw[..., IW_LANE:IW_LANE + IDX_HEADS], 1, 2)
    q_pad = ((0, 0), (0, t_pad - t), (0, 0))
    ya = _dsa(jnp.pad(aq, q_pad), jnp.pad(iq, q_pad), jnp.pad(iw_t, ((0, 0), (0, 0), (0, t_pad - t))),
              ika, ikb, k16, v16, key_off=key_off, topk=min(TOPK_MAX, n_keys // 4), q_tile=q_tile,
              key_tile=key_tile)[:, :t]

    yb, s_new = _hgrn(hq, hk, hlf, hv, hg, lp["hgrn_norm_gain"], state0)

    out = _merge_ffn(x, ya, yb, cq, mem16[0], mem16[1], gates,
                     [lp["w_out_a"], lp["w_out_b"], lp["w_out_c"], lp["w_out"], lp["w_up"], lp["w_down"]],
                     [lp["post_mix_gain"], lp["pre_ffn_gain"], lp["post_ffn_gain"]], row_tile=min(MERGE_ROWS, t))
    new_k = k32.reshape(bsz, t, A_KV_HEADS, HEAD_DIM)
    new_v = v32.reshape(bsz, t, A_KV_HEADS, HEAD_DIM)
    return out, (new_k, new_v, ikw[..., :IDX_DIM], s_new)


MERGE_ROWS = 512
DSA_QUERIES = 512


def _tiles(t):
    return 256, max(LANES, min(DSA_QUERIES, t))


def kernel(x_prompt, x_sample, cache_k, cache_v, cache_idx_k, state_hgrn, cache_mem_k, cache_mem_v, mem_prompt, w_in, w_mem_kv, mem_norm_gain, hgrn_lb_logits, hgrn_norm_gain, w_out_a, w_out_b, w_out_c, w_out, pre_mix_gain, post_mix_gain, pre_ffn_gain, post_ffn_gain, w_up, w_down):
    depth = w_in.shape[0]
    d = x_prompt.shape[-1]
    bp, tp, _ = x_prompt.shape
    bs, ts, _ = x_sample.shape
    past = cache_k.shape[2]
    lower_bounds = jnp.cumsum(jax.nn.softmax(hgrn_lb_logits.astype(F32), axis=0), axis=0)
    pos_p = jnp.arange(tp, dtype=I32)
    pos_s = past + jnp.arange(ts, dtype=I32)
    bf = lambda a: a.astype(MXU_DTYPE)

    xp, xs = x_prompt, x_sample
    outs_p, outs_s = [], []
    for l in range(depth):
        lp = {
            "w_in": _pack_w_in(w_in[l], d), "hgrn_norm_gain": hgrn_norm_gain[l],
            "w_out_a": bf(w_out_a[l]), "w_out_b": bf(w_out_b[l]), "w_out_c": bf(w_out_c[l]), "w_out": bf(w_out[l]),
            "pre_mix_gain": pre_mix_gain[l], "post_mix_gain": post_mix_gain[l], "pre_ffn_gain": pre_ffn_gain[l],
            "post_ffn_gain": post_ffn_gain[l], "w_up": bf(w_up[l]), "w_down": bf(w_down[l]),
        }
        mk32, mv32, mk16, mv16 = _memory_kv(mem_prompt, mem_norm_gain[l], bf(w_mem_kv[l]))
        n_mem = mem_prompt.shape[1]
        row_tile, q_tile = _tiles(tp)
        xp, st_p = _layer(xp, pos_p, lower_bounds[l], lp, cache=None,
                          state0=jnp.zeros((bp, B_HEADS, B_KEY_DIM, B_VAL_DIM), F32), mem16=(mk16, mv16),
                          row_tile=row_tile, q_tile=q_tile, key_tile=512)
        outs_p.append(st_p + (mk32.reshape(bp, n_mem, C_HEADS, C_HEAD_DIM), mv32.reshape(bp, n_mem, C_HEADS, C_HEAD_DIM)))

        cik = cache_idx_k[l]
        zeros_ik = jnp.zeros_like(cik)
        cache = (bf(jnp.concatenate([cik, zeros_ik], axis=-1)), bf(jnp.concatenate([zeros_ik, cik], axis=-1)),
                 bf(cache_k[l].reshape(bs, past, KV_WIDTH)), bf(cache_v[l].reshape(bs, past, KV_WIDTH)))
        mem16 = (bf(cache_mem_k[l].reshape(bs, -1, C_WIDTH)), bf(cache_mem_v[l].reshape(bs, -1, C_WIDTH)))
        row_tile, q_tile = _tiles(ts)
        xs, st_s = _layer(xs, pos_s, lower_bounds[l], lp, cache=cache, state0=state_hgrn[l].astype(F32),
                          mem16=mem16, row_tile=row_tile, q_tile=q_tile, key_tile=384)
        outs_s.append(st_s)

    new_k_p, new_v_p, new_ik_p, new_s_p, new_mk_p, new_mv_p = [jnp.stack(a) for a in zip(*outs_p)]
    new_k_s, new_v_s, new_ik_s, new_s_s = [jnp.stack(a) for a in zip(*outs_s)]
    return (xp, xs, new_k_p, new_v_p, new_ik_p, new_s_p, new_mk_p, new_mv_p, new_k_s, new_v_s, new_ik_s, new_s_s)
```

```python
import functools

import jax
import jax.numpy as jnp
import numpy as np
from jax import lax
from jax.experimental import pallas as pl
from jax.experimental.pallas import tpu as pltpu

CHUNK = 64
N_BRANCHES = 3
A_HEADS, A_KV_HEADS, HEAD_DIM = 6, 2, 128
ROT_DIM = HEAD_DIM // 4
IDX_HEADS, IDX_DIM = 8, 64
IDX_ROT_DIM = IDX_DIM // 4
TOPK_MAX = 256
B_HEADS, B_KEY_DIM, B_VAL_DIM = 6, 128, 128
C_HEADS, C_HEAD_DIM = 4, 128
ROPE_THETA = 500000.0
EPS = 1e-6

A_WIDTH = A_HEADS * HEAD_DIM
KV_WIDTH = A_KV_HEADS * HEAD_DIM
IQ_WIDTH = IDX_HEADS * IDX_DIM
B_WIDTH = B_HEADS * B_KEY_DIM
C_WIDTH = C_HEADS * C_HEAD_DIM

LANES = 128
SUBLANES = 8
VMEM_LIMIT_BYTES = 56 * 1024 * 1024

MXU_DTYPE = jnp.bfloat16
F32 = jnp.float32
I32 = jnp.int32

IW_LANE = 96
NEG_BIG = -1e30
LOG2E = 1.4426950408889634
ATT_ROWS = 128
INT_MIN = -(2 ** 31)
F32_LOWEST = -3.4028234663852886e38

OFF_AQ = 0
OFF_AK = OFF_AQ + A_WIDTH
OFF_AV = OFF_AK + KV_WIDTH
OFF_IQ = OFF_AV + KV_WIDTH
OFF_IKW = OFF_IQ + IQ_WIDTH
OFF_BQ = OFF_IKW + LANES
OFF_BF = OFF_BQ + B_WIDTH
OFF_BI = OFF_BF + B_WIDTH
OFF_BG = OFF_BI + B_WIDTH
OFF_CQ = OFF_BG + B_WIDTH
OFF_GATES = OFF_CQ + C_WIDTH


def _sigmoid(x):
    return 1.0 / (1.0 + jnp.exp(-x))


def _dot(a, b):
    return jnp.dot(a, b, preferred_element_type=F32)


def _dot_nt(a, b):
    return lax.dot_general(a, b, (((1,), (1,)), ((), ())), preferred_element_type=F32)


def _rms(x, gain):
    return x * lax.rsqrt(jnp.mean(x * x, axis=-1, keepdims=True) + EPS) * gain


def _resident(shape):
    zeros = (0,) * len(shape)
    return pl.BlockSpec(shape, lambda *_: zeros, pipeline_mode=pl.Buffered(1))


def _rope(y, cos, sin_up, sin_down, half):
    return y * cos + pltpu.roll(y, half, 1) * sin_up + pltpu.roll(y, LANES - half, 1) * sin_down


def _in_proj_kernel(x_ref, gain_ref, lb_ref, w_ref, tri_ref, ca_ref, sau_ref, sad_ref, ci_ref, siu_ref, sid_ref,
                    aq_ref, k32_ref, v32_ref, k16_ref, v16_ref, iq_ref, ikw_ref, ika_ref, ikb_ref,
                    hq_ref, hk_ref, hlf_ref, hv_ref, hg_ref, cq_ref, gt_ref, ik32_ref):
    hb = _rms(x_ref[...], gain_ref[...]).astype(MXU_DTYPE)

    def proj(c0, width):
        return _dot(hb, w_ref[:, c0:c0 + width])

    ca, sau, sad = ca_ref[...], sau_ref[...], sad_ref[...]
    ci, siu, sid = ci_ref[...], siu_ref[...], sid_ref[...]

    y = proj(OFF_AQ, A_WIDTH)
    for h in range(A_HEADS):
        sl = slice(h * LANES, (h + 1) * LANES)
        aq_ref[:, sl] = (_rope(y[:, sl], ca, sau, sad, ROT_DIM // 2) * ((HEAD_DIM ** -0.5) * LOG2E)).astype(MXU_DTYPE)

    y = proj(OFF_AK, KV_WIDTH)
    for h in range(A_KV_HEADS):
        sl = slice(h * LANES, (h + 1) * LANES)
        r = _rope(y[:, sl], ca, sau, sad, ROT_DIM // 2)
        k32_ref[:, sl] = r
        k16_ref[:, sl] = r.astype(MXU_DTYPE)

    y = proj(OFF_AV, KV_WIDTH)
    v32_ref[...] = y
    v16_ref[...] = y.astype(MXU_DTYPE)

    y = proj(OFF_IQ, IQ_WIDTH)
    for j in range(IQ_WIDTH // LANES):
        sl = slice(j * LANES, (j + 1) * LANES)
        iq_ref[:, sl] = _rope(y[:, sl], ci, siu, sid, IDX_ROT_DIM // 2).astype(MXU_DTYPE)

    r = _rope(proj(OFF_IKW, LANES), ci, siu, sid, IDX_ROT_DIM // 2)
    lane = lax.broadcasted_iota(I32, r.shape, 1)
    ikw_ref[...] = jnp.where(lane < IDX_DIM, r, r * ((IDX_DIM ** -0.5) * (IDX_HEADS ** -0.5)))
    ik32_ref[...] = r[:, :IDX_DIM]
    ik_low = jnp.where(lane < IDX_DIM, r, 0.0)
    ika_ref[...] = ik_low.astype(MXU_DTYPE)
    ikb_ref[...] = pltpu.roll(ik_low, IDX_DIM, 1).astype(MXU_DTYPE)

    y = proj(OFF_BQ, B_WIDTH)
    hq_ref[...] = (y * _sigmoid(y)).astype(MXU_DTYPE)
    lb = lb_ref[...]
    f = lb + (1.0 - lb) * _sigmoid(proj(OFF_BF, B_WIDTH))
    hk_ref[...] = (1.0 - f).astype(MXU_DTYPE)
    hi, mid, lo = _split3(jnp.log(f))
    tri = tri_ref[...]
    hlf_ref[...] = _dot(tri, hi) + _dot(tri, mid) + _dot(tri, lo)
    hv_ref[...] = proj(OFF_BI, B_WIDTH).astype(MXU_DTYPE)
    y = proj(OFF_BG, B_WIDTH)
    hg_ref[...] = (y * _sigmoid(y)).astype(MXU_DTYPE)

    cq_ref[...] = (proj(OFF_CQ, C_WIDTH) * (C_HEAD_DIM ** -0.5)).astype(MXU_DTYPE)

    n_gate = gt_ref.shape[1]
    for c0 in range(0, n_gate, B_WIDTH):
        gt_ref[:, c0:c0 + B_WIDTH] = _sigmoid(proj(OFF_GATES + c0, B_WIDTH)).astype(MXU_DTYPE)


def _rope_tables(pos, rot_dim, head_dim):
    half = rot_dim // 2
    inv_freq = ROPE_THETA ** (-jnp.arange(half, dtype=F32) / half)
    ang = pos.astype(F32)[:, None] * inv_freq[None, :]
    cos, sin = jnp.cos(ang), jnp.sin(ang)
    lane = np.arange(LANES) % head_dim
    idx = lane % half
    first, second = lane < half, (lane >= half) & (lane < rot_dim)
    cos_t = jnp.where(first | second, cos[:, idx], 1.0)
    sin_up = jnp.where(second, sin[:, idx], 0.0)
    sin_down = jnp.where(first, -sin[:, idx], 0.0)
    return cos_t, sin_up, sin_down


def _in_proj(x, pos, gain, lower_bound, w_packed, row_tile):
    bsz, t, d = x.shape
    n = bsz * t
    tm = row_tile
    assert n % tm == 0 and (t % tm == 0 or tm % t == 0)
    tables = _rope_tables(pos, ROT_DIM, HEAD_DIM) + _rope_tables(pos, IDX_ROT_DIM, IDX_DIM)
    table_rows = max(t, tm)
    tables = [jnp.tile(tb, (table_rows // t, 1)) for tb in tables]
    n_table_blocks = table_rows // tm
    n_gate = N_BRANCHES * d
    w_width = w_packed.shape[1]
    assert tm % CHUNK == 0 and t % CHUNK == 0
    r = np.arange(tm)
    chunk_tri = jnp.asarray((r[:, None] // CHUNK == r[None, :] // CHUNK) & (r[None, :] <= r[:, None]), MXU_DTYPE)

    def rows(width):
        return pl.BlockSpec((tm, width), lambda i: (i, 0))

    table_spec = pl.BlockSpec((tm, LANES), lambda i: (i % n_table_blocks, 0))
    widths_dtypes = [
        (A_WIDTH, MXU_DTYPE), (KV_WIDTH, F32), (KV_WIDTH, F32), (KV_WIDTH, MXU_DTYPE), (KV_WIDTH, MXU_DTYPE),
        (IQ_WIDTH, MXU_DTYPE), (LANES, F32), (LANES, MXU_DTYPE), (LANES, MXU_DTYPE),
        (B_WIDTH, MXU_DTYPE), (B_WIDTH, MXU_DTYPE), (B_WIDTH, F32), (B_WIDTH, MXU_DTYPE), (B_WIDTH, MXU_DTYPE),
        (C_WIDTH, MXU_DTYPE), (n_gate, MXU_DTYPE), (IDX_DIM, F32)]
    outs = pl.pallas_call(
        _in_proj_kernel,
        grid=(n // tm,),
        in_specs=[rows(d), _resident((1, d)), _resident((1, B_WIDTH)), _resident((d, w_width)), _resident((tm, tm))]
        + [table_spec] * 6,
        out_specs=[rows(w) for w, _ in widths_dtypes],
        out_shape=[jax.ShapeDtypeStruct((n, w), dt) for w, dt in widths_dtypes],
        compiler_params=pltpu.CompilerParams(dimension_semantics=("arbitrary",), vmem_limit_bytes=VMEM_LIMIT_BYTES),
        name="in_proj",
    )(x.reshape(n, d), gain.reshape(1, d), lower_bound.reshape(1, B_WIDTH), w_packed, chunk_tri, *tables)
    return [o.reshape(bsz, t, o.shape[-1]) for o in outs]


def _pack_w_in(w, d):
    widths = (A_WIDTH, KV_WIDTH, KV_WIDTH, IQ_WIDTH, IDX_DIM, IDX_HEADS, B_WIDTH, B_WIDTH, B_WIDTH, B_WIDTH,
              C_WIDTH, N_BRANCHES * d)
    points = [int(s) for s in np.cumsum(widths)[:-1]]
    a_q, a_k, a_v, i_q, i_k, i_w, b_q, b_f, b_i, b_g, c_q, gates = jnp.split(w, points, axis=-1)
    ikw = jnp.concatenate([i_k, jnp.zeros((d, IW_LANE - IDX_DIM), w.dtype), i_w,
                           jnp.zeros((d, LANES - IW_LANE - IDX_HEADS), w.dtype)], axis=-1)
    return jnp.concatenate([a_q, a_k, a_v, i_q, ikw, b_q, b_f, b_i, b_g, c_q, gates], axis=-1).astype(MXU_DTYPE)


def _float_of_key(key):
    return pltpu.bitcast(jnp.where(key < 0, (key - 1) ^ 0x7FFFFFFF, key), F32)


def _lane_tiled(x, n_tiles):
    return x if n_tiles == 1 else jnp.concatenate([x] * n_tiles, axis=1)


COUNT_GROUPS = 8
COUNT_TIERS = ((8, 6), (4, 3))
TIE_ROWS = 128


def _dsa_kernel(aq_ref, iq_ref, iw_ref, ika_ref, ikb_ref, k_ref, v_ref, ya_ref,
                key_sc, bias_sc, m_sc, l_sc, acc_sc, *, tq, tk, key_off, topk, q_sub):
    qi = pl.program_id(1)
    n_vis = key_off + (qi + 1) * tq
    n_kb = lax.div(n_vis + (tk - 1), tk)
    q_pos = qi * tq + lax.broadcasted_iota(I32, (1, tq), 1)
    limit = key_off + (lax.shift_right_logical(q_pos, CHUNK.bit_length() - 1) + 1) * CHUNK

    def for_each_block(body):
        def pair(j, carry):
            body(2 * j, 0)
            body(2 * j + 1, 1)
            return carry

        lax.fori_loop(0, lax.shift_right_logical(n_kb, 1), pair, 0)

        @pl.when((n_kb & 1) == 1)
        def _():
            body(n_kb - 1, 0)

    def score_block(kb, slot):
        k0 = pl.multiple_of(kb * tk, tk)
        ika = ika_ref[0, pl.ds(k0, tk), :]
        ikb = ikb_ref[0, pl.ds(k0, tk), :]
        acc = jnp.zeros((tk, tq), F32)
        for j in range(IQ_WIDTH // LANES):
            iqt = iq_ref[0, :, j * LANES:(j + 1) * LANES]
            acc = acc + jnp.maximum(_dot_nt(ika, iqt), 0.0) * iw_ref[0, 2 * j:2 * j + 1, :]
            acc = acc + jnp.maximum(_dot_nt(ikb, iqt), 0.0) * iw_ref[0, 2 * j + 1:2 * j + 2, :]
        key_pos = k0 + lax.broadcasted_iota(I32, (tk, tq), 0)
        key_sc[kb] = jnp.where(key_pos < limit, acc, -jnp.inf)

    for_each_block(score_block)

    ones_rows = jnp.ones((2 * SUBLANES, tk), MXU_DTYPE)

    def count(pred):
        step = SUBLANES * COUNT_GROUPS

        def hits(kb):
            return _dot(ones_rows, jnp.where(pred(key_sc[kb]), 1.0, 0.0).astype(MXU_DTYPE))

        def one_block(kb, acc):
            for r0 in range(0, tk, step):
                acc = acc + jnp.where(pred(key_sc[kb, r0:r0 + step, :]), 1, 0)
            return acc

        def group_body(first, size, n_matmul):
            def group(j, carry):
                by_matmul, by_adds = carry
                kb = first + size * j
                parts = [hits(kb + i) for i in range(n_matmul)]
                while len(parts) > 1:
                    parts = [a + b for a, b in zip(parts[0::2], parts[1::2])] + parts[len(parts) & ~1:]
                for i in range(n_matmul, size):
                    by_adds = one_block(kb + i, by_adds)
                return by_matmul + parts[0], by_adds
            return group

        carry = (jnp.zeros((2 * SUBLANES, tq), F32), jnp.zeros((step, tq), I32))
        first = 0
        for size, n_matmul in COUNT_TIERS:
            n_groups = lax.div(n_kb - first, size)
            carry = lax.fori_loop(0, n_groups, group_body(first, size, n_matmul), carry)
            first = first + size * n_groups
        by_matmul, by_adds = carry
        by_adds = lax.fori_loop(first, n_kb, one_block, by_adds)
        return by_matmul[0:1, :].astype(I32) + jnp.sum(by_adds, axis=0, keepdims=True)

    def key_bit(it, u):
        trial = u | jnp.left_shift(jnp.int32(1), 31 - it)
        cand = _float_of_key(trial ^ INT_MIN)
        return jnp.where(count(lambda s: s >= cand) >= topk, trial, u)

    u = lax.fori_loop(0, 32, key_bit, jnp.zeros((1, tq), I32))
    thr = _float_of_key(u ^ INT_MIN)
    n_gt = count(lambda s: s > thr)
    n_ge = count(lambda s: s >= thr)
    has_tie = jnp.where((n_ge > topk) & (thr > -jnp.inf), 1.0, 0.0)

    @pl.when(jnp.max(has_tie) > 0.0)
    def _():
        need = (topk - n_gt).astype(F32)
        ri = lax.broadcasted_iota(I32, (TIE_ROWS, TIE_ROWS), 0)
        ci = lax.broadcasted_iota(I32, (TIE_ROWS, TIE_ROWS), 1)
        lower = jnp.where(ci <= ri, 1.0, 0.0).astype(MXU_DTYPE)

        def demote(kb, seen):
            pieces = []
            for r0 in range(0, tk, TIE_ROWS):
                blk = key_sc[kb, r0:r0 + TIE_ROWS, :]
                tie = blk == thr
                pieces.append((r0, blk, tie, _dot(lower, jnp.where(tie, 1.0, 0.0).astype(MXU_DTYPE))))
            for r0, blk, tie, prefix in pieces:
                rank = prefix + seen
                key_sc[kb, r0:r0 + TIE_ROWS, :] = jnp.where(tie & (rank > need), -jnp.inf, blk)
                seen = rank[TIE_ROWS - 1:TIE_ROWS, :]
            return seen

        lax.fori_loop(0, n_kb, demote, jnp.zeros((1, tq), F32))

    thr_sel = jnp.maximum(thr, F32_LOWEST)

    m_sc[...] = jnp.full(m_sc.shape, NEG_BIG, F32)
    l_sc[...] = jnp.zeros(l_sc.shape, F32)
    acc_sc[...] = jnp.zeros(acc_sc.shape, F32)
    group = A_HEADS // A_KV_HEADS
    n_lane_tiles = tk // LANES
    ones_blk = jnp.ones((tk, LANES), MXU_DTYPE)

    def attend_block(kb, slot):
        k0 = pl.multiple_of(kb * tk, tk)
        bias_sc[slot] = jnp.where(key_sc[kb] >= thr_sel, 0.0, NEG_BIG).T
        for g in range(A_KV_HEADS):
            kblk = k_ref[0, pl.ds(k0, tk), g * LANES:(g + 1) * LANES]
            vaug = jnp.concatenate([v_ref[0, pl.ds(k0, tk), g * LANES:(g + 1) * LANES], ones_blk], axis=1)
            for r in range(group):
                h = g * group + r
                for rt in range(tq // q_sub):
                    rows = pl.ds(rt * q_sub, q_sub)
                    q = aq_ref[0, rows, h * LANES:(h + 1) * LANES]
                    s = _dot_nt(q, kblk) + bias_sc[slot, rows, :]
                    m_prev = m_sc[h, rows, :]
                    m_new = jnp.maximum(m_prev, jnp.max(s, axis=1, keepdims=True))
                    alpha = jnp.exp2(m_prev - m_new)
                    p = jnp.exp2(s - _lane_tiled(m_new, n_lane_tiles))
                    pv = _dot(p.astype(MXU_DTYPE), vaug)
                    acc_sc[h, rows, :] = alpha * acc_sc[h, rows, :] + pv[:, :LANES]
                    l_sc[h, rows, :] = alpha * l_sc[h, rows, :] + pv[:, LANES:]
                    m_sc[h, rows, :] = m_new
    for_each_block(attend_block)
    for h in range(A_HEADS):
        ya_ref[0, :, h * LANES:(h + 1) * LANES] = (acc_sc[h] / l_sc[h]).astype(ya_ref.dtype)


def _dsa(aq, iq, iw_t, ika, ikb, k16, v16, *, key_off, topk, q_tile, key_tile):
    bsz, t, _ = aq.shape
    lp = k16.shape[1]
    tq, tk = q_tile, key_tile
    assert lp % tk == 0
    n_kb_all = lp // tk
    assert t % tq == 0 and tq % LANES == 0 and tk % (SUBLANES * COUNT_GROUPS) == 0 and key_off + t <= lp
    assert tk >= topk
    kernel = functools.partial(_dsa_kernel, tq=tq, tk=tk, key_off=key_off, topk=topk, q_sub=min(tq, ATT_ROWS))

    def q_spec(width):
        return pl.BlockSpec((1, tq, width), lambda b, i: (b, i, 0))

    def key_spec(width):
        return pl.BlockSpec((1, lp, width), lambda b, i: (b, 0, 0), pipeline_mode=pl.Buffered(1))

    return pl.pallas_call(
        kernel,
        grid=(bsz, t // tq),
        in_specs=[q_spec(A_WIDTH), q_spec(IQ_WIDTH), pl.BlockSpec((1, IDX_HEADS, tq), lambda b, i: (b, 0, i)),
                  key_spec(LANES), key_spec(LANES), key_spec(KV_WIDTH), key_spec(KV_WIDTH)],
        out_specs=q_spec(A_WIDTH),
        out_shape=jax.ShapeDtypeStruct((bsz, t, A_WIDTH), MXU_DTYPE),
        scratch_shapes=[pltpu.VMEM((n_kb_all, tk, tq), F32), pltpu.VMEM((2, tq, tk), F32),
                        pltpu.VMEM((A_HEADS, tq, LANES), F32), pltpu.VMEM((A_HEADS, tq, LANES), F32),
                        pltpu.VMEM((A_HEADS, tq, LANES), F32)],
        compiler_params=pltpu.CompilerParams(dimension_semantics=("arbitrary", "arbitrary"),
                                             vmem_limit_bytes=VMEM_LIMIT_BYTES),
        name="dsa",
    )(aq, iq, iw_t, ika, ikb, k16, v16)


SUB = SUBLANES
HGRN_ROWS = 4
HGRN_GROUP_HEADS = 2
N_SUB = CHUNK // SUB
INTER_ROWS = SUB * (N_SUB * (N_SUB - 1) // 2)


def _split3(x):
    hi = x.astype(MXU_DTYPE)
    r1 = x - hi.astype(F32)
    mid = r1.astype(MXU_DTYPE)
    lo = (r1 - mid.astype(F32)).astype(MXU_DTYPE)
    return hi, mid, lo


def _hgrn_constants():
    t = np.arange(CHUNK)
    place = ((np.arange(SUB * LANES)[:, None] // LANES) == (t[None, :] % SUB)).astype(np.float32)
    intra = ((t[:, None] // SUB == t[None, :] // SUB) & (t[None, :] <= t[:, None])).astype(np.float32)
    seg = np.concatenate([np.full(i * SUB, i) for i in range(1, N_SUB)])
    inter = ((t[:, None] // SUB) == seg[None, :]).astype(np.float32)
    return jnp.asarray(place, MXU_DTYPE), jnp.asarray(intra, F32), jnp.asarray(inter, F32)


def _hgrn_kernel(q_ref, k_ref, b_ref, v_ref, g_ref, gain_ref, s0_ref, place_ref, intra_ref, inter_ref,
                 yb_ref, sout_ref, st_sc):
    c = pl.program_id(1)
    n_rows = q_ref.shape[0]

    @pl.when(c == 0)
    def _():
        for i in range(n_rows):
            for h in range(B_HEADS):
                st_sc[i, h] = s0_ref[i, h].T

    def row_bcast(x, r, n):
        return jnp.broadcast_to(x[r:r + 1, :], (n, x.shape[1]))

    place = place_ref[...]

    def head_group(i, g):
        width = HGRN_GROUP_HEADS * LANES
        lanes = slice(g * width, (g + 1) * width)
        q = q_ref[i, :, lanes].astype(F32)
        k = k_ref[i, :, lanes].astype(F32)
        v16 = v_ref[i, :, lanes]
        b = b_ref[i, :, lanes]

        b_start = jnp.concatenate(
            [jnp.zeros((SUB, width), F32)] + [row_bcast(b, j * SUB - 1, SUB) for j in range(1, N_SUB)], axis=0)
        q_hat = (q * jnp.exp(b - b_start)).astype(MXU_DTYPE)
        q_state = (q * jnp.exp(b)).astype(MXU_DTYPE)
        b_end = b[CHUNK - 1:CHUNK, :]
        k_end = (k * jnp.exp(b_end - b)).astype(MXU_DTYPE)

        k_hat = jnp.concatenate(
            [k[:j * SUB] * jnp.exp(row_bcast(b, j * SUB - 1, j * SUB) - b[:j * SUB]) for j in range(1, N_SUB)],
            axis=0).astype(MXU_DTYPE)
        v_hat = jnp.concatenate([v16[:j * SUB] for j in range(1, N_SUB)], axis=0)

        q3 = q.reshape(N_SUB, SUB, width)
        k3 = k.reshape(N_SUB, SUB, width)
        b3 = b.reshape(N_SUB, SUB, width)
        intra = []
        for s in range(SUB):
            kp = jnp.broadcast_to(k3[:, s:s + 1, :], q3.shape)
            bp = jnp.broadcast_to(b3[:, s:s + 1, :], q3.shape)
            x = q3 * kp * jnp.exp(jnp.minimum(b3 - bp, 0.0))
            intra.append(x.reshape(CHUNK, width).astype(MXU_DTYPE))

        for hh in range(HGRN_GROUP_HEADS):
            h = g * HGRN_GROUP_HEADS + hh
            sl = slice(hh * LANES, (hh + 1) * LANES)
            out_lanes = slice(h * LANES, (h + 1) * LANES)
            st = st_sc[i, h]
            x_cat = jnp.concatenate([x[:, sl] for x in intra], axis=1)
            p_intra = (_dot(x_cat, place) * intra_ref[...]).astype(MXU_DTYPE)
            p_inter = (_dot_nt(q_hat[:, sl], k_hat[:, sl]) * inter_ref[...]).astype(MXU_DTYPE)
            o = (_dot(p_inter, v_hat[:, sl]) + _dot(p_intra, v16[:, sl])
                 + _dot_nt(q_state[:, sl], st.astype(MXU_DTYPE)))
            st_sc[i, h] = st * jnp.exp(b_end[:, sl]) + _dot(v16[:, sl].astype(F32).T.astype(MXU_DTYPE), k_end[:, sl])
            y = _rms(o, gain_ref[:, out_lanes]) * g_ref[i, :, out_lanes].astype(F32)
            yb_ref[i, :, out_lanes] = y.astype(yb_ref.dtype)

    for i in range(n_rows):
        for g in range(B_HEADS // HGRN_GROUP_HEADS):
            head_group(i, g)

    @pl.when(c == pl.num_programs(1) - 1)
    def _():
        for i in range(n_rows):
            for h in range(B_HEADS):
                sout_ref[i, h] = st_sc[i, h].T


def _hgrn(hq, hk, hlf, hv, hg, norm_gain, state0):
    bsz, t, _ = hq.shape
    assert t % CHUNK == 0 and bsz % HGRN_ROWS == 0
    nb = HGRN_ROWS
    consts = _hgrn_constants()
    seq_spec = pl.BlockSpec((nb, CHUNK, B_WIDTH), lambda b, c: (b, c, 0))
    state_spec = pl.BlockSpec((nb, B_HEADS, B_KEY_DIM, B_VAL_DIM), lambda b, c: (b, 0, 0, 0))
    return pl.pallas_call(
        _hgrn_kernel,
        grid=(bsz // nb, t // CHUNK),
        in_specs=[seq_spec] * 5 + [_resident((1, B_WIDTH)), state_spec] + [_resident(x.shape) for x in consts],
        out_specs=[seq_spec, state_spec],
        out_shape=[jax.ShapeDtypeStruct((bsz, t, B_WIDTH), MXU_DTYPE),
                   jax.ShapeDtypeStruct((bsz, B_HEADS, B_KEY_DIM, B_VAL_DIM), F32)],
        scratch_shapes=[pltpu.VMEM((nb, B_HEADS, B_VAL_DIM, B_KEY_DIM), F32)],
        compiler_params=pltpu.CompilerParams(dimension_semantics=("arbitrary", "arbitrary"),
                                             vmem_limit_bytes=VMEM_LIMIT_BYTES),
        name="hgrn",
    )(hq, hk, hlf, hv, hg, norm_gain.reshape(1, B_WIDTH), state0, *consts)


def _memory_kv_kernel(mem_ref, gain_ref, w_ref, k32_ref, v32_ref, k16_ref, v16_ref):
    hb = _rms(mem_ref[0], gain_ref[...]).astype(MXU_DTYPE)
    mk = _dot(hb, w_ref[:, :C_WIDTH])
    mv = _dot(hb, w_ref[:, C_WIDTH:])
    k32_ref[0] = mk
    v32_ref[0] = mv
    k16_ref[0] = mk.astype(MXU_DTYPE)
    v16_ref[0] = mv.astype(MXU_DTYPE)


def _memory_kv(mem, gain, w16):
    bsz, n, d = mem.shape
    spec = pl.BlockSpec((1, n, C_WIDTH), lambda b: (b, 0, 0))
    return pl.pallas_call(
        _memory_kv_kernel,
        grid=(bsz,),
        in_specs=[pl.BlockSpec((1, n, d), lambda b: (b, 0, 0)), _resident((1, d)), _resident((d, 2 * C_WIDTH))],
        out_specs=[spec] * 4,
        out_shape=[jax.ShapeDtypeStruct((bsz, n, C_WIDTH), dt) for dt in (F32, F32, MXU_DTYPE, MXU_DTYPE)],
        compiler_params=pltpu.CompilerParams(dimension_semantics=("arbitrary",), vmem_limit_bytes=VMEM_LIMIT_BYTES),
        name="memory_kv",
    )(mem, gain.reshape(1, d), w16)


def _merge_ffn_kernel(x_ref, ya_ref, yb_ref, cq_ref, mk_ref, mv_ref, gt_ref, wa_ref, wb_ref, wc_ref, wo_ref,
                      g_mix_ref, g_pre_ref, g_post_ref, wu_ref, wd_ref, out_ref):
    nb, tm, d = x_ref.shape
    rows = nb * tm

    def flat(ref):
        return ref[...].reshape(rows, ref.shape[-1])

    per_row = []
    for i in range(nb):
        heads = []
        for h in range(C_HEADS):
            sl = slice(h * C_HEAD_DIM, (h + 1) * C_HEAD_DIM)
            s = _dot_nt(cq_ref[i, :, sl], mk_ref[i, :, sl])
            p = jnp.exp(s - jnp.max(s, axis=-1, keepdims=True))
            p = p / jnp.sum(p, axis=-1, keepdims=True)
            heads.append(_dot(p.astype(MXU_DTYPE), mv_ref[i, :, sl]).astype(MXU_DTYPE))
        per_row.append(jnp.concatenate(heads, axis=1))
    y_c = per_row[0] if nb == 1 else jnp.concatenate(per_row, axis=0)

    gt = flat(gt_ref)
    merged = (gt[:, 0:d].astype(F32) * _dot(flat(ya_ref), wa_ref[...])
              + gt[:, d:2 * d].astype(F32) * _dot(flat(yb_ref), wb_ref[...])
              + gt[:, 2 * d:3 * d].astype(F32) * _dot(y_c, wc_ref[...]))
    x1 = flat(x_ref) + _rms(_dot(merged.astype(MXU_DTYPE), wo_ref[...]), g_mix_ref[...])
    u = jnp.maximum(_dot(_rms(x1, g_pre_ref[...]).astype(MXU_DTYPE), wu_ref[...]), 0.0)
    down = _dot((u * u).astype(MXU_DTYPE), wd_ref[...])
    out_ref[...] = (x1 + _rms(down, g_post_ref[...])).reshape(nb, tm, d)


def _merge_ffn(x, ya, yb, cq, mk16, mv16, gates, weights, gains, row_tile):
    bsz, t, d = x.shape
    tm = min(row_tile, t)
    nb = max(1, min(bsz, row_tile // t))
    assert t % tm == 0 and bsz % nb == 0
    n_mem = mk16.shape[1]

    def rows(width):
        return pl.BlockSpec((nb, tm, width), lambda b, i: (b, i, 0))

    mem_spec = pl.BlockSpec((nb, n_mem, C_WIDTH), lambda b, i: (b, 0, 0))
    return pl.pallas_call(
        _merge_ffn_kernel,
        grid=(bsz // nb, t // tm),
        in_specs=[rows(d), rows(A_WIDTH), rows(B_WIDTH), rows(C_WIDTH), mem_spec, mem_spec, rows(N_BRANCHES * d)]
        + [_resident(w.shape) for w in weights[:4]] + [_resident((1, d))] * 3
        + [_resident(w.shape) for w in weights[4:]],
        out_specs=rows(d),
        out_shape=jax.ShapeDtypeStruct((bsz, t, d), F32),
        compiler_params=pltpu.CompilerParams(dimension_semantics=("arbitrary", "arbitrary"),
                                             vmem_limit_bytes=VMEM_LIMIT_BYTES),
        name="merge_ffn",
    )(x, ya, yb, cq, mk16, mv16, gates, *weights[:4], *[g.reshape(1, d) for g in gains], *weights[4:])


def _layer(x, pos, lower_bound, lp, *, cache, state0, mem16, row_tile, q_tile, key_tile):
    bsz, t, d = x.shape
    (aq, k32, v32, k16, v16, iq, ikw, ika, ikb, hq, hk, hlf, hv, hg, cq, gates, ik32) = _in_proj(
        x, pos, lp["pre_mix_gain"], lower_bound, lp["w_in"], min(row_tile, bsz * t))

    keys = [ika, ikb, k16, v16]
    if cache is not None:
        keys = [jnp.concatenate([c, kn], axis=1) for c, kn in zip(cache, keys)]
    n_keys = keys[0].shape[1]
    t_pad = -(-t // q_tile) * q_tile
    key_off = n_keys - t
    n_pad = -(-(key_off + t_pad) // key_tile) * key_tile
    ika, ikb, k16, v16 = [jnp.pad(kk, ((0, 0), (0, n_pad - n_keys), (0, 0))) for kk in keys]
    iw_t = jnp.swapaxes(ikw[..., IW_LANE:IW_LANE + IDX_HEADS], 1, 2)
    q_pad = ((0, 0), (0, t_pad - t), (0, 0))
    ya = _dsa(jnp.pad(aq, q_pad), jnp.pad(iq, q_pad), jnp.pad(iw_t, ((0, 0), (0, 0), (0, t_pad - t))),
              ika, ikb, k16, v16, key_off=key_off, topk=min(TOPK_MAX, n_keys // 4), q_tile=q_tile,
              key_tile=key_tile)[:, :t]

    yb, s_new = _hgrn(hq, hk, hlf, hv, hg, lp["hgrn_norm_gain"], state0)

    out = _merge_ffn(x, ya, yb, cq, mem16[0], mem16[1], gates,
                     [lp["w_out_a"], lp["w_out_b"], lp["w_out_c"], lp["w_out"], lp["w_up"], lp["w_down"]],
                     [lp["post_mix_gain"], lp["pre_ffn_gain"], lp["post_ffn_gain"]], row_tile=MERGE_ROWS)
    new_k = k32.reshape(bsz, t, A_KV_HEADS, HEAD_DIM)
    new_v = v32.reshape(bsz, t, A_KV_HEADS, HEAD_DIM)
    return out, (new_k, new_v, ik32, s_new)


MERGE_ROWS = 512
DSA_QUERIES = 512


def _tiles(t):
    return 256, max(LANES, min(DSA_QUERIES, t))


def kernel(x_prompt, x_sample, cache_k, cache_v, cache_idx_k, state_hgrn, cache_mem_k, cache_mem_v, mem_prompt, w_in, w_mem_kv, mem_norm_gain, hgrn_lb_logits, hgrn_norm_gain, w_out_a, w_out_b, w_out_c, w_out, pre_mix_gain, post_mix_gain, pre_ffn_gain, post_ffn_gain, w_up, w_down):
    depth = w_in.shape[0]
    d = x_prompt.shape[-1]
    bp, tp, _ = x_prompt.shape
    bs, ts, _ = x_sample.shape
    past = cache_k.shape[2]
    lower_bounds = jnp.cumsum(jax.nn.softmax(hgrn_lb_logits.astype(F32), axis=0), axis=0)
    pos_p = jnp.arange(tp, dtype=I32)
    pos_s = past + jnp.arange(ts, dtype=I32)
    bf = lambda a: a.astype(MXU_DTYPE)

    xp, xs = x_prompt, x_sample
    outs_p, outs_s = [], []
    for l in range(depth):
        lp = {
            "w_in": _pack_w_in(w_in[l], d), "hgrn_norm_gain": hgrn_norm_gain[l],
            "w_out_a": bf(w_out_a[l]), "w_out_b": bf(w_out_b[l]), "w_out_c": bf(w_out_c[l]), "w_out": bf(w_out[l]),
            "pre_mix_gain": pre_mix_gain[l], "post_mix_gain": post_mix_gain[l], "pre_ffn_gain": pre_ffn_gain[l],
            "post_ffn_gain": post_ffn_gain[l], "w_up": bf(w_up[l]), "w_down": bf(w_down[l]),
        }
        mk32, mv32, mk16, mv16 = _memory_kv(mem_prompt, mem_norm_gain[l], bf(w_mem_kv[l]))
        n_mem = mem_prompt.shape[1]
        row_tile, q_tile = _tiles(tp)
        xp, st_p = _layer(xp, pos_p, lower_bounds[l], lp, cache=None,
                          state0=jnp.zeros((bp, B_HEADS, B_KEY_DIM, B_VAL_DIM), F32), mem16=(mk16, mv16),
                          row_tile=row_tile, q_tile=q_tile, key_tile=512)
        outs_p.append(st_p + (mk32.reshape(bp, n_mem, C_HEADS, C_HEAD_DIM), mv32.reshape(bp, n_mem, C_HEADS, C_HEAD_DIM)))

        cik = cache_idx_k[l]
        zeros_ik = jnp.zeros_like(cik)
        cache = (bf(jnp.concatenate([cik, zeros_ik], axis=-1)), bf(jnp.concatenate([zeros_ik, cik], axis=-1)),
                 bf(cache_k[l].reshape(bs, past, KV_WIDTH)), bf(cache_v[l].reshape(bs, past, KV_WIDTH)))
        mem16 = (bf(cache_mem_k[l].reshape(bs, -1, C_WIDTH)), bf(cache_mem_v[l].reshape(bs, -1, C_WIDTH)))
        row_tile, q_tile = _tiles(ts)
        xs, st_s = _layer(xs, pos_s, lower_bounds[l], lp, cache=cache, state0=state_hgrn[l].astype(F32),
                          mem16=mem16, row_tile=row_tile, q_tile=q_tile, key_tile=384)
        outs_s.append(st_s)

    new_k_p, new_v_p, new_ik_p, new_s_p, new_mk_p, new_mv_p = [jnp.stack(a) for a in zip(*outs_p)]
    new_k_s, new_v_s, new_ik_s, new_s_s = [jnp.stack(a) for a in zip(*outs_s)]
    return (xp, xs, new_k_p, new_v_p, new_ik_p, new_s_p, new_mk_p, new_mv_p, new_k_s, new_v_s, new_ik_s, new_s_s)
```

```python
import functools

import jax
import jax.numpy as jnp
import numpy as np
from jax import lax
from jax.experimental import pallas as pl
from jax.experimental.pallas import tpu as pltpu

CHUNK = 64
N_BRANCHES = 3
A_HEADS, A_KV_HEADS, HEAD_DIM = 6, 2, 128
ROT_DIM = HEAD_DIM // 4
IDX_HEADS, IDX_DIM = 8, 64
IDX_ROT_DIM = IDX_DIM // 4
TOPK_MAX = 256
B_HEADS, B_KEY_DIM, B_VAL_DIM = 6, 128, 128
C_HEADS, C_HEAD_DIM = 4, 128
ROPE_THETA = 500000.0
EPS = 1e-6

A_WIDTH = A_HEADS * HEAD_DIM
KV_WIDTH = A_KV_HEADS * HEAD_DIM
IQ_WIDTH = IDX_HEADS * IDX_DIM
B_WIDTH = B_HEADS * B_KEY_DIM
C_WIDTH = C_HEADS * C_HEAD_DIM

LANES = 128
SUBLANES = 8
VMEM_LIMIT_BYTES = 56 * 1024 * 1024

MXU_DTYPE = jnp.bfloat16
F32 = jnp.float32
I32 = jnp.int32

IW_LANE = 96
NEG_BIG = -1e30
LOG2E = 1.4426950408889634
ATT_ROWS = 128
INT_MIN = -(2 ** 31)
F32_LOWEST = -3.4028234663852886e38

OFF_AQ = 0
OFF_AK = OFF_AQ + A_WIDTH
OFF_AV = OFF_AK + KV_WIDTH
OFF_IQ = OFF_AV + KV_WIDTH
OFF_IKW = OFF_IQ + IQ_WIDTH
OFF_BQ = OFF_IKW + LANES
OFF_BF = OFF_BQ + B_WIDTH
OFF_BI = OFF_BF + B_WIDTH
OFF_BG = OFF_BI + B_WIDTH
OFF_CQ = OFF_BG + B_WIDTH
OFF_GATES = OFF_CQ + C_WIDTH


def _sigmoid(x):
    return 1.0 / (1.0 + jnp.exp(-x))


def _dot(a, b):
    return jnp.dot(a, b, preferred_element_type=F32)


def _dot_nt(a, b):
    return lax.dot_general(a, b, (((1,), (1,)), ((), ())), preferred_element_type=F32)


def _rms(x, gain):
    return x * lax.rsqrt(jnp.mean(x * x, axis=-1, keepdims=True) + EPS) * gain


def _resident(shape):
    zeros = (0,) * len(shape)
    return pl.BlockSpec(shape, lambda *_: zeros, pipeline_mode=pl.Buffered(1))


def _rope(y, cos, sin_up, sin_down, half):
    return y * cos + pltpu.roll(y, half, 1) * sin_up + pltpu.roll(y, LANES - half, 1) * sin_down


def _in_proj_kernel(x_ref, gain_ref, lb_ref, w_ref, tri_ref, ca_ref, sau_ref, sad_ref, ci_ref, siu_ref, sid_ref,
                    aq_ref, k32_ref, v32_ref, k16_ref, v16_ref, iq_ref, ikw_ref, ika_ref, ikb_ref,
                    hq_ref, hk_ref, hlf_ref, hv_ref, hg_ref, cq_ref, gt_ref, ik32_ref):
    hb = _rms(x_ref[...], gain_ref[...]).astype(MXU_DTYPE)

    def proj(c0, width):
        return _dot(hb, w_ref[:, c0:c0 + width])

    ca, sau, sad = ca_ref[...], sau_ref[...], sad_ref[...]
    ci, siu, sid = ci_ref[...], siu_ref[...], sid_ref[...]

    y = proj(OFF_AQ, A_WIDTH)
    for h in range(A_HEADS):
        sl = slice(h * LANES, (h + 1) * LANES)
        aq_ref[:, sl] = (_rope(y[:, sl], ca, sau, sad, ROT_DIM // 2) * ((HEAD_DIM ** -0.5) * LOG2E)).astype(MXU_DTYPE)

    y = proj(OFF_AK, KV_WIDTH)
    for h in range(A_KV_HEADS):
        sl = slice(h * LANES, (h + 1) * LANES)
        r = _rope(y[:, sl], ca, sau, sad, ROT_DIM // 2)
        k32_ref[:, sl] = r
        k16_ref[:, sl] = r.astype(MXU_DTYPE)

    y = proj(OFF_AV, KV_WIDTH)
    v32_ref[...] = y
    v16_ref[...] = y.astype(MXU_DTYPE)

    y = proj(OFF_IQ, IQ_WIDTH)
    for j in range(IQ_WIDTH // LANES):
        sl = slice(j * LANES, (j + 1) * LANES)
        iq_ref[:, sl] = _rope(y[:, sl], ci, siu, sid, IDX_ROT_DIM // 2).astype(MXU_DTYPE)

    r = _rope(proj(OFF_IKW, LANES), ci, siu, sid, IDX_ROT_DIM // 2)
    lane = lax.broadcasted_iota(I32, r.shape, 1)
    ikw_ref[...] = jnp.where(lane < IDX_DIM, r, r * ((IDX_DIM ** -0.5) * (IDX_HEADS ** -0.5)))
    ik32_ref[...] = r[:, :IDX_DIM]
    ik_low = jnp.where(lane < IDX_DIM, r, 0.0)
    ika_ref[...] = ik_low.astype(MXU_DTYPE)
    ikb_ref[...] = pltpu.roll(ik_low, IDX_DIM, 1).astype(MXU_DTYPE)

    y = proj(OFF_BQ, B_WIDTH)
    hq_ref[...] = (y * _sigmoid(y)).astype(MXU_DTYPE)
    lb = lb_ref[...]
    f = lb + (1.0 - lb) * _sigmoid(proj(OFF_BF, B_WIDTH))
    hk_ref[...] = (1.0 - f).astype(MXU_DTYPE)
    hi, mid, lo = _split3(jnp.log(f))
    tri = tri_ref[...]
    n_tri = tri.shape[0]
    for r0 in range(0, hlf_ref.shape[0], n_tri):
        rs = slice(r0, r0 + n_tri)
        hlf_ref[rs, :] = _dot(tri, hi[rs]) + _dot(tri, mid[rs]) + _dot(tri, lo[rs])
    hv_ref[...] = proj(OFF_BI, B_WIDTH).astype(MXU_DTYPE)
    y = proj(OFF_BG, B_WIDTH)
    hg_ref[...] = (y * _sigmoid(y)).astype(MXU_DTYPE)

    cq_ref[...] = (proj(OFF_CQ, C_WIDTH) * (C_HEAD_DIM ** -0.5)).astype(MXU_DTYPE)

    n_gate = gt_ref.shape[1]
    for c0 in range(0, n_gate, B_WIDTH):
        gt_ref[:, c0:c0 + B_WIDTH] = _sigmoid(proj(OFF_GATES + c0, B_WIDTH)).astype(MXU_DTYPE)


def _rope_tables(pos, rot_dim, head_dim):
    half = rot_dim // 2
    inv_freq = ROPE_THETA ** (-jnp.arange(half, dtype=F32) / half)
    ang = pos.astype(F32)[:, None] * inv_freq[None, :]
    cos, sin = jnp.cos(ang), jnp.sin(ang)
    lane = np.arange(LANES) % head_dim
    idx = lane % half
    first, second = lane < half, (lane >= half) & (lane < rot_dim)
    cos_t = jnp.where(first | second, cos[:, idx], 1.0)
    sin_up = jnp.where(second, sin[:, idx], 0.0)
    sin_down = jnp.where(first, -sin[:, idx], 0.0)
    return cos_t, sin_up, sin_down


def _in_proj(x, pos, gain, lower_bound, w_packed, row_tile):
    bsz, t, d = x.shape
    n = bsz * t
    tm = row_tile
    assert n % tm == 0 and (t % tm == 0 or tm % t == 0)
    tables = _rope_tables(pos, ROT_DIM, HEAD_DIM) + _rope_tables(pos, IDX_ROT_DIM, IDX_DIM)
    table_rows = max(t, tm)
    tables = [jnp.tile(tb, (table_rows // t, 1)) for tb in tables]
    n_table_blocks = table_rows // tm
    n_gate = N_BRANCHES * d
    w_width = w_packed.shape[1]
    n_tri = min(tm, CUMSUM_ROWS)
    assert n_tri % CHUNK == 0 and tm % n_tri == 0 and t % CHUNK == 0
    r = np.arange(n_tri)
    chunk_tri = jnp.asarray((r[:, None] // CHUNK == r[None, :] // CHUNK) & (r[None, :] <= r[:, None]), MXU_DTYPE)

    def rows(width):
        return pl.BlockSpec((tm, width), lambda i: (i, 0))

    table_spec = pl.BlockSpec((tm, LANES), lambda i: (i % n_table_blocks, 0))
    widths_dtypes = [
        (A_WIDTH, MXU_DTYPE), (KV_WIDTH, F32), (KV_WIDTH, F32), (KV_WIDTH, MXU_DTYPE), (KV_WIDTH, MXU_DTYPE),
        (IQ_WIDTH, MXU_DTYPE), (LANES, F32), (LANES, MXU_DTYPE), (LANES, MXU_DTYPE),
        (B_WIDTH, MXU_DTYPE), (B_WIDTH, MXU_DTYPE), (B_WIDTH, F32), (B_WIDTH, MXU_DTYPE), (B_WIDTH, MXU_DTYPE),
        (C_WIDTH, MXU_DTYPE), (n_gate, MXU_DTYPE), (IDX_DIM, F32)]
    outs = pl.pallas_call(
        _in_proj_kernel,
        grid=(n // tm,),
        in_specs=[rows(d), _resident((1, d)), _resident((1, B_WIDTH)), _resident((d, w_width)), _resident((n_tri, n_tri))]
        + [table_spec] * 6,
        out_specs=[rows(w) for w, _ in widths_dtypes],
        out_shape=[jax.ShapeDtypeStruct((n, w), dt) for w, dt in widths_dtypes],
        compiler_params=pltpu.CompilerParams(dimension_semantics=("arbitrary",), vmem_limit_bytes=VMEM_LIMIT_BYTES),
        name="in_proj",
    )(x.reshape(n, d), gain.reshape(1, d), lower_bound.reshape(1, B_WIDTH), w_packed, chunk_tri, *tables)
    return [o.reshape(bsz, t, o.shape[-1]) for o in outs]


def _pack_w_in(w, d):
    widths = (A_WIDTH, KV_WIDTH, KV_WIDTH, IQ_WIDTH, IDX_DIM, IDX_HEADS, B_WIDTH, B_WIDTH, B_WIDTH, B_WIDTH,
              C_WIDTH, N_BRANCHES * d)
    points = [int(s) for s in np.cumsum(widths)[:-1]]
    a_q, a_k, a_v, i_q, i_k, i_w, b_q, b_f, b_i, b_g, c_q, gates = jnp.split(w, points, axis=-1)
    ikw = jnp.concatenate([i_k, jnp.zeros((d, IW_LANE - IDX_DIM), w.dtype), i_w,
                           jnp.zeros((d, LANES - IW_LANE - IDX_HEADS), w.dtype)], axis=-1)
    return jnp.concatenate([a_q, a_k, a_v, i_q, ikw, b_q, b_f, b_i, b_g, c_q, gates], axis=-1).astype(MXU_DTYPE)


def _float_of_key(key):
    return pltpu.bitcast(jnp.where(key < 0, (key - 1) ^ 0x7FFFFFFF, key), F32)


def _lane_tiled(x, n_tiles):
    return x if n_tiles == 1 else jnp.concatenate([x] * n_tiles, axis=1)


COUNT_GROUPS = 8
COUNT_TIERS = ((8, 6), (4, 3))
TIE_ROWS = 128
SCORE_UNROLL = 4
ATT_UNROLL = 2


def _dsa_kernel(aq_ref, iq_ref, iw_ref, ika_ref, ikb_ref, k_ref, v_ref, ya_ref,
                key_sc, bias_sc, m_sc, l_sc, acc_sc, *, tq, tk, key_off, topk, q_sub):
    qi = pl.program_id(1)
    n_vis = key_off + (qi + 1) * tq
    n_kb = lax.div(n_vis + (tk - 1), tk)
    q_pos = qi * tq + lax.broadcasted_iota(I32, (1, tq), 1)
    limit = key_off + (lax.shift_right_logical(q_pos, CHUNK.bit_length() - 1) + 1) * CHUNK

    def for_each_block(body, unroll):
        def trip(j, carry):
            for i in range(unroll):
                body(unroll * j + i, i)
            return carry

        n_trips = lax.div(n_kb, unroll)
        lax.fori_loop(0, n_trips, trip, 0)
        done = unroll * n_trips
        size = unroll // 2
        while size >= 1:
            has = ((n_kb - done) & size) != 0

            @pl.when(has)
            def _(done=done, size=size):
                for i in range(size):
                    body(done + i, i)

            done = done + jnp.where(has, size, 0)
            size //= 2

    def score_block(kb, slot):
        k0 = pl.multiple_of(kb * tk, tk)
        ika = ika_ref[0, pl.ds(k0, tk), :]
        ikb = ikb_ref[0, pl.ds(k0, tk), :]
        acc = jnp.zeros((tk, tq), F32)
        for j in range(IQ_WIDTH // LANES):
            iqt = iq_ref[0, :, j * LANES:(j + 1) * LANES]
            acc = acc + jnp.maximum(_dot_nt(ika, iqt), 0.0) * iw_ref[0, 2 * j:2 * j + 1, :]
            acc = acc + jnp.maximum(_dot_nt(ikb, iqt), 0.0) * iw_ref[0, 2 * j + 1:2 * j + 2, :]
        key_pos = k0 + lax.broadcasted_iota(I32, (tk, tq), 0)
        key_sc[kb] = jnp.where(key_pos < limit, acc, -jnp.inf)

    for_each_block(score_block, SCORE_UNROLL)

    ones_rows = jnp.ones((2 * SUBLANES, tk), MXU_DTYPE)

    def count(pred):
        step = SUBLANES * COUNT_GROUPS

        def hits(kb):
            return _dot(ones_rows, jnp.where(pred(key_sc[kb]), 1.0, 0.0).astype(MXU_DTYPE))

        def one_block(kb, acc):
            for r0 in range(0, tk, step):
                acc = acc + jnp.where(pred(key_sc[kb, r0:r0 + step, :]), 1, 0)
            return acc

        def group_body(first, size, n_matmul):
            def group(j, carry):
                by_matmul, by_adds = carry
                kb = first + size * j
                parts = [hits(kb + i) for i in range(n_matmul)]
                while len(parts) > 1:
                    parts = [a + b for a, b in zip(parts[0::2], parts[1::2])] + parts[len(parts) & ~1:]
                for i in range(n_matmul, size):
                    by_adds = one_block(kb + i, by_adds)
                return by_matmul + parts[0], by_adds
            return group

        carry = (jnp.zeros((2 * SUBLANES, tq), F32), jnp.zeros((step, tq), I32))
        first = 0
        for size, n_matmul in COUNT_TIERS:
            n_groups = lax.div(n_kb - first, size)
            carry = lax.fori_loop(0, n_groups, group_body(first, size, n_matmul), carry)
            first = first + size * n_groups
        by_matmul, by_adds = carry
        by_adds = lax.fori_loop(first, n_kb, one_block, by_adds)
        return by_matmul[0:1, :].astype(I32) + jnp.sum(by_adds, axis=0, keepdims=True)

    def key_bit(it, u):
        trial = u | jnp.left_shift(jnp.int32(1), 31 - it)
        cand = _float_of_key(trial ^ INT_MIN)
        return jnp.where(count(lambda s: s >= cand) >= topk, trial, u)

    u = lax.fori_loop(0, 32, key_bit, jnp.zeros((1, tq), I32))
    thr = _float_of_key(u ^ INT_MIN)
    n_gt = count(lambda s: s > thr)
    n_ge = count(lambda s: s >= thr)
    has_tie = jnp.where((n_ge > topk) & (thr > -jnp.inf), 1.0, 0.0)

    @pl.when(jnp.max(has_tie) > 0.0)
    def _():
        need = (topk - n_gt).astype(F32)
        ri = lax.broadcasted_iota(I32, (TIE_ROWS, TIE_ROWS), 0)
        ci = lax.broadcasted_iota(I32, (TIE_ROWS, TIE_ROWS), 1)
        lower = jnp.where(ci <= ri, 1.0, 0.0).astype(MXU_DTYPE)

        def demote(kb, seen):
            pieces = []
            for r0 in range(0, tk, TIE_ROWS):
                blk = key_sc[kb, r0:r0 + TIE_ROWS, :]
                tie = blk == thr
                pieces.append((r0, blk, tie, _dot(lower, jnp.where(tie, 1.0, 0.0).astype(MXU_DTYPE))))
            for r0, blk, tie, prefix in pieces:
                rank = prefix + seen
                key_sc[kb, r0:r0 + TIE_ROWS, :] = jnp.where(tie & (rank > need), -jnp.inf, blk)
                seen = rank[TIE_ROWS - 1:TIE_ROWS, :]
            return seen

        lax.fori_loop(0, n_kb, demote, jnp.zeros((1, tq), F32))

    thr_sel = jnp.maximum(thr, F32_LOWEST)

    m_sc[...] = jnp.full(m_sc.shape, NEG_BIG, F32)
    l_sc[...] = jnp.zeros(l_sc.shape, F32)
    acc_sc[...] = jnp.zeros(acc_sc.shape, F32)
    group = A_HEADS // A_KV_HEADS
    n_lane_tiles = tk // LANES
    ones_blk = jnp.ones((tk, LANES), MXU_DTYPE)

    def attend_block(kb, slot):
        k0 = pl.multiple_of(kb * tk, tk)
        bias_sc[slot] = jnp.where(key_sc[kb] >= thr_sel, 0.0, NEG_BIG).T
        for g in range(A_KV_HEADS):
            kblk = k_ref[0, pl.ds(k0, tk), g * LANES:(g + 1) * LANES]
            vaug = jnp.concatenate([v_ref[0, pl.ds(k0, tk), g * LANES:(g + 1) * LANES], ones_blk], axis=1)
            for r in range(group):
                h = g * group + r
                for rt in range(tq // q_sub):
                    rows = pl.ds(rt * q_sub, q_sub)
                    q = aq_ref[0, rows, h * LANES:(h + 1) * LANES]
                    s = _dot_nt(q, kblk) + bias_sc[slot, rows, :]
                    m_prev = m_sc[h, rows, :]
                    m_new = jnp.maximum(m_prev, jnp.max(s, axis=1, keepdims=True))
                    alpha = jnp.exp2(m_prev - m_new)
                    p = jnp.exp2(s - _lane_tiled(m_new, n_lane_tiles))
                    pv = _dot(p.astype(MXU_DTYPE), vaug)
                    acc_sc[h, rows, :] = alpha * acc_sc[h, rows, :] + pv[:, :LANES]
                    l_sc[h, rows, :] = alpha * l_sc[h, rows, :] + pv[:, LANES:]
                    m_sc[h, rows, :] = m_new
    for_each_block(attend_block, ATT_UNROLL)
    for h in range(A_HEADS):
        ya_ref[0, :, h * LANES:(h + 1) * LANES] = (acc_sc[h] / l_sc[h]).astype(ya_ref.dtype)


def _dsa(aq, iq, iw_t, ika, ikb, k16, v16, *, key_off, topk, q_tile, key_tile):
    bsz, t, _ = aq.shape
    lp = k16.shape[1]
    tq, tk = q_tile, key_tile
    assert lp % tk == 0
    n_kb_all = lp // tk
    assert t % tq == 0 and tq % LANES == 0 and tk % (SUBLANES * COUNT_GROUPS) == 0 and key_off + t <= lp
    assert tk >= topk
    kernel = functools.partial(_dsa_kernel, tq=tq, tk=tk, key_off=key_off, topk=topk, q_sub=min(tq, ATT_ROWS))

    def q_spec(width):
        return pl.BlockSpec((1, tq, width), lambda b, i: (b, i, 0))

    def key_spec(width):
        return pl.BlockSpec((1, lp, width), lambda b, i: (b, 0, 0), pipeline_mode=pl.Buffered(1))

    return pl.pallas_call(
        kernel,
        grid=(bsz, t // tq),
        in_specs=[q_spec(A_WIDTH), q_spec(IQ_WIDTH), pl.BlockSpec((1, IDX_HEADS, tq), lambda b, i: (b, 0, i)),
                  key_spec(LANES), key_spec(LANES), key_spec(KV_WIDTH), key_spec(KV_WIDTH)],
        out_specs=q_spec(A_WIDTH),
        out_shape=jax.ShapeDtypeStruct((bsz, t, A_WIDTH), MXU_DTYPE),
        scratch_shapes=[pltpu.VMEM((n_kb_all, tk, tq), F32), pltpu.VMEM((ATT_UNROLL, tq, tk), F32),
                        pltpu.VMEM((A_HEADS, tq, LANES), F32), pltpu.VMEM((A_HEADS, tq, LANES), F32),
                        pltpu.VMEM((A_HEADS, tq, LANES), F32)],
        compiler_params=pltpu.CompilerParams(dimension_semantics=("arbitrary", "arbitrary"),
                                             vmem_limit_bytes=VMEM_LIMIT_BYTES),
        name="dsa",
    )(aq, iq, iw_t, ika, ikb, k16, v16)


SUB = SUBLANES
HGRN_ROWS = 4
HGRN_GROUP_HEADS = 2
N_SUB = CHUNK // SUB
INTER_ROWS = SUB * (N_SUB * (N_SUB - 1) // 2)


def _split3(x):
    hi = x.astype(MXU_DTYPE)
    r1 = x - hi.astype(F32)
    mid = r1.astype(MXU_DTYPE)
    lo = (r1 - mid.astype(F32)).astype(MXU_DTYPE)
    return hi, mid, lo


def _hgrn_constants():
    t = np.arange(CHUNK)
    place = ((np.arange(SUB * LANES)[:, None] // LANES) == (t[None, :] % SUB)).astype(np.float32)
    intra = ((t[:, None] // SUB == t[None, :] // SUB) & (t[None, :] <= t[:, None])).astype(np.float32)
    seg = np.concatenate([np.full(i * SUB, i) for i in range(1, N_SUB)])
    inter = ((t[:, None] // SUB) == seg[None, :]).astype(np.float32)
    return jnp.asarray(place, MXU_DTYPE), jnp.asarray(intra, F32), jnp.asarray(inter, F32)


def _hgrn_kernel(q_ref, k_ref, b_ref, v_ref, g_ref, gain_ref, s0_ref, place_ref, intra_ref, inter_ref,
                 yb_ref, sout_ref, st_sc):
    c = pl.program_id(1)
    n_rows = q_ref.shape[0]

    @pl.when(c == 0)
    def _():
        for i in range(n_rows):
            for h in range(B_HEADS):
                st_sc[i, h] = s0_ref[i, h].T

    def row_bcast(x, r, n):
        return jnp.broadcast_to(x[r:r + 1, :], (n, x.shape[1]))

    place = place_ref[...]

    def head_group(i, g):
        width = HGRN_GROUP_HEADS * LANES
        lanes = slice(g * width, (g + 1) * width)
        q = q_ref[i, :, lanes].astype(F32)
        k = k_ref[i, :, lanes].astype(F32)
        v16 = v_ref[i, :, lanes]
        b = b_ref[i, :, lanes]

        b_start = jnp.concatenate(
            [jnp.zeros((SUB, width), F32)] + [row_bcast(b, j * SUB - 1, SUB) for j in range(1, N_SUB)], axis=0)
        q_hat = (q * jnp.exp(b - b_start)).astype(MXU_DTYPE)
        q_state = (q * jnp.exp(b)).astype(MXU_DTYPE)
        b_end = b[CHUNK - 1:CHUNK, :]
        k_end = (k * jnp.exp(b_end - b)).astype(MXU_DTYPE)

        k_hat = jnp.concatenate(
            [k[:j * SUB] * jnp.exp(row_bcast(b, j * SUB - 1, j * SUB) - b[:j * SUB]) for j in range(1, N_SUB)],
            axis=0).astype(MXU_DTYPE)
        v_hat = jnp.concatenate([v16[:j * SUB] for j in range(1, N_SUB)], axis=0)

        q3 = q.reshape(N_SUB, SUB, width)
        k3 = k.reshape(N_SUB, SUB, width)
        b3 = b.reshape(N_SUB, SUB, width)
        intra = []
        for s in range(SUB):
            kp = jnp.broadcast_to(k3[:, s:s + 1, :], q3.shape)
            bp = jnp.broadcast_to(b3[:, s:s + 1, :], q3.shape)
            x = q3 * kp * jnp.exp(jnp.minimum(b3 - bp, 0.0))
            intra.append(x.reshape(CHUNK, width).astype(MXU_DTYPE))

        for hh in range(HGRN_GROUP_HEADS):
            h = g * HGRN_GROUP_HEADS + hh
            sl = slice(hh * LANES, (hh + 1) * LANES)
            out_lanes = slice(h * LANES, (h + 1) * LANES)
            st = st_sc[i, h]
            x_cat = jnp.concatenate([x[:, sl] for x in intra], axis=1)
            p_intra = (_dot(x_cat, place) * intra_ref[...]).astype(MXU_DTYPE)
            p_inter = (_dot_nt(q_hat[:, sl], k_hat[:, sl]) * inter_ref[...]).astype(MXU_DTYPE)
            o = (_dot(p_inter, v_hat[:, sl]) + _dot(p_intra, v16[:, sl])
                 + _dot_nt(q_state[:, sl], st.astype(MXU_DTYPE)))
            st_sc[i, h] = st * jnp.exp(b_end[:, sl]) + _dot(v16[:, sl].astype(F32).T.astype(MXU_DTYPE), k_end[:, sl])
            y = _rms(o, gain_ref[:, out_lanes]) * g_ref[i, :, out_lanes].astype(F32)
            yb_ref[i, :, out_lanes] = y.astype(yb_ref.dtype)

    for i in range(n_rows):
        for g in range(B_HEADS // HGRN_GROUP_HEADS):
            head_group(i, g)

    @pl.when(c == pl.num_programs(1) - 1)
    def _():
        for i in range(n_rows):
            for h in range(B_HEADS):
                sout_ref[i, h] = st_sc[i, h].T


def _hgrn(hq, hk, hlf, hv, hg, norm_gain, state0):
    bsz, t, _ = hq.shape
    assert t % CHUNK == 0 and bsz % HGRN_ROWS == 0
    nb = HGRN_ROWS
    consts = _hgrn_constants()
    seq_spec = pl.BlockSpec((nb, CHUNK, B_WIDTH), lambda b, c: (b, c, 0))
    state_spec = pl.BlockSpec((nb, B_HEADS, B_KEY_DIM, B_VAL_DIM), lambda b, c: (b, 0, 0, 0))
    return pl.pallas_call(
        _hgrn_kernel,
        grid=(bsz // nb, t // CHUNK),
        in_specs=[seq_spec] * 5 + [_resident((1, B_WIDTH)), state_spec] + [_resident(x.shape) for x in consts],
        out_specs=[seq_spec, state_spec],
        out_shape=[jax.ShapeDtypeStruct((bsz, t, B_WIDTH), MXU_DTYPE),
                   jax.ShapeDtypeStruct((bsz, B_HEADS, B_KEY_DIM, B_VAL_DIM), F32)],
        scratch_shapes=[pltpu.VMEM((nb, B_HEADS, B_VAL_DIM, B_KEY_DIM), F32)],
        compiler_params=pltpu.CompilerParams(dimension_semantics=("arbitrary", "arbitrary"),
                                             vmem_limit_bytes=VMEM_LIMIT_BYTES),
        name="hgrn",
    )(hq, hk, hlf, hv, hg, norm_gain.reshape(1, B_WIDTH), state0, *consts)


def _memory_kv_kernel(mem_ref, gain_ref, w_ref, k32_ref, v32_ref, k16_ref, v16_ref):
    hb = _rms(mem_ref[0], gain_ref[...]).astype(MXU_DTYPE)
    mk = _dot(hb, w_ref[:, :C_WIDTH])
    mv = _dot(hb, w_ref[:, C_WIDTH:])
    k32_ref[0] = mk
    v32_ref[0] = mv
    k16_ref[0] = mk.astype(MXU_DTYPE)
    v16_ref[0] = mv.astype(MXU_DTYPE)


def _memory_kv(mem, gain, w16):
    bsz, n, d = mem.shape
    spec = pl.BlockSpec((1, n, C_WIDTH), lambda b: (b, 0, 0))
    return pl.pallas_call(
        _memory_kv_kernel,
        grid=(bsz,),
        in_specs=[pl.BlockSpec((1, n, d), lambda b: (b, 0, 0)), _resident((1, d)), _resident((d, 2 * C_WIDTH))],
        out_specs=[spec] * 4,
        out_shape=[jax.ShapeDtypeStruct((bsz, n, C_WIDTH), dt) for dt in (F32, F32, MXU_DTYPE, MXU_DTYPE)],
        compiler_params=pltpu.CompilerParams(dimension_semantics=("arbitrary",), vmem_limit_bytes=VMEM_LIMIT_BYTES),
        name="memory_kv",
    )(mem, gain.reshape(1, d), w16)


def _merge_ffn_kernel(x_ref, ya_ref, yb_ref, cq_ref, mk_ref, mv_ref, gt_ref, wa_ref, wb_ref, wc_ref, wo_ref,
                      g_mix_ref, g_pre_ref, g_post_ref, wu_ref, wd_ref, out_ref):
    nb, tm, d = x_ref.shape
    rows = nb * tm

    def flat(ref):
        return ref[...].reshape(rows, ref.shape[-1])

    per_row = []
    for i in range(nb):
        heads = []
        for h in range(C_HEADS):
            sl = slice(h * C_HEAD_DIM, (h + 1) * C_HEAD_DIM)
            s = _dot_nt(cq_ref[i, :, sl], mk_ref[i, :, sl])
            p = jnp.exp(s - jnp.max(s, axis=-1, keepdims=True))
            p = p / jnp.sum(p, axis=-1, keepdims=True)
            heads.append(_dot(p.astype(MXU_DTYPE), mv_ref[i, :, sl]).astype(MXU_DTYPE))
        per_row.append(jnp.concatenate(heads, axis=1))
    y_c = per_row[0] if nb == 1 else jnp.concatenate(per_row, axis=0)

    gt = flat(gt_ref)
    merged = (gt[:, 0:d].astype(F32) * _dot(flat(ya_ref), wa_ref[...])
              + gt[:, d:2 * d].astype(F32) * _dot(flat(yb_ref), wb_ref[...])
              + gt[:, 2 * d:3 * d].astype(F32) * _dot(y_c, wc_ref[...]))
    x1 = flat(x_ref) + _rms(_dot(merged.astype(MXU_DTYPE), wo_ref[...]), g_mix_ref[...])
    u = jnp.maximum(_dot(_rms(x1, g_pre_ref[...]).astype(MXU_DTYPE), wu_ref[...]), 0.0)
    down = _dot((u * u).astype(MXU_DTYPE), wd_ref[...])
    out_ref[...] = (x1 + _rms(down, g_post_ref[...])).reshape(nb, tm, d)


def _merge_ffn(x, ya, yb, cq, mk16, mv16, gates, weights, gains, row_tile):
    bsz, t, d = x.shape
    tm = min(row_tile, t)
    nb = max(1, min(bsz, row_tile // t))
    assert t % tm == 0 and bsz % nb == 0
    n_mem = mk16.shape[1]

    def rows(width):
        return pl.BlockSpec((nb, tm, width), lambda b, i: (b, i, 0))

    mem_spec = pl.BlockSpec((nb, n_mem, C_WIDTH), lambda b, i: (b, 0, 0))
    return pl.pallas_call(
        _merge_ffn_kernel,
        grid=(bsz // nb, t // tm),
        in_specs=[rows(d), rows(A_WIDTH), rows(B_WIDTH), rows(C_WIDTH), mem_spec, mem_spec, rows(N_BRANCHES * d)]
        + [_resident(w.shape) for w in weights[:4]] + [_resident((1, d))] * 3
        + [_resident(w.shape) for w in weights[4:]],
        out_specs=rows(d),
        out_shape=jax.ShapeDtypeStruct((bsz, t, d), F32),
        compiler_params=pltpu.CompilerParams(dimension_semantics=("arbitrary", "arbitrary"),
                                             vmem_limit_bytes=VMEM_LIMIT_BYTES),
        name="merge_ffn",
    )(x, ya, yb, cq, mk16, mv16, gates, *weights[:4], *[g.reshape(1, d) for g in gains], *weights[4:])


def _layer(x, pos, lower_bound, lp, *, cache, state0, mem16, row_tile, q_tile, key_tile):
    bsz, t, d = x.shape
    (aq, k32, v32, k16, v16, iq, ikw, ika, ikb, hq, hk, hlf, hv, hg, cq, gates, ik32) = _in_proj(
        x, pos, lp["pre_mix_gain"], lower_bound, lp["w_in"], min(row_tile, bsz * t))

    keys = [ika, ikb, k16, v16]
    if cache is not None:
        keys = [jnp.concatenate([c, kn], axis=1) for c, kn in zip(cache, keys)]
    n_keys = keys[0].shape[1]
    t_pad = -(-t // q_tile) * q_tile
    key_off = n_keys - t
    n_pad = -(-(key_off + t_pad) // key_tile) * key_tile
    ika, ikb, k16, v16 = [jnp.pad(kk, ((0, 0), (0, n_pad - n_keys), (0, 0))) for kk in keys]
    iw_t = jnp.swapaxes(ikw[..., IW_LANE:IW_LANE + IDX_HEADS], 1, 2)
    q_pad = ((0, 0), (0, t_pad - t), (0, 0))
    ya = _dsa(jnp.pad(aq, q_pad), jnp.pad(iq, q_pad), jnp.pad(iw_t, ((0, 0), (0, 0), (0, t_pad - t))),
              ika, ikb, k16, v16, key_off=key_off, topk=min(TOPK_MAX, n_keys // 4), q_tile=q_tile,
              key_tile=key_tile)[:, :t]

    yb, s_new = _hgrn(hq, hk, hlf, hv, hg, lp["hgrn_norm_gain"], state0)

    out = _merge_ffn(x, ya, yb, cq, mem16[0], mem16[1], gates,
                     [lp["w_out_a"], lp["w_out_b"], lp["w_out_c"], lp["w_out"], lp["w_up"], lp["w_down"]],
                     [lp["post_mix_gain"], lp["pre_ffn_gain"], lp["post_ffn_gain"]], row_tile=MERGE_ROWS)
    new_k = k32.reshape(bsz, t, A_KV_HEADS, HEAD_DIM)
    new_v = v32.reshape(bsz, t, A_KV_HEADS, HEAD_DIM)
    return out, (new_k, new_v, ik32, s_new)


PROJ_ROWS = 512
CUMSUM_ROWS = 256
MERGE_ROWS = 512
DSA_QUERIES = 512


def _tiles(t):
    return PROJ_ROWS, max(LANES, min(DSA_QUERIES, t))


def kernel(x_prompt, x_sample, cache_k, cache_v, cache_idx_k, state_hgrn, cache_mem_k, cache_mem_v, mem_prompt, w_in, w_mem_kv, mem_norm_gain, hgrn_lb_logits, hgrn_norm_gain, w_out_a, w_out_b, w_out_c, w_out, pre_mix_gain, post_mix_gain, pre_ffn_gain, post_ffn_gain, w_up, w_down):
    depth = w_in.shape[0]
    d = x_prompt.shape[-1]
    bp, tp, _ = x_prompt.shape
    bs, ts, _ = x_sample.shape
    past = cache_k.shape[2]
    lower_bounds = jnp.cumsum(jax.nn.softmax(hgrn_lb_logits.astype(F32), axis=0), axis=0)
    pos_p = jnp.arange(tp, dtype=I32)
    pos_s = past + jnp.arange(ts, dtype=I32)
    bf = lambda a: a.astype(MXU_DTYPE)

    xp, xs = x_prompt, x_sample
    outs_p, outs_s = [], []
    for l in range(depth):
        lp = {
            "w_in": _pack_w_in(w_in[l], d), "hgrn_norm_gain": hgrn_norm_gain[l],
            "w_out_a": bf(w_out_a[l]), "w_out_b": bf(w_out_b[l]), "w_out_c": bf(w_out_c[l]), "w_out": bf(w_out[l]),
            "pre_mix_gain": pre_mix_gain[l], "post_mix_gain": post_mix_gain[l], "pre_ffn_gain": pre_ffn_gain[l],
            "post_ffn_gain": post_ffn_gain[l], "w_up": bf(w_up[l]), "w_down": bf(w_down[l]),
        }
        mk32, mv32, mk16, mv16 = _memory_kv(mem_prompt, mem_norm_gain[l], bf(w_mem_kv[l]))
        n_mem = mem_prompt.shape[1]
        row_tile, q_tile = _tiles(tp)
        xp, st_p = _layer(xp, pos_p, lower_bounds[l], lp, cache=None,
                          state0=jnp.zeros((bp, B_HEADS, B_KEY_DIM, B_VAL_DIM), F32), mem16=(mk16, mv16),
                          row_tile=row_tile, q_tile=q_tile, key_tile=512)
        outs_p.append(st_p + (mk32.reshape(bp, n_mem, C_HEADS, C_HEAD_DIM), mv32.reshape(bp, n_mem, C_HEADS, C_HEAD_DIM)))

        cik = cache_idx_k[l]
        zeros_ik = jnp.zeros_like(cik)
        cache = (bf(jnp.concatenate([cik, zeros_ik], axis=-1)), bf(jnp.concatenate([zeros_ik, cik], axis=-1)),
                 bf(cache_k[l].reshape(bs, past, KV_WIDTH)), bf(cache_v[l].reshape(bs, past, KV_WIDTH)))
        mem16 = (bf(cache_mem_k[l].reshape(bs, -1, C_WIDTH)), bf(cache_mem_v[l].reshape(bs, -1, C_WIDTH)))
        row_tile, q_tile = _tiles(ts)
        xs, st_s = _layer(xs, pos_s, lower_bounds[l], lp, cache=cache, state0=state_hgrn[l].astype(F32),
                          mem16=mem16, row_tile=row_tile, q_tile=q_tile, key_tile=384)
        outs_s.append(st_s)

    new_k_p, new_v_p, new_ik_p, new_s_p, new_mk_p, new_mv_p = [jnp.stack(a) for a in zip(*outs_p)]
    new_k_s, new_v_s, new_ik_s, new_s_s = [jnp.stack(a) for a in zip(*outs_s)]
    return (xp, xs, new_k_p, new_v_p, new_ik_p, new_s_p, new_mk_p, new_mv_p, new_k_s, new_v_s, new_ik_s, new_s_s)
```

```python
import functools

import jax
import jax.numpy as jnp
import numpy as np
from jax import lax
from jax.experimental import pallas as pl
from jax.experimental.pallas import tpu as pltpu

CHUNK = 64
N_BRANCHES = 3
A_HEADS, A_KV_HEADS, HEAD_DIM = 6, 2, 128
ROT_DIM = HEAD_DIM // 4
IDX_HEADS, IDX_DIM = 8, 64
IDX_ROT_DIM = IDX_DIM // 4
TOPK_MAX = 256
B_HEADS, B_KEY_DIM, B_VAL_DIM = 6, 128, 128
C_HEADS, C_HEAD_DIM = 4, 128
ROPE_THETA = 500000.0
EPS = 1e-6

A_WIDTH = A_HEADS * HEAD_DIM
KV_WIDTH = A_KV_HEADS * HEAD_DIM
IQ_WIDTH = IDX_HEADS * IDX_DIM
B_WIDTH = B_HEADS * B_KEY_DIM
C_WIDTH = C_HEADS * C_HEAD_DIM

LANES = 128
SUBLANES = 8
VMEM_LIMIT_BYTES = 56 * 1024 * 1024

MXU_DTYPE = jnp.bfloat16
F32 = jnp.float32
I32 = jnp.int32

IW_LANE = 96
NEG_BIG = -1e30
LOG2E = 1.4426950408889634
ATT_ROWS = 128
INT_MIN = -(2 ** 31)
F32_LOWEST = -3.4028234663852886e38

OFF_AQ = 0
OFF_AK = OFF_AQ + A_WIDTH
OFF_AV = OFF_AK + KV_WIDTH
OFF_IQ = OFF_AV + KV_WIDTH
OFF_IKW = OFF_IQ + IQ_WIDTH
OFF_BQ = OFF_IKW + LANES
OFF_BF = OFF_BQ + B_WIDTH
OFF_BI = OFF_BF + B_WIDTH
OFF_BG = OFF_BI + B_WIDTH
OFF_CQ = OFF_BG + B_WIDTH
OFF_GATES = OFF_CQ + C_WIDTH


def _sigmoid(x):
    return 1.0 / (1.0 + jnp.exp(-x))


def _dot(a, b):
    return jnp.dot(a, b, preferred_element_type=F32)


def _dot_nt(a, b):
    return lax.dot_general(a, b, (((1,), (1,)), ((), ())), preferred_element_type=F32)


def _rms(x, gain):
    return x * lax.rsqrt(jnp.mean(x * x, axis=-1, keepdims=True) + EPS) * gain


def _resident(shape):
    zeros = (0,) * len(shape)
    return pl.BlockSpec(shape, lambda *_: zeros, pipeline_mode=pl.Buffered(1))


def _rope(y, cos, sin_up, sin_down, half):
    return y * cos + pltpu.roll(y, half, 1) * sin_up + pltpu.roll(y, LANES - half, 1) * sin_down


def _in_proj_kernel(x_ref, gain_ref, lb_ref, w_ref, tri_ref, ca_ref, sau_ref, sad_ref, ci_ref, siu_ref, sid_ref,
                    aq_ref, k32_ref, v32_ref, k16_ref, v16_ref, iq_ref, ikw_ref, ika_ref, ikb_ref,
                    hq_ref, hk_ref, hlf_ref, hv_ref, hg_ref, cq_ref, gt_ref, ik32_ref):
    hb = _rms(x_ref[...], gain_ref[...]).astype(MXU_DTYPE)

    def proj(c0, width):
        return _dot(hb, w_ref[:, c0:c0 + width])

    ca, sau, sad = ca_ref[...], sau_ref[...], sad_ref[...]
    ci, siu, sid = ci_ref[...], siu_ref[...], sid_ref[...]

    y = proj(OFF_AQ, A_WIDTH)
    for h in range(A_HEADS):
        sl = slice(h * LANES, (h + 1) * LANES)
        aq_ref[:, sl] = (_rope(y[:, sl], ca, sau, sad, ROT_DIM // 2) * ((HEAD_DIM ** -0.5) * LOG2E)).astype(MXU_DTYPE)

    y = proj(OFF_AK, KV_WIDTH)
    for h in range(A_KV_HEADS):
        sl = slice(h * LANES, (h + 1) * LANES)
        r = _rope(y[:, sl], ca, sau, sad, ROT_DIM // 2)
        k32_ref[:, sl] = r
        k16_ref[:, sl] = r.astype(MXU_DTYPE)

    y = proj(OFF_AV, KV_WIDTH)
    v32_ref[...] = y
    v16_ref[...] = y.astype(MXU_DTYPE)

    y = proj(OFF_IQ, IQ_WIDTH)
    for j in range(IQ_WIDTH // LANES):
        sl = slice(j * LANES, (j + 1) * LANES)
        iq_ref[:, sl] = _rope(y[:, sl], ci, siu, sid, IDX_ROT_DIM // 2).astype(MXU_DTYPE)

    r = _rope(proj(OFF_IKW, LANES), ci, siu, sid, IDX_ROT_DIM // 2)
    lane = lax.broadcasted_iota(I32, r.shape, 1)
    ikw_ref[...] = jnp.where(lane < IDX_DIM, r, r * ((IDX_DIM ** -0.5) * (IDX_HEADS ** -0.5)))
    ik32_ref[...] = r[:, :IDX_DIM]
    ik_low = jnp.where(lane < IDX_DIM, r, 0.0)
    ika_ref[...] = ik_low.astype(MXU_DTYPE)
    ikb_ref[...] = pltpu.roll(ik_low, IDX_DIM, 1).astype(MXU_DTYPE)

    y = proj(OFF_BQ, B_WIDTH)
    hq_ref[...] = (y * _sigmoid(y)).astype(MXU_DTYPE)
    lb = lb_ref[...]
    f = lb + (1.0 - lb) * _sigmoid(proj(OFF_BF, B_WIDTH))
    hk_ref[...] = (1.0 - f).astype(MXU_DTYPE)
    hi, mid, lo = _split3(jnp.log(f))
    tri = tri_ref[...]
    n_tri = tri.shape[0]
    for r0 in range(0, hlf_ref.shape[0], n_tri):
        rs = slice(r0, r0 + n_tri)
        hlf_ref[rs, :] = _dot(tri, hi[rs]) + _dot(tri, mid[rs]) + _dot(tri, lo[rs])
    hv_ref[...] = proj(OFF_BI, B_WIDTH).astype(MXU_DTYPE)
    y = proj(OFF_BG, B_WIDTH)
    hg_ref[...] = (y * _sigmoid(y)).astype(MXU_DTYPE)

    cq_ref[...] = (proj(OFF_CQ, C_WIDTH) * (C_HEAD_DIM ** -0.5)).astype(MXU_DTYPE)

    n_gate = gt_ref.shape[1]
    for c0 in range(0, n_gate, B_WIDTH):
        gt_ref[:, c0:c0 + B_WIDTH] = _sigmoid(proj(OFF_GATES + c0, B_WIDTH)).astype(MXU_DTYPE)


def _rope_tables(pos, rot_dim, head_dim):
    half = rot_dim // 2
    inv_freq = ROPE_THETA ** (-jnp.arange(half, dtype=F32) / half)
    ang = pos.astype(F32)[:, None] * inv_freq[None, :]
    cos, sin = jnp.cos(ang), jnp.sin(ang)
    lane = np.arange(LANES) % head_dim
    idx = lane % half
    first, second = lane < half, (lane >= half) & (lane < rot_dim)
    cos_t = jnp.where(first | second, cos[:, idx], 1.0)
    sin_up = jnp.where(second, sin[:, idx], 0.0)
    sin_down = jnp.where(first, -sin[:, idx], 0.0)
    return cos_t, sin_up, sin_down


def _in_proj(x, pos, gain, lower_bound, w_packed, row_tile):
    bsz, t, d = x.shape
    n = bsz * t
    tm = row_tile
    assert n % tm == 0 and (t % tm == 0 or tm % t == 0)
    tables = _rope_tables(pos, ROT_DIM, HEAD_DIM) + _rope_tables(pos, IDX_ROT_DIM, IDX_DIM)
    table_rows = max(t, tm)
    tables = [jnp.tile(tb, (table_rows // t, 1)) for tb in tables]
    n_table_blocks = table_rows // tm
    n_gate = N_BRANCHES * d
    w_width = w_packed.shape[1]
    n_tri = min(tm, CUMSUM_ROWS)
    assert n_tri % CHUNK == 0 and tm % n_tri == 0 and t % CHUNK == 0
    r = np.arange(n_tri)
    chunk_tri = jnp.asarray((r[:, None] // CHUNK == r[None, :] // CHUNK) & (r[None, :] <= r[:, None]), MXU_DTYPE)

    def rows(width):
        return pl.BlockSpec((tm, width), lambda i: (i, 0))

    table_spec = pl.BlockSpec((tm, LANES), lambda i: (i % n_table_blocks, 0))
    widths_dtypes = [
        (A_WIDTH, MXU_DTYPE), (KV_WIDTH, F32), (KV_WIDTH, F32), (KV_WIDTH, MXU_DTYPE), (KV_WIDTH, MXU_DTYPE),
        (IQ_WIDTH, MXU_DTYPE), (LANES, F32), (LANES, MXU_DTYPE), (LANES, MXU_DTYPE),
        (B_WIDTH, MXU_DTYPE), (B_WIDTH, MXU_DTYPE), (B_WIDTH, F32), (B_WIDTH, MXU_DTYPE), (B_WIDTH, MXU_DTYPE),
        (C_WIDTH, MXU_DTYPE), (n_gate, MXU_DTYPE), (IDX_DIM, F32)]
    outs = pl.pallas_call(
        _in_proj_kernel,
        grid=(n // tm,),
        in_specs=[rows(d), _resident((1, d)), _resident((1, B_WIDTH)), _resident((d, w_width)), _resident((n_tri, n_tri))]
        + [table_spec] * 6,
        out_specs=[rows(w) for w, _ in widths_dtypes],
        out_shape=[jax.ShapeDtypeStruct((n, w), dt) for w, dt in widths_dtypes],
        compiler_params=pltpu.CompilerParams(dimension_semantics=("arbitrary",), vmem_limit_bytes=VMEM_LIMIT_BYTES),
        name="in_proj",
    )(x.reshape(n, d), gain.reshape(1, d), lower_bound.reshape(1, B_WIDTH), w_packed, chunk_tri, *tables)
    return [o.reshape(bsz, t, o.shape[-1]) for o in outs]


def _pack_w_in(w, d):
    widths = (A_WIDTH, KV_WIDTH, KV_WIDTH, IQ_WIDTH, IDX_DIM, IDX_HEADS, B_WIDTH, B_WIDTH, B_WIDTH, B_WIDTH,
              C_WIDTH, N_BRANCHES * d)
    points = [int(s) for s in np.cumsum(widths)[:-1]]
    a_q, a_k, a_v, i_q, i_k, i_w, b_q, b_f, b_i, b_g, c_q, gates = jnp.split(w, points, axis=-1)
    ikw = jnp.concatenate([i_k, jnp.zeros((d, IW_LANE - IDX_DIM), w.dtype), i_w,
                           jnp.zeros((d, LANES - IW_LANE - IDX_HEADS), w.dtype)], axis=-1)
    return jnp.concatenate([a_q, a_k, a_v, i_q, ikw, b_q, b_f, b_i, b_g, c_q, gates], axis=-1).astype(MXU_DTYPE)


def _float_of_key(key):
    return pltpu.bitcast(jnp.where(key < 0, (key - 1) ^ 0x7FFFFFFF, key), F32)


def _lane_tiled(x, n_tiles):
    return x if n_tiles == 1 else jnp.concatenate([x] * n_tiles, axis=1)


COUNT_GROUPS = 8
COUNT_TIERS = ((8, 6), (4, 3))
TIE_ROWS = 128
SCORE_UNROLL = 4
ATT_UNROLL = 4


def _dsa_kernel(aq_ref, iq_ref, iw_ref, ika_ref, ikb_ref, k_ref, v_ref, ya_ref,
                key_sc, bias_sc, m_sc, l_sc, acc_sc, *, tq, tk, key_off, topk, q_sub):
    qi = pl.program_id(1)
    n_vis = key_off + (qi + 1) * tq
    n_kb = lax.div(n_vis + (tk - 1), tk)
    q_pos = qi * tq + lax.broadcasted_iota(I32, (1, tq), 1)
    limit = key_off + (lax.shift_right_logical(q_pos, CHUNK.bit_length() - 1) + 1) * CHUNK

    def for_each_block(body, unroll):
        def trip(j, carry):
            for i in range(unroll):
                body(unroll * j + i, i)
            return carry

        n_trips = lax.div(n_kb, unroll)
        lax.fori_loop(0, n_trips, trip, 0)
        done = unroll * n_trips
        size = unroll // 2
        while size >= 1:
            has = ((n_kb - done) & size) != 0

            @pl.when(has)
            def _(done=done, size=size):
                for i in range(size):
                    body(done + i, i)

            done = done + jnp.where(has, size, 0)
            size //= 2

    def score_block(kb, slot):
        k0 = pl.multiple_of(kb * tk, tk)
        ika = ika_ref[0, pl.ds(k0, tk), :]
        ikb = ikb_ref[0, pl.ds(k0, tk), :]
        acc = jnp.zeros((tk, tq), F32)
        for j in range(IQ_WIDTH // LANES):
            iqt = iq_ref[0, :, j * LANES:(j + 1) * LANES]
            acc = acc + jnp.maximum(_dot_nt(ika, iqt), 0.0) * iw_ref[0, 2 * j:2 * j + 1, :]
            acc = acc + jnp.maximum(_dot_nt(ikb, iqt), 0.0) * iw_ref[0, 2 * j + 1:2 * j + 2, :]
        key_pos = k0 + lax.broadcasted_iota(I32, (tk, tq), 0)
        key_sc[kb] = jnp.where(key_pos < limit, acc, -jnp.inf)

    for_each_block(score_block, SCORE_UNROLL)

    ones_rows = jnp.ones((2 * SUBLANES, tk), MXU_DTYPE)

    def count(pred):
        step = SUBLANES * COUNT_GROUPS

        def hits(kb):
            return _dot(ones_rows, jnp.where(pred(key_sc[kb]), 1.0, 0.0).astype(MXU_DTYPE))

        def one_block(kb, acc):
            for r0 in range(0, tk, step):
                acc = acc + jnp.where(pred(key_sc[kb, r0:r0 + step, :]), 1, 0)
            return acc

        def group_body(first, size, n_matmul):
            def group(j, carry):
                by_matmul, by_adds = carry
                kb = first + size * j
                parts = [hits(kb + i) for i in range(n_matmul)]
                while len(parts) > 1:
                    parts = [a + b for a, b in zip(parts[0::2], parts[1::2])] + parts[len(parts) & ~1:]
                for i in range(n_matmul, size):
                    by_adds = one_block(kb + i, by_adds)
                return by_matmul + parts[0], by_adds
            return group

        carry = (jnp.zeros((2 * SUBLANES, tq), F32), jnp.zeros((step, tq), I32))
        first = 0
        for size, n_matmul in COUNT_TIERS:
            n_groups = lax.div(n_kb - first, size)
            carry = lax.fori_loop(0, n_groups, group_body(first, size, n_matmul), carry)
            first = first + size * n_groups
        by_matmul, by_adds = carry
        by_adds = lax.fori_loop(first, n_kb, one_block, by_adds)
        return by_matmul[0:1, :].astype(I32) + jnp.sum(by_adds, axis=0, keepdims=True)

    def key_bit(it, u):
        trial = u | jnp.left_shift(jnp.int32(1), 31 - it)
        cand = _float_of_key(trial ^ INT_MIN)
        return jnp.where(count(lambda s: s >= cand) >= topk, trial, u)

    u = lax.fori_loop(0, 32, key_bit, jnp.zeros((1, tq), I32))
    thr = _float_of_key(u ^ INT_MIN)
    n_gt = count(lambda s: s > thr)
    n_ge = count(lambda s: s >= thr)
    has_tie = jnp.where((n_ge > topk) & (thr > -jnp.inf), 1.0, 0.0)

    @pl.when(jnp.max(has_tie) > 0.0)
    def _():
        need = (topk - n_gt).astype(F32)
        ri = lax.broadcasted_iota(I32, (TIE_ROWS, TIE_ROWS), 0)
        ci = lax.broadcasted_iota(I32, (TIE_ROWS, TIE_ROWS), 1)
        lower = jnp.where(ci <= ri, 1.0, 0.0).astype(MXU_DTYPE)

        def demote(kb, seen):
            pieces = []
            for r0 in range(0, tk, TIE_ROWS):
                blk = key_sc[kb, r0:r0 + TIE_ROWS, :]
                tie = blk == thr
                pieces.append((r0, blk, tie, _dot(lower, jnp.where(tie, 1.0, 0.0).astype(MXU_DTYPE))))
            for r0, blk, tie, prefix in pieces:
                rank = prefix + seen
                key_sc[kb, r0:r0 + TIE_ROWS, :] = jnp.where(tie & (rank > need), -jnp.inf, blk)
                seen = rank[TIE_ROWS - 1:TIE_ROWS, :]
            return seen

        lax.fori_loop(0, n_kb, demote, jnp.zeros((1, tq), F32))

    thr_sel = jnp.maximum(thr, F32_LOWEST)

    m_sc[...] = jnp.full(m_sc.shape, NEG_BIG, F32)
    l_sc[...] = jnp.zeros(l_sc.shape, F32)
    acc_sc[...] = jnp.zeros(acc_sc.shape, F32)
    group = A_HEADS // A_KV_HEADS
    n_lane_tiles = tk // LANES
    ones_blk = jnp.ones((tk, LANES), MXU_DTYPE)

    def attend_block(kb, slot):
        k0 = pl.multiple_of(kb * tk, tk)
        bias_sc[slot] = jnp.where(key_sc[kb] >= thr_sel, 0.0, NEG_BIG).T
        for g in range(A_KV_HEADS):
            kblk = k_ref[0, pl.ds(k0, tk), g * LANES:(g + 1) * LANES]
            vaug = jnp.concatenate([v_ref[0, pl.ds(k0, tk), g * LANES:(g + 1) * LANES], ones_blk], axis=1)
            for r in range(group):
                h = g * group + r
                for rt in range(tq // q_sub):
                    rows = pl.ds(rt * q_sub, q_sub)
                    q = aq_ref[0, rows, h * LANES:(h + 1) * LANES]
                    s = _dot_nt(q, kblk) + bias_sc[slot, rows, :]
                    m_prev = m_sc[h, rows, :]
                    m_new = jnp.maximum(m_prev, jnp.max(s, axis=1, keepdims=True))
                    alpha = jnp.exp2(m_prev - m_new)
                    p = jnp.exp2(s - _lane_tiled(m_new, n_lane_tiles))
                    pv = _dot(p.astype(MXU_DTYPE), vaug)
                    acc_sc[h, rows, :] = alpha * acc_sc[h, rows, :] + pv[:, :LANES]
                    l_sc[h, rows, :] = alpha * l_sc[h, rows, :] + pv[:, LANES:]
                    m_sc[h, rows, :] = m_new
    for_each_block(attend_block, ATT_UNROLL)
    for h in range(A_HEADS):
        ya_ref[0, :, h * LANES:(h + 1) * LANES] = (acc_sc[h] / l_sc[h]).astype(ya_ref.dtype)


def _dsa(aq, iq, iw_t, ika, ikb, k16, v16, *, key_off, topk, q_tile, key_tile):
    bsz, t, _ = aq.shape
    lp = k16.shape[1]
    tq, tk = q_tile, key_tile
    assert lp % tk == 0
    n_kb_all = lp // tk
    assert t % tq == 0 and tq % LANES == 0 and tk % (SUBLANES * COUNT_GROUPS) == 0 and key_off + t <= lp
    assert tk >= topk
    kernel = functools.partial(_dsa_kernel, tq=tq, tk=tk, key_off=key_off, topk=topk, q_sub=min(tq, ATT_ROWS))

    def q_spec(width):
        return pl.BlockSpec((1, tq, width), lambda b, i: (b, i, 0))

    def key_spec(width):
        return pl.BlockSpec((1, lp, width), lambda b, i: (b, 0, 0), pipeline_mode=pl.Buffered(1))

    return pl.pallas_call(
        kernel,
        grid=(bsz, t // tq),
        in_specs=[q_spec(A_WIDTH), q_spec(IQ_WIDTH), pl.BlockSpec((1, IDX_HEADS, tq), lambda b, i: (b, 0, i)),
                  key_spec(LANES), key_spec(LANES), key_spec(KV_WIDTH), key_spec(KV_WIDTH)],
        out_specs=q_spec(A_WIDTH),
        out_shape=jax.ShapeDtypeStruct((bsz, t, A_WIDTH), MXU_DTYPE),
        scratch_shapes=[pltpu.VMEM((n_kb_all, tk, tq), F32), pltpu.VMEM((ATT_UNROLL, tq, tk), F32),
                        pltpu.VMEM((A_HEADS, tq, LANES), F32), pltpu.VMEM((A_HEADS, tq, LANES), F32),
                        pltpu.VMEM((A_HEADS, tq, LANES), F32)],
        compiler_params=pltpu.CompilerParams(dimension_semantics=("arbitrary", "arbitrary"),
                                             vmem_limit_bytes=VMEM_LIMIT_BYTES),
        name="dsa",
    )(aq, iq, iw_t, ika, ikb, k16, v16)


SUB = SUBLANES
HGRN_ROWS = 4
HGRN_GROUP_HEADS = 2
N_SUB = CHUNK // SUB
INTER_ROWS = SUB * (N_SUB * (N_SUB - 1) // 2)


def _split3(x):
    hi = x.astype(MXU_DTYPE)
    r1 = x - hi.astype(F32)
    mid = r1.astype(MXU_DTYPE)
    lo = (r1 - mid.astype(F32)).astype(MXU_DTYPE)
    return hi, mid, lo


def _hgrn_constants():
    t = np.arange(CHUNK)
    place = ((np.arange(SUB * LANES)[:, None] // LANES) == (t[None, :] % SUB)).astype(np.float32)
    intra = ((t[:, None] // SUB == t[None, :] // SUB) & (t[None, :] <= t[:, None])).astype(np.float32)
    seg = np.concatenate([np.full(i * SUB, i) for i in range(1, N_SUB)])
    inter = ((t[:, None] // SUB) == seg[None, :]).astype(np.float32)
    return jnp.asarray(place, MXU_DTYPE), jnp.asarray(intra, F32), jnp.asarray(inter, F32)


def _hgrn_kernel(q_ref, k_ref, b_ref, v_ref, g_ref, gain_ref, s0_ref, place_ref, intra_ref, inter_ref,
                 yb_ref, sout_ref, st_sc):
    c = pl.program_id(1)
    n_rows = q_ref.shape[0]

    @pl.when(c == 0)
    def _():
        for i in range(n_rows):
            for h in range(B_HEADS):
                st_sc[i, h] = s0_ref[i, h].T

    def row_bcast(x, r, n):
        return jnp.broadcast_to(x[r:r + 1, :], (n, x.shape[1]))

    place = place_ref[...]

    def head_group(i, g):
        width = HGRN_GROUP_HEADS * LANES
        lanes = slice(g * width, (g + 1) * width)
        q = q_ref[i, :, lanes].astype(F32)
        k = k_ref[i, :, lanes].astype(F32)
        v16 = v_ref[i, :, lanes]
        b = b_ref[i, :, lanes]

        b_start = jnp.concatenate(
            [jnp.zeros((SUB, width), F32)] + [row_bcast(b, j * SUB - 1, SUB) for j in range(1, N_SUB)], axis=0)
        q_hat = (q * jnp.exp(b - b_start)).astype(MXU_DTYPE)
        q_state = (q * jnp.exp(b)).astype(MXU_DTYPE)
        b_end = b[CHUNK - 1:CHUNK, :]
        k_end = (k * jnp.exp(b_end - b)).astype(MXU_DTYPE)

        k_hat = jnp.concatenate(
            [k[:j * SUB] * jnp.exp(row_bcast(b, j * SUB - 1, j * SUB) - b[:j * SUB]) for j in range(1, N_SUB)],
            axis=0).astype(MXU_DTYPE)
        v_hat = jnp.concatenate([v16[:j * SUB] for j in range(1, N_SUB)], axis=0)

        q3 = q.reshape(N_SUB, SUB, width)
        k3 = k.reshape(N_SUB, SUB, width)
        b3 = b.reshape(N_SUB, SUB, width)
        intra = []
        for s in range(SUB):
            kp = jnp.broadcast_to(k3[:, s:s + 1, :], q3.shape)
            bp = jnp.broadcast_to(b3[:, s:s + 1, :], q3.shape)
            x = q3 * kp * jnp.exp(jnp.minimum(b3 - bp, 0.0))
            intra.append(x.reshape(CHUNK, width).astype(MXU_DTYPE))

        for hh in range(HGRN_GROUP_HEADS):
            h = g * HGRN_GROUP_HEADS + hh
            sl = slice(hh * LANES, (hh + 1) * LANES)
            out_lanes = slice(h * LANES, (h + 1) * LANES)
            st = st_sc[i, h]
            x_cat = jnp.concatenate([x[:, sl] for x in intra], axis=1)
            p_intra = (_dot(x_cat, place) * intra_ref[...]).astype(MXU_DTYPE)
            p_inter = (_dot_nt(q_hat[:, sl], k_hat[:, sl]) * inter_ref[...]).astype(MXU_DTYPE)
            o = (_dot(p_inter, v_hat[:, sl]) + _dot(p_intra, v16[:, sl])
                 + _dot_nt(q_state[:, sl], st.astype(MXU_DTYPE)))
            st_sc[i, h] = st * jnp.exp(b_end[:, sl]) + _dot(v16[:, sl].astype(F32).T.astype(MXU_DTYPE), k_end[:, sl])
            y = _rms(o, gain_ref[:, out_lanes]) * g_ref[i, :, out_lanes].astype(F32)
            yb_ref[i, :, out_lanes] = y.astype(yb_ref.dtype)

    for i in range(n_rows):
        for g in range(B_HEADS // HGRN_GROUP_HEADS):
            head_group(i, g)

    @pl.when(c == pl.num_programs(1) - 1)
    def _():
        for i in range(n_rows):
            for h in range(B_HEADS):
                sout_ref[i, h] = st_sc[i, h].T


def _hgrn(hq, hk, hlf, hv, hg, norm_gain, state0):
    bsz, t, _ = hq.shape
    assert t % CHUNK == 0 and bsz % HGRN_ROWS == 0
    nb = HGRN_ROWS
    consts = _hgrn_constants()
    seq_spec = pl.BlockSpec((nb, CHUNK, B_WIDTH), lambda b, c: (b, c, 0))
    state_spec = pl.BlockSpec((nb, B_HEADS, B_KEY_DIM, B_VAL_DIM), lambda b, c: (b, 0, 0, 0))
    return pl.pallas_call(
        _hgrn_kernel,
        grid=(bsz // nb, t // CHUNK),
        in_specs=[seq_spec] * 5 + [_resident((1, B_WIDTH)), state_spec] + [_resident(x.shape) for x in consts],
        out_specs=[seq_spec, state_spec],
        out_shape=[jax.ShapeDtypeStruct((bsz, t, B_WIDTH), MXU_DTYPE),
                   jax.ShapeDtypeStruct((bsz, B_HEADS, B_KEY_DIM, B_VAL_DIM), F32)],
        scratch_shapes=[pltpu.VMEM((nb, B_HEADS, B_VAL_DIM, B_KEY_DIM), F32)],
        compiler_params=pltpu.CompilerParams(dimension_semantics=("arbitrary", "arbitrary"),
                                             vmem_limit_bytes=VMEM_LIMIT_BYTES),
        name="hgrn",
    )(hq, hk, hlf, hv, hg, norm_gain.reshape(1, B_WIDTH), state0, *consts)


def _memory_kv_kernel(mem_ref, gain_ref, w_ref, k32_ref, v32_ref, k16_ref, v16_ref):
    hb = _rms(mem_ref[0], gain_ref[...]).astype(MXU_DTYPE)
    mk = _dot(hb, w_ref[:, :C_WIDTH])
    mv = _dot(hb, w_ref[:, C_WIDTH:])
    k32_ref[0] = mk
    v32_ref[0] = mv
    k16_ref[0] = mk.astype(MXU_DTYPE)
    v16_ref[0] = mv.astype(MXU_DTYPE)


def _memory_kv(mem, gain, w16):
    bsz, n, d = mem.shape
    spec = pl.BlockSpec((1, n, C_WIDTH), lambda b: (b, 0, 0))
    return pl.pallas_call(
        _memory_kv_kernel,
        grid=(bsz,),
        in_specs=[pl.BlockSpec((1, n, d), lambda b: (b, 0, 0)), _resident((1, d)), _resident((d, 2 * C_WIDTH))],
        out_specs=[spec] * 4,
        out_shape=[jax.ShapeDtypeStruct((bsz, n, C_WIDTH), dt) for dt in (F32, F32, MXU_DTYPE, MXU_DTYPE)],
        compiler_params=pltpu.CompilerParams(dimension_semantics=("arbitrary",), vmem_limit_bytes=VMEM_LIMIT_BYTES),
        name="memory_kv",
    )(mem, gain.reshape(1, d), w16)


def _merge_ffn_kernel(x_ref, ya_ref, yb_ref, cq_ref, mk_ref, mv_ref, gt_ref, wa_ref, wb_ref, wc_ref, wo_ref,
                      g_mix_ref, g_pre_ref, g_post_ref, wu_ref, wd_ref, out_ref):
    nb, tm, d = x_ref.shape
    rows = nb * tm

    def flat(ref):
        return ref[...].reshape(rows, ref.shape[-1])

    per_row = []
    for i in range(nb):
        heads = []
        for h in range(C_HEADS):
            sl = slice(h * C_HEAD_DIM, (h + 1) * C_HEAD_DIM)
            s = _dot_nt(cq_ref[i, :, sl], mk_ref[i, :, sl])
            p = jnp.exp(s - jnp.max(s, axis=-1, keepdims=True))
            p = p / jnp.sum(p, axis=-1, keepdims=True)
            heads.append(_dot(p.astype(MXU_DTYPE), mv_ref[i, :, sl]).astype(MXU_DTYPE))
        per_row.append(jnp.concatenate(heads, axis=1))
    y_c = per_row[0] if nb == 1 else jnp.concatenate(per_row, axis=0)

    gt = flat(gt_ref)
    merged = (gt[:, 0:d].astype(F32) * _dot(flat(ya_ref), wa_ref[...])
              + gt[:, d:2 * d].astype(F32) * _dot(flat(yb_ref), wb_ref[...])
              + gt[:, 2 * d:3 * d].astype(F32) * _dot(y_c, wc_ref[...]))
    x1 = flat(x_ref) + _rms(_dot(merged.astype(MXU_DTYPE), wo_ref[...]), g_mix_ref[...])
    u = jnp.maximum(_dot(_rms(x1, g_pre_ref[...]).astype(MXU_DTYPE), wu_ref[...]), 0.0)
    down = _dot((u * u).astype(MXU_DTYPE), wd_ref[...])
    out_ref[...] = (x1 + _rms(down, g_post_ref[...])).reshape(nb, tm, d)


def _merge_ffn(x, ya, yb, cq, mk16, mv16, gates, weights, gains, row_tile):
    bsz, t, d = x.shape
    tm = min(row_tile, t)
    nb = max(1, min(bsz, row_tile // t))
    assert t % tm == 0 and bsz % nb == 0
    n_mem = mk16.shape[1]

    def rows(width):
        return pl.BlockSpec((nb, tm, width), lambda b, i: (b, i, 0))

    mem_spec = pl.BlockSpec((nb, n_mem, C_WIDTH), lambda b, i: (b, 0, 0))
    return pl.pallas_call(
        _merge_ffn_kernel,
        grid=(bsz // nb, t // tm),
        in_specs=[rows(d), rows(A_WIDTH), rows(B_WIDTH), rows(C_WIDTH), mem_spec, mem_spec, rows(N_BRANCHES * d)]
        + [_resident(w.shape) for w in weights[:4]] + [_resident((1, d))] * 3
        + [_resident(w.shape) for w in weights[4:]],
        out_specs=rows(d),
        out_shape=jax.ShapeDtypeStruct((bsz, t, d), F32),
        compiler_params=pltpu.CompilerParams(dimension_semantics=("arbitrary", "arbitrary"),
                                             vmem_limit_bytes=VMEM_LIMIT_BYTES),
        name="merge_ffn",
    )(x, ya, yb, cq, mk16, mv16, gates, *weights[:4], *[g.reshape(1, d) for g in gains], *weights[4:])


def _layer(x, pos, lower_bound, lp, *, cache, state0, mem16, row_tile, q_tile, key_tile):
    bsz, t, d = x.shape
    (aq, k32, v32, k16, v16, iq, ikw, ika, ikb, hq, hk, hlf, hv, hg, cq, gates, ik32) = _in_proj(
        x, pos, lp["pre_mix_gain"], lower_bound, lp["w_in"], min(row_tile, bsz * t))

    keys = [ika, ikb, k16, v16]
    if cache is not None:
        keys = [jnp.concatenate([c, kn], axis=1) for c, kn in zip(cache, keys)]
    n_keys = keys[0].shape[1]
    t_pad = -(-t // q_tile) * q_tile
    key_off = n_keys - t
    n_pad = -(-(key_off + t_pad) // key_tile) * key_tile
    ika, ikb, k16, v16 = [jnp.pad(kk, ((0, 0), (0, n_pad - n_keys), (0, 0))) for kk in keys]
    iw_t = jnp.swapaxes(ikw[..., IW_LANE:IW_LANE + IDX_HEADS], 1, 2)
    q_pad = ((0, 0), (0, t_pad - t), (0, 0))
    ya = _dsa(jnp.pad(aq, q_pad), jnp.pad(iq, q_pad), jnp.pad(iw_t, ((0, 0), (0, 0), (0, t_pad - t))),
              ika, ikb, k16, v16, key_off=key_off, topk=min(TOPK_MAX, n_keys // 4), q_tile=q_tile,
              key_tile=key_tile)[:, :t]

    yb, s_new = _hgrn(hq, hk, hlf, hv, hg, lp["hgrn_norm_gain"], state0)

    out = _merge_ffn(x, ya, yb, cq, mem16[0], mem16[1], gates,
                     [lp["w_out_a"], lp["w_out_b"], lp["w_out_c"], lp["w_out"], lp["w_up"], lp["w_down"]],
                     [lp["post_mix_gain"], lp["pre_ffn_gain"], lp["post_ffn_gain"]], row_tile=MERGE_ROWS)
    new_k = k32.reshape(bsz, t, A_KV_HEADS, HEAD_DIM)
    new_v = v32.reshape(bsz, t, A_KV_HEADS, HEAD_DIM)
    return out, (new_k, new_v, ik32, s_new)


PROJ_ROWS = 512
CUMSUM_ROWS = 256
MERGE_ROWS = 512
DSA_QUERIES = 512


def _tiles(t):
    return PROJ_ROWS, max(LANES, min(DSA_QUERIES, t))


def kernel(x_prompt, x_sample, cache_k, cache_v, cache_idx_k, state_hgrn, cache_mem_k, cache_mem_v, mem_prompt, w_in, w_mem_kv, mem_norm_gain, hgrn_lb_logits, hgrn_norm_gain, w_out_a, w_out_b, w_out_c, w_out, pre_mix_gain, post_mix_gain, pre_ffn_gain, post_ffn_gain, w_up, w_down):
    depth = w_in.shape[0]
    d = x_prompt.shape[-1]
    bp, tp, _ = x_prompt.shape
    bs, ts, _ = x_sample.shape
    past = cache_k.shape[2]
    lower_bounds = jnp.cumsum(jax.nn.softmax(hgrn_lb_logits.astype(F32), axis=0), axis=0)
    pos_p = jnp.arange(tp, dtype=I32)
    pos_s = past + jnp.arange(ts, dtype=I32)
    bf = lambda a: a.astype(MXU_DTYPE)

    xp, xs = x_prompt, x_sample
    outs_p, outs_s = [], []
    for l in range(depth):
        lp = {
            "w_in": _pack_w_in(w_in[l], d), "hgrn_norm_gain": hgrn_norm_gain[l],
            "w_out_a": bf(w_out_a[l]), "w_out_b": bf(w_out_b[l]), "w_out_c": bf(w_out_c[l]), "w_out": bf(w_out[l]),
            "pre_mix_gain": pre_mix_gain[l], "post_mix_gain": post_mix_gain[l], "pre_ffn_gain": pre_ffn_gain[l],
            "post_ffn_gain": post_ffn_gain[l], "w_up": bf(w_up[l]), "w_down": bf(w_down[l]),
        }
        mk32, mv32, mk16, mv16 = _memory_kv(mem_prompt, mem_norm_gain[l], bf(w_mem_kv[l]))
        n_mem = mem_prompt.shape[1]
        row_tile, q_tile = _tiles(tp)
        xp, st_p = _layer(xp, pos_p, lower_bounds[l], lp, cache=None,
                          state0=jnp.zeros((bp, B_HEADS, B_KEY_DIM, B_VAL_DIM), F32), mem16=(mk16, mv16),
                          row_tile=row_tile, q_tile=q_tile, key_tile=512)
        outs_p.append(st_p + (mk32.reshape(bp, n_mem, C_HEADS, C_HEAD_DIM), mv32.reshape(bp, n_mem, C_HEADS, C_HEAD_DIM)))

        cik = cache_idx_k[l]
        zeros_ik = jnp.zeros_like(cik)
        cache = (bf(jnp.concatenate([cik, zeros_ik], axis=-1)), bf(jnp.concatenate([zeros_ik, cik], axis=-1)),
                 bf(cache_k[l].reshape(bs, past, KV_WIDTH)), bf(cache_v[l].reshape(bs, past, KV_WIDTH)))
        mem16 = (bf(cache_mem_k[l].reshape(bs, -1, C_WIDTH)), bf(cache_mem_v[l].reshape(bs, -1, C_WIDTH)))
        row_tile, q_tile = _tiles(ts)
        xs, st_s = _layer(xs, pos_s, lower_bounds[l], lp, cache=cache, state0=state_hgrn[l].astype(F32),
                          mem16=mem16, row_tile=row_tile, q_tile=q_tile, key_tile=384)
        outs_s.append(st_s)

    new_k_p, new_v_p, new_ik_p, new_s_p, new_mk_p, new_mv_p = [jnp.stack(a) for a in zip(*outs_p)]
    new_k_s, new_v_s, new_ik_s, new_s_s = [jnp.stack(a) for a in zip(*outs_s)]
    return (xp, xs, new_k_p, new_v_p, new_ik_p, new_s_p, new_mk_p, new_mv_p, new_k_s, new_v_s, new_ik_s, new_s_s)
```

```python
import functools

import jax
import jax.numpy as jnp
import numpy as np
from jax import lax
from jax.experimental import pallas as pl
from jax.experimental.pallas import tpu as pltpu

CHUNK = 64
N_BRANCHES = 3
A_HEADS, A_KV_HEADS, HEAD_DIM = 6, 2, 128
ROT_DIM = HEAD_DIM // 4
IDX_HEADS, IDX_DIM = 8, 64
IDX_ROT_DIM = IDX_DIM // 4
TOPK_MAX = 256
B_HEADS, B_KEY_DIM, B_VAL_DIM = 6, 128, 128
C_HEADS, C_HEAD_DIM = 4, 128
ROPE_THETA = 500000.0
EPS = 1e-6

A_WIDTH = A_HEADS * HEAD_DIM
KV_WIDTH = A_KV_HEADS * HEAD_DIM
IQ_WIDTH = IDX_HEADS * IDX_DIM
B_WIDTH = B_HEADS * B_KEY_DIM
C_WIDTH = C_HEADS * C_HEAD_DIM

LANES = 128
SUBLANES = 8
VMEM_LIMIT_BYTES = 56 * 1024 * 1024

MXU_DTYPE = jnp.bfloat16
F32 = jnp.float32
I32 = jnp.int32

IW_LANE = 96
NEG_BIG = -1e30
LOG2E = 1.4426950408889634
ATT_ROWS = 128
INT_MIN = -(2 ** 31)
F32_LOWEST = -3.4028234663852886e38

OFF_AQ = 0
OFF_AK = OFF_AQ + A_WIDTH
OFF_AV = OFF_AK + KV_WIDTH
OFF_IQ = OFF_AV + KV_WIDTH
OFF_IKW = OFF_IQ + IQ_WIDTH
OFF_BQ = OFF_IKW + LANES
OFF_BF = OFF_BQ + B_WIDTH
OFF_BI = OFF_BF + B_WIDTH
OFF_BG = OFF_BI + B_WIDTH
OFF_CQ = OFF_BG + B_WIDTH
OFF_GATES = OFF_CQ + C_WIDTH


def _sigmoid(x):
    return 1.0 / (1.0 + jnp.exp(-x))


def _dot(a, b):
    return jnp.dot(a, b, preferred_element_type=F32)


def _dot_nt(a, b):
    return lax.dot_general(a, b, (((1,), (1,)), ((), ())), preferred_element_type=F32)


def _rms(x, gain):
    return x * lax.rsqrt(jnp.mean(x * x, axis=-1, keepdims=True) + EPS) * gain


def _resident(shape):
    zeros = (0,) * len(shape)
    return pl.BlockSpec(shape, lambda *_: zeros, pipeline_mode=pl.Buffered(1))


def _rope(y, cos, sin_up, sin_down, half):
    return y * cos + pltpu.roll(y, half, 1) * sin_up + pltpu.roll(y, LANES - half, 1) * sin_down


def _in_proj_kernel(x_ref, gain_ref, lb_ref, w_ref, tri_ref, ca_ref, sau_ref, sad_ref, ci_ref, siu_ref, sid_ref,
                    aq_ref, k32_ref, v32_ref, k16_ref, v16_ref, iq_ref, ikw_ref, ika_ref, ikb_ref,
                    hq_ref, hk_ref, hlf_ref, hv_ref, hg_ref, cq_ref, gt_ref, ik32_ref):
    hb = _rms(x_ref[...], gain_ref[...]).astype(MXU_DTYPE)

    def proj(c0, width):
        return _dot(hb, w_ref[:, c0:c0 + width])

    ca, sau, sad = ca_ref[...], sau_ref[...], sad_ref[...]
    ci, siu, sid = ci_ref[...], siu_ref[...], sid_ref[...]

    y = proj(OFF_AQ, A_WIDTH)
    for h in range(A_HEADS):
        sl = slice(h * LANES, (h + 1) * LANES)
        aq_ref[:, sl] = (_rope(y[:, sl], ca, sau, sad, ROT_DIM // 2) * ((HEAD_DIM ** -0.5) * LOG2E)).astype(MXU_DTYPE)

    y = proj(OFF_AK, KV_WIDTH)
    for h in range(A_KV_HEADS):
        sl = slice(h * LANES, (h + 1) * LANES)
        r = _rope(y[:, sl], ca, sau, sad, ROT_DIM // 2)
        k32_ref[:, sl] = r
        k16_ref[:, sl] = r.astype(MXU_DTYPE)

    y = proj(OFF_AV, KV_WIDTH)
    v32_ref[...] = y
    v16_ref[...] = y.astype(MXU_DTYPE)

    y = proj(OFF_IQ, IQ_WIDTH)
    for j in range(IQ_WIDTH // LANES):
        sl = slice(j * LANES, (j + 1) * LANES)
        iq_ref[:, sl] = _rope(y[:, sl], ci, siu, sid, IDX_ROT_DIM // 2).astype(MXU_DTYPE)

    r = _rope(proj(OFF_IKW, LANES), ci, siu, sid, IDX_ROT_DIM // 2)
    lane = lax.broadcasted_iota(I32, r.shape, 1)
    ikw_ref[...] = jnp.where(lane < IDX_DIM, r, r * ((IDX_DIM ** -0.5) * (IDX_HEADS ** -0.5)))
    ik32_ref[...] = r[:, :IDX_DIM]
    ik_low = jnp.where(lane < IDX_DIM, r, 0.0)
    ika_ref[...] = ik_low.astype(MXU_DTYPE)
    ikb_ref[...] = pltpu.roll(ik_low, IDX_DIM, 1).astype(MXU_DTYPE)

    y = proj(OFF_BQ, B_WIDTH)
    hq_ref[...] = (y * _sigmoid(y)).astype(MXU_DTYPE)
    lb = lb_ref[...]
    f = lb + (1.0 - lb) * _sigmoid(proj(OFF_BF, B_WIDTH))
    hk_ref[...] = (1.0 - f).astype(MXU_DTYPE)
    hi, mid, lo = _split3(jnp.log(f))
    tri = tri_ref[...]
    n_tri = tri.shape[0]
    for r0 in range(0, hlf_ref.shape[0], n_tri):
        rs = slice(r0, r0 + n_tri)
        hlf_ref[rs, :] = _dot(tri, hi[rs]) + _dot(tri, mid[rs]) + _dot(tri, lo[rs])
    hv_ref[...] = proj(OFF_BI, B_WIDTH).astype(MXU_DTYPE)
    y = proj(OFF_BG, B_WIDTH)
    hg_ref[...] = (y * _sigmoid(y)).astype(MXU_DTYPE)

    cq_ref[...] = (proj(OFF_CQ, C_WIDTH) * (C_HEAD_DIM ** -0.5)).astype(MXU_DTYPE)

    n_gate = gt_ref.shape[1]
    for c0 in range(0, n_gate, B_WIDTH):
        gt_ref[:, c0:c0 + B_WIDTH] = _sigmoid(proj(OFF_GATES + c0, B_WIDTH)).astype(MXU_DTYPE)


def _rope_tables(pos, rot_dim, head_dim):
    half = rot_dim // 2
    inv_freq = ROPE_THETA ** (-jnp.arange(half, dtype=F32) / half)
    ang = pos.astype(F32)[:, None] * inv_freq[None, :]
    cos, sin = jnp.cos(ang), jnp.sin(ang)
    lane = np.arange(LANES) % head_dim
    idx = lane % half
    first, second = lane < half, (lane >= half) & (lane < rot_dim)
    cos_t = jnp.where(first | second, cos[:, idx], 1.0)
    sin_up = jnp.where(second, sin[:, idx], 0.0)
    sin_down = jnp.where(first, -sin[:, idx], 0.0)
    return cos_t, sin_up, sin_down


def _in_proj(x, pos, gain, lower_bound, w_packed, row_tile):
    bsz, t, d = x.shape
    n = bsz * t
    tm = row_tile
    assert n % tm == 0 and (t % tm == 0 or tm % t == 0)
    tables = _rope_tables(pos, ROT_DIM, HEAD_DIM) + _rope_tables(pos, IDX_ROT_DIM, IDX_DIM)
    table_rows = max(t, tm)
    tables = [jnp.tile(tb, (table_rows // t, 1)) for tb in tables]
    n_table_blocks = table_rows // tm
    n_gate = N_BRANCHES * d
    w_width = w_packed.shape[1]
    n_tri = min(tm, CUMSUM_ROWS)
    assert n_tri % CHUNK == 0 and tm % n_tri == 0 and t % CHUNK == 0
    r = np.arange(n_tri)
    chunk_tri = jnp.asarray((r[:, None] // CHUNK == r[None, :] // CHUNK) & (r[None, :] <= r[:, None]), MXU_DTYPE)

    def rows(width):
        return pl.BlockSpec((tm, width), lambda i: (i, 0))

    table_spec = pl.BlockSpec((tm, LANES), lambda i: (i % n_table_blocks, 0))
    widths_dtypes = [
        (A_WIDTH, MXU_DTYPE), (KV_WIDTH, F32), (KV_WIDTH, F32), (KV_WIDTH, MXU_DTYPE), (KV_WIDTH, MXU_DTYPE),
        (IQ_WIDTH, MXU_DTYPE), (LANES, F32), (LANES, MXU_DTYPE), (LANES, MXU_DTYPE),
        (B_WIDTH, MXU_DTYPE), (B_WIDTH, MXU_DTYPE), (B_WIDTH, F32), (B_WIDTH, MXU_DTYPE), (B_WIDTH, MXU_DTYPE),
        (C_WIDTH, MXU_DTYPE), (n_gate, MXU_DTYPE), (IDX_DIM, F32)]
    outs = pl.pallas_call(
        _in_proj_kernel,
        grid=(n // tm,),
        in_specs=[rows(d), _resident((1, d)), _resident((1, B_WIDTH)), _resident((d, w_width)), _resident((n_tri, n_tri))]
        + [table_spec] * 6,
        out_specs=[rows(w) for w, _ in widths_dtypes],
        out_shape=[jax.ShapeDtypeStruct((n, w), dt) for w, dt in widths_dtypes],
        compiler_params=pltpu.CompilerParams(dimension_semantics=("arbitrary",), vmem_limit_bytes=VMEM_LIMIT_BYTES),
        name="in_proj",
    )(x.reshape(n, d), gain.reshape(1, d), lower_bound.reshape(1, B_WIDTH), w_packed, chunk_tri, *tables)
    return [o.reshape(bsz, t, o.shape[-1]) for o in outs]


def _pack_w_in(w, d):
    widths = (A_WIDTH, KV_WIDTH, KV_WIDTH, IQ_WIDTH, IDX_DIM, IDX_HEADS, B_WIDTH, B_WIDTH, B_WIDTH, B_WIDTH,
              C_WIDTH, N_BRANCHES * d)
    points = [int(s) for s in np.cumsum(widths)[:-1]]
    a_q, a_k, a_v, i_q, i_k, i_w, b_q, b_f, b_i, b_g, c_q, gates = jnp.split(w, points, axis=-1)
    ikw = jnp.concatenate([i_k, jnp.zeros((d, IW_LANE - IDX_DIM), w.dtype), i_w,
                           jnp.zeros((d, LANES - IW_LANE - IDX_HEADS), w.dtype)], axis=-1)
    return jnp.concatenate([a_q, a_k, a_v, i_q, ikw, b_q, b_f, b_i, b_g, c_q, gates], axis=-1).astype(MXU_DTYPE)


def _float_of_key(key):
    return pltpu.bitcast(jnp.where(key < 0, (key - 1) ^ 0x7FFFFFFF, key), F32)


def _lane_tiled(x, n_tiles):
    return x if n_tiles == 1 else jnp.concatenate([x] * n_tiles, axis=1)


COUNT_GROUPS = 8
COUNT_TIERS = ((8, 6), (4, 3))
TIE_ROWS = 128
SCORE_UNROLL = 4
ATT_UNROLL = 4


def _dsa_kernel(aq_ref, iq_ref, iw_ref, ika_ref, ikb_ref, k_ref, v_ref, ya_ref,
                key_sc, bias_sc, m_sc, l_sc, acc_sc, *, tq, tk, key_off, topk, q_sub):
    qi = pl.program_id(1)
    n_vis = key_off + (qi + 1) * tq
    n_kb = lax.div(n_vis + (tk - 1), tk)
    q_pos = qi * tq + lax.broadcasted_iota(I32, (1, tq), 1)
    limit = key_off + (lax.shift_right_logical(q_pos, CHUNK.bit_length() - 1) + 1) * CHUNK

    def for_each_block(body, unroll):
        def trip(j, carry):
            for i in range(unroll):
                body(unroll * j + i, i)
            return carry

        n_trips = lax.div(n_kb, unroll)
        lax.fori_loop(0, n_trips, trip, 0)
        done = unroll * n_trips
        size = unroll // 2
        while size >= 1:
            has = ((n_kb - done) & size) != 0

            @pl.when(has)
            def _(done=done, size=size):
                for i in range(size):
                    body(done + i, i)

            done = done + jnp.where(has, size, 0)
            size //= 2

    def score_block(kb, slot):
        k0 = pl.multiple_of(kb * tk, tk)
        ika = ika_ref[0, pl.ds(k0, tk), :]
        ikb = ikb_ref[0, pl.ds(k0, tk), :]
        acc = jnp.zeros((tk, tq), F32)
        for j in range(IQ_WIDTH // LANES):
            iqt = iq_ref[0, :, j * LANES:(j + 1) * LANES]
            acc = acc + jnp.maximum(_dot_nt(ika, iqt), 0.0) * iw_ref[0, 2 * j:2 * j + 1, :]
            acc = acc + jnp.maximum(_dot_nt(ikb, iqt), 0.0) * iw_ref[0, 2 * j + 1:2 * j + 2, :]
        key_pos = k0 + lax.broadcasted_iota(I32, (tk, tq), 0)
        key_sc[kb] = jnp.where(key_pos < limit, acc, -jnp.inf)

    for_each_block(score_block, SCORE_UNROLL)

    ones_rows = jnp.ones((2 * SUBLANES, tk), MXU_DTYPE)

    def count(pred):
        step = SUBLANES * COUNT_GROUPS

        def hits(kb):
            return _dot(ones_rows, jnp.where(pred(key_sc[kb]), 1.0, 0.0).astype(MXU_DTYPE))

        def one_block(kb, acc):
            for r0 in range(0, tk, step):
                acc = acc + jnp.where(pred(key_sc[kb, r0:r0 + step, :]), 1, 0)
            return acc

        def group_body(first, size, n_matmul):
            def group(j, carry):
                by_matmul, by_adds = carry
                kb = first + size * j
                parts = [hits(kb + i) for i in range(n_matmul)]
                while len(parts) > 1:
                    parts = [a + b for a, b in zip(parts[0::2], parts[1::2])] + parts[len(parts) & ~1:]
                for i in range(n_matmul, size):
                    by_adds = one_block(kb + i, by_adds)
                return by_matmul + parts[0], by_adds
            return group

        carry = (jnp.zeros((2 * SUBLANES, tq), F32), jnp.zeros((step, tq), I32))
        first = 0
        for size, n_matmul in COUNT_TIERS:
            n_groups = lax.div(n_kb - first, size)
            carry = lax.fori_loop(0, n_groups, group_body(first, size, n_matmul), carry)
            first = first + size * n_groups
        by_matmul, by_adds = carry
        by_adds = lax.fori_loop(first, n_kb, one_block, by_adds)
        return by_matmul[0:1, :].astype(I32) + jnp.sum(by_adds, axis=0, keepdims=True)

    def key_bit(it, carry):
        u, n_ge = carry
        trial = u | jnp.left_shift(jnp.int32(1), 31 - it)
        cand = _float_of_key(trial ^ INT_MIN)
        n_trial = count(lambda s: s >= cand)
        accept = n_trial >= topk
        return jnp.where(accept, trial, u), jnp.where(accept, n_trial, n_ge)

    u, n_ge = lax.fori_loop(0, 32, key_bit, (jnp.zeros((1, tq), I32), jnp.zeros((1, tq), I32)))
    thr = _float_of_key(u ^ INT_MIN)
    n_gt = count(lambda s: s > thr)
    has_tie = jnp.where((n_ge > topk) & (thr > -jnp.inf), 1.0, 0.0)

    @pl.when(jnp.max(has_tie) > 0.0)
    def _():
        need = (topk - n_gt).astype(F32)
        ri = lax.broadcasted_iota(I32, (TIE_ROWS, TIE_ROWS), 0)
        ci = lax.broadcasted_iota(I32, (TIE_ROWS, TIE_ROWS), 1)
        lower = jnp.where(ci <= ri, 1.0, 0.0).astype(MXU_DTYPE)

        def demote(kb, seen):
            pieces = []
            for r0 in range(0, tk, TIE_ROWS):
                blk = key_sc[kb, r0:r0 + TIE_ROWS, :]
                tie = blk == thr
                pieces.append((r0, blk, tie, _dot(lower, jnp.where(tie, 1.0, 0.0).astype(MXU_DTYPE))))
            for r0, blk, tie, prefix in pieces:
                rank = prefix + seen
                key_sc[kb, r0:r0 + TIE_ROWS, :] = jnp.where(tie & (rank > need), -jnp.inf, blk)
                seen = rank[TIE_ROWS - 1:TIE_ROWS, :]
            return seen

        lax.fori_loop(0, n_kb, demote, jnp.zeros((1, tq), F32))

    thr_sel = jnp.maximum(thr, F32_LOWEST)

    m_sc[...] = jnp.full(m_sc.shape, NEG_BIG, F32)
    l_sc[...] = jnp.zeros(l_sc.shape, F32)
    acc_sc[...] = jnp.zeros(acc_sc.shape, F32)
    group = A_HEADS // A_KV_HEADS
    n_lane_tiles = tk // LANES
    ones_blk = jnp.ones((tk, LANES), MXU_DTYPE)

    def attend_block(kb, slot):
        k0 = pl.multiple_of(kb * tk, tk)
        bias_sc[slot] = jnp.where(key_sc[kb] >= thr_sel, 0.0, NEG_BIG).T
        for g in range(A_KV_HEADS):
            kblk = k_ref[0, pl.ds(k0, tk), g * LANES:(g + 1) * LANES]
            vaug = jnp.concatenate([v_ref[0, pl.ds(k0, tk), g * LANES:(g + 1) * LANES], ones_blk], axis=1)
            for r in range(group):
                h = g * group + r
                for rt in range(tq // q_sub):
                    rows = pl.ds(rt * q_sub, q_sub)
                    q = aq_ref[0, rows, h * LANES:(h + 1) * LANES]
                    s = _dot_nt(q, kblk) + bias_sc[slot, rows, :]
                    m_prev = m_sc[h, rows, :]
                    m_new = jnp.maximum(m_prev, jnp.max(s, axis=1, keepdims=True))
                    alpha = jnp.exp2(m_prev - m_new)
                    p = jnp.exp2(s - _lane_tiled(m_new, n_lane_tiles))
                    pv = _dot(p.astype(MXU_DTYPE), vaug)
                    acc_sc[h, rows, :] = alpha * acc_sc[h, rows, :] + pv[:, :LANES]
                    l_sc[h, rows, :] = alpha * l_sc[h, rows, :] + pv[:, LANES:]
                    m_sc[h, rows, :] = m_new
    for_each_block(attend_block, ATT_UNROLL)
    for h in range(A_HEADS):
        ya_ref[0, :, h * LANES:(h + 1) * LANES] = (acc_sc[h] / l_sc[h]).astype(ya_ref.dtype)


def _dsa(aq, iq, iw_t, ika, ikb, k16, v16, *, key_off, topk, q_tile, key_tile):
    bsz, t, _ = aq.shape
    lp = k16.shape[1]
    tq, tk = q_tile, key_tile
    assert lp % tk == 0
    n_kb_all = lp // tk
    assert t % tq == 0 and tq % LANES == 0 and tk % (SUBLANES * COUNT_GROUPS) == 0 and key_off + t <= lp
    assert tk >= topk
    kernel = functools.partial(_dsa_kernel, tq=tq, tk=tk, key_off=key_off, topk=topk, q_sub=min(tq, ATT_ROWS))

    def q_spec(width):
        return pl.BlockSpec((1, tq, width), lambda b, i: (b, i, 0))

    def key_spec(width):
        return pl.BlockSpec((1, lp, width), lambda b, i: (b, 0, 0), pipeline_mode=pl.Buffered(1))

    return pl.pallas_call(
        kernel,
        grid=(bsz, t // tq),
        in_specs=[q_spec(A_WIDTH), q_spec(IQ_WIDTH), pl.BlockSpec((1, IDX_HEADS, tq), lambda b, i: (b, 0, i)),
                  key_spec(LANES), key_spec(LANES), key_spec(KV_WIDTH), key_spec(KV_WIDTH)],
        out_specs=q_spec(A_WIDTH),
        out_shape=jax.ShapeDtypeStruct((bsz, t, A_WIDTH), MXU_DTYPE),
        scratch_shapes=[pltpu.VMEM((n_kb_all, tk, tq), F32), pltpu.VMEM((ATT_UNROLL, tq, tk), F32),
                        pltpu.VMEM((A_HEADS, tq, LANES), F32), pltpu.VMEM((A_HEADS, tq, LANES), F32),
                        pltpu.VMEM((A_HEADS, tq, LANES), F32)],
        compiler_params=pltpu.CompilerParams(dimension_semantics=("arbitrary", "arbitrary"),
                                             vmem_limit_bytes=VMEM_LIMIT_BYTES),
        name="dsa",
    )(aq, iq, iw_t, ika, ikb, k16, v16)


SUB = SUBLANES
HGRN_ROWS = 4
HGRN_GROUP_HEADS = 2
N_SUB = CHUNK // SUB
INTER_ROWS = SUB * (N_SUB * (N_SUB - 1) // 2)


def _split3(x):
    hi = x.astype(MXU_DTYPE)
    r1 = x - hi.astype(F32)
    mid = r1.astype(MXU_DTYPE)
    lo = (r1 - mid.astype(F32)).astype(MXU_DTYPE)
    return hi, mid, lo


def _hgrn_constants():
    t = np.arange(CHUNK)
    place = ((np.arange(SUB * LANES)[:, None] // LANES) == (t[None, :] % SUB)).astype(np.float32)
    intra = ((t[:, None] // SUB == t[None, :] // SUB) & (t[None, :] <= t[:, None])).astype(np.float32)
    seg = np.concatenate([np.full(i * SUB, i) for i in range(1, N_SUB)])
    inter = ((t[:, None] // SUB) == seg[None, :]).astype(np.float32)
    return jnp.asarray(place, MXU_DTYPE), jnp.asarray(intra, F32), jnp.asarray(inter, F32)


def _hgrn_kernel(q_ref, k_ref, b_ref, v_ref, g_ref, gain_ref, s0_ref, place_ref, intra_ref, inter_ref,
                 yb_ref, sout_ref, st_sc):
    c = pl.program_id(1)
    n_rows = q_ref.shape[0]

    @pl.when(c == 0)
    def _():
        for i in range(n_rows):
            for h in range(B_HEADS):
                st_sc[i, h] = s0_ref[i, h].T

    def row_bcast(x, r, n):
        return jnp.broadcast_to(x[r:r + 1, :], (n, x.shape[1]))

    place = place_ref[...]

    def head_group(i, g):
        width = HGRN_GROUP_HEADS * LANES
        lanes = slice(g * width, (g + 1) * width)
        q = q_ref[i, :, lanes].astype(F32)
        k = k_ref[i, :, lanes].astype(F32)
        v16 = v_ref[i, :, lanes]
        b = b_ref[i, :, lanes]

        b_start = jnp.concatenate(
            [jnp.zeros((SUB, width), F32)] + [row_bcast(b, j * SUB - 1, SUB) for j in range(1, N_SUB)], axis=0)
        q_hat = (q * jnp.exp(b - b_start)).astype(MXU_DTYPE)
        q_state = (q * jnp.exp(b)).astype(MXU_DTYPE)
        b_end = b[CHUNK - 1:CHUNK, :]
        k_end = (k * jnp.exp(b_end - b)).astype(MXU_DTYPE)

        k_hat = jnp.concatenate(
            [k[:j * SUB] * jnp.exp(row_bcast(b, j * SUB - 1, j * SUB) - b[:j * SUB]) for j in range(1, N_SUB)],
            axis=0).astype(MXU_DTYPE)
        v_hat = jnp.concatenate([v16[:j * SUB] for j in range(1, N_SUB)], axis=0)

        q3 = q.reshape(N_SUB, SUB, width)
        k3 = k.reshape(N_SUB, SUB, width)
        b3 = b.reshape(N_SUB, SUB, width)
        intra = []
        for s in range(SUB):
            kp = jnp.broadcast_to(k3[:, s:s + 1, :], q3.shape)
            bp = jnp.broadcast_to(b3[:, s:s + 1, :], q3.shape)
            x = q3 * kp * jnp.exp(jnp.minimum(b3 - bp, 0.0))
            intra.append(x.reshape(CHUNK, width).astype(MXU_DTYPE))

        for hh in range(HGRN_GROUP_HEADS):
            h = g * HGRN_GROUP_HEADS + hh
            sl = slice(hh * LANES, (hh + 1) * LANES)
            out_lanes = slice(h * LANES, (h + 1) * LANES)
            st = st_sc[i, h]
            x_cat = jnp.concatenate([x[:, sl] for x in intra], axis=1)
            p_intra = (_dot(x_cat, place) * intra_ref[...]).astype(MXU_DTYPE)
            p_inter = (_dot_nt(q_hat[:, sl], k_hat[:, sl]) * inter_ref[...]).astype(MXU_DTYPE)
            o = (_dot(p_inter, v_hat[:, sl]) + _dot(p_intra, v16[:, sl])
                 + _dot_nt(q_state[:, sl], st.astype(MXU_DTYPE)))
            st_sc[i, h] = st * jnp.exp(b_end[:, sl]) + _dot(v16[:, sl].astype(F32).T.astype(MXU_DTYPE), k_end[:, sl])
            y = _rms(o, gain_ref[:, out_lanes]) * g_ref[i, :, out_lanes].astype(F32)
            yb_ref[i, :, out_lanes] = y.astype(yb_ref.dtype)

    for i in range(n_rows):
        for g in range(B_HEADS // HGRN_GROUP_HEADS):
            head_group(i, g)

    @pl.when(c == pl.num_programs(1) - 1)
    def _():
        for i in range(n_rows):
            for h in range(B_HEADS):
                sout_ref[i, h] = st_sc[i, h].T


def _hgrn(hq, hk, hlf, hv, hg, norm_gain, state0):
    bsz, t, _ = hq.shape
    assert t % CHUNK == 0 and bsz % HGRN_ROWS == 0
    nb = HGRN_ROWS
    consts = _hgrn_constants()
    seq_spec = pl.BlockSpec((nb, CHUNK, B_WIDTH), lambda b, c: (b, c, 0))
    state_spec = pl.BlockSpec((nb, B_HEADS, B_KEY_DIM, B_VAL_DIM), lambda b, c: (b, 0, 0, 0))
    return pl.pallas_call(
        _hgrn_kernel,
        grid=(bsz // nb, t // CHUNK),
        in_specs=[seq_spec] * 5 + [_resident((1, B_WIDTH)), state_spec] + [_resident(x.shape) for x in consts],
        out_specs=[seq_spec, state_spec],
        out_shape=[jax.ShapeDtypeStruct((bsz, t, B_WIDTH), MXU_DTYPE),
                   jax.ShapeDtypeStruct((bsz, B_HEADS, B_KEY_DIM, B_VAL_DIM), F32)],
        scratch_shapes=[pltpu.VMEM((nb, B_HEADS, B_VAL_DIM, B_KEY_DIM), F32)],
        compiler_params=pltpu.CompilerParams(dimension_semantics=("arbitrary", "arbitrary"),
                                             vmem_limit_bytes=VMEM_LIMIT_BYTES),
        name="hgrn",
    )(hq, hk, hlf, hv, hg, norm_gain.reshape(1, B_WIDTH), state0, *consts)


def _memory_kv_kernel(mem_ref, gain_ref, w_ref, k32_ref, v32_ref, k16_ref, v16_ref):
    hb = _rms(mem_ref[0], gain_ref[...]).astype(MXU_DTYPE)
    mk = _dot(hb, w_ref[:, :C_WIDTH])
    mv = _dot(hb, w_ref[:, C_WIDTH:])
    k32_ref[0] = mk
    v32_ref[0] = mv
    k16_ref[0] = mk.astype(MXU_DTYPE)
    v16_ref[0] = mv.astype(MXU_DTYPE)


def _memory_kv(mem, gain, w16):
    bsz, n, d = mem.shape
    spec = pl.BlockSpec((1, n, C_WIDTH), lambda b: (b, 0, 0))
    return pl.pallas_call(
        _memory_kv_kernel,
        grid=(bsz,),
        in_specs=[pl.BlockSpec((1, n, d), lambda b: (b, 0, 0)), _resident((1, d)), _resident((d, 2 * C_WIDTH))],
        out_specs=[spec] * 4,
        out_shape=[jax.ShapeDtypeStruct((bsz, n, C_WIDTH), dt) for dt in (F32, F32, MXU_DTYPE, MXU_DTYPE)],
        compiler_params=pltpu.CompilerParams(dimension_semantics=("arbitrary",), vmem_limit_bytes=VMEM_LIMIT_BYTES),
        name="memory_kv",
    )(mem, gain.reshape(1, d), w16)


def _merge_ffn_kernel(x_ref, ya_ref, yb_ref, cq_ref, mk_ref, mv_ref, gt_ref, wa_ref, wb_ref, wc_ref, wo_ref,
                      g_mix_ref, g_pre_ref, g_post_ref, wu_ref, wd_ref, out_ref):
    nb, tm, d = x_ref.shape
    rows = nb * tm

    def flat(ref):
        return ref[...].reshape(rows, ref.shape[-1])

    per_row = []
    for i in range(nb):
        heads = []
        for h in range(C_HEADS):
            sl = slice(h * C_HEAD_DIM, (h + 1) * C_HEAD_DIM)
            s = _dot_nt(cq_ref[i, :, sl], mk_ref[i, :, sl])
            p = jnp.exp(s - jnp.max(s, axis=-1, keepdims=True))
            p = p / jnp.sum(p, axis=-1, keepdims=True)
            heads.append(_dot(p.astype(MXU_DTYPE), mv_ref[i, :, sl]).astype(MXU_DTYPE))
        per_row.append(jnp.concatenate(heads, axis=1))
    y_c = per_row[0] if nb == 1 else jnp.concatenate(per_row, axis=0)

    gt = flat(gt_ref)
    merged = (gt[:, 0:d].astype(F32) * _dot(flat(ya_ref), wa_ref[...])
              + gt[:, d:2 * d].astype(F32) * _dot(flat(yb_ref), wb_ref[...])
              + gt[:, 2 * d:3 * d].astype(F32) * _dot(y_c, wc_ref[...]))
    x1 = flat(x_ref) + _rms(_dot(merged.astype(MXU_DTYPE), wo_ref[...]), g_mix_ref[...])
    u = jnp.maximum(_dot(_rms(x1, g_pre_ref[...]).astype(MXU_DTYPE), wu_ref[...]), 0.0)
    down = _dot((u * u).astype(MXU_DTYPE), wd_ref[...])
    out_ref[...] = (x1 + _rms(down, g_post_ref[...])).reshape(nb, tm, d)


def _merge_ffn(x, ya, yb, cq, mk16, mv16, gates, weights, gains, row_tile):
    bsz, t, d = x.shape
    tm = min(row_tile, t)
    nb = max(1, min(bsz, row_tile // t))
    assert t % tm == 0 and bsz % nb == 0
    n_mem = mk16.shape[1]

    def rows(width):
        return pl.BlockSpec((nb, tm, width), lambda b, i: (b, i, 0))

    mem_spec = pl.BlockSpec((nb, n_mem, C_WIDTH), lambda b, i: (b, 0, 0))
    return pl.pallas_call(
        _merge_ffn_kernel,
        grid=(bsz // nb, t // tm),
        in_specs=[rows(d), rows(A_WIDTH), rows(B_WIDTH), rows(C_WIDTH), mem_spec, mem_spec, rows(N_BRANCHES * d)]
        + [_resident(w.shape) for w in weights[:4]] + [_resident((1, d))] * 3
        + [_resident(w.shape) for w in weights[4:]],
        out_specs=rows(d),
        out_shape=jax.ShapeDtypeStruct((bsz, t, d), F32),
        compiler_params=pltpu.CompilerParams(dimension_semantics=("arbitrary", "arbitrary"),
                                             vmem_limit_bytes=VMEM_LIMIT_BYTES),
        name="merge_ffn",
    )(x, ya, yb, cq, mk16, mv16, gates, *weights[:4], *[g.reshape(1, d) for g in gains], *weights[4:])


def _layer(x, pos, lower_bound, lp, *, cache, state0, mem16, row_tile, q_tile, key_tile):
    bsz, t, d = x.shape
    (aq, k32, v32, k16, v16, iq, ikw, ika, ikb, hq, hk, hlf, hv, hg, cq, gates, ik32) = _in_proj(
        x, pos, lp["pre_mix_gain"], lower_bound, lp["w_in"], min(row_tile, bsz * t))

    keys = [ika, ikb, k16, v16]
    if cache is not None:
        keys = [jnp.concatenate([c, kn], axis=1) for c, kn in zip(cache, keys)]
    n_keys = keys[0].shape[1]
    t_pad = -(-t // q_tile) * q_tile
    key_off = n_keys - t
    n_pad = -(-(key_off + t_pad) // key_tile) * key_tile
    ika, ikb, k16, v16 = [jnp.pad(kk, ((0, 0), (0, n_pad - n_keys), (0, 0))) for kk in keys]
    iw_t = jnp.swapaxes(ikw[..., IW_LANE:IW_LANE + IDX_HEADS], 1, 2)
    q_pad = ((0, 0), (0, t_pad - t), (0, 0))
    ya = _dsa(jnp.pad(aq, q_pad), jnp.pad(iq, q_pad), jnp.pad(iw_t, ((0, 0), (0, 0), (0, t_pad - t))),
              ika, ikb, k16, v16, key_off=key_off, topk=min(TOPK_MAX, n_keys // 4), q_tile=q_tile,
              key_tile=key_tile)[:, :t]

    yb, s_new = _hgrn(hq, hk, hlf, hv, hg, lp["hgrn_norm_gain"], state0)

    out = _merge_ffn(x, ya, yb, cq, mem16[0], mem16[1], gates,
                     [lp["w_out_a"], lp["w_out_b"], lp["w_out_c"], lp["w_out"], lp["w_up"], lp["w_down"]],
                     [lp["post_mix_gain"], lp["pre_ffn_gain"], lp["post_ffn_gain"]], row_tile=MERGE_ROWS)
    new_k = k32.reshape(bsz, t, A_KV_HEADS, HEAD_DIM)
    new_v = v32.reshape(bsz, t, A_KV_HEADS, HEAD_DIM)
    return out, (new_k, new_v, ik32, s_new)


PROJ_ROWS = 512
CUMSUM_ROWS = 256
MERGE_ROWS = 512
DSA_QUERIES = 512


def _tiles(t):
    return PROJ_ROWS, max(LANES, min(DSA_QUERIES, t))


def kernel(x_prompt, x_sample, cache_k, cache_v, cache_idx_k, state_hgrn, cache_mem_k, cache_mem_v, mem_prompt, w_in, w_mem_kv, mem_norm_gain, hgrn_lb_logits, hgrn_norm_gain, w_out_a, w_out_b, w_out_c, w_out, pre_mix_gain, post_mix_gain, pre_ffn_gain, post_ffn_gain, w_up, w_down):
    depth = w_in.shape[0]
    d = x_prompt.shape[-1]
    bp, tp, _ = x_prompt.shape
    bs, ts, _ = x_sample.shape
    past = cache_k.shape[2]
    lower_bounds = jnp.cumsum(jax.nn.softmax(hgrn_lb_logits.astype(F32), axis=0), axis=0)
    pos_p = jnp.arange(tp, dtype=I32)
    pos_s = past + jnp.arange(ts, dtype=I32)
    bf = lambda a: a.astype(MXU_DTYPE)

    xp, xs = x_prompt, x_sample
    outs_p, outs_s = [], []
    for l in range(depth):
        lp = {
            "w_in": _pack_w_in(w_in[l], d), "hgrn_norm_gain": hgrn_norm_gain[l],
            "w_out_a": bf(w_out_a[l]), "w_out_b": bf(w_out_b[l]), "w_out_c": bf(w_out_c[l]), "w_out": bf(w_out[l]),
            "pre_mix_gain": pre_mix_gain[l], "post_mix_gain": post_mix_gain[l], "pre_ffn_gain": pre_ffn_gain[l],
            "post_ffn_gain": post_ffn_gain[l], "w_up": bf(w_up[l]), "w_down": bf(w_down[l]),
        }
        mk32, mv32, mk16, mv16 = _memory_kv(mem_prompt, mem_norm_gain[l], bf(w_mem_kv[l]))
        n_mem = mem_prompt.shape[1]
        row_tile, q_tile = _tiles(tp)
        xp, st_p = _layer(xp, pos_p, lower_bounds[l], lp, cache=None,
                          state0=jnp.zeros((bp, B_HEADS, B_KEY_DIM, B_VAL_DIM), F32), mem16=(mk16, mv16),
                          row_tile=row_tile, q_tile=q_tile, key_tile=512)
        outs_p.append(st_p + (mk32.reshape(bp, n_mem, C_HEADS, C_HEAD_DIM), mv32.reshape(bp, n_mem, C_HEADS, C_HEAD_DIM)))

        cik = cache_idx_k[l]
        zeros_ik = jnp.zeros_like(cik)
        cache = (bf(jnp.concatenate([cik, zeros_ik], axis=-1)), bf(jnp.concatenate([zeros_ik, cik], axis=-1)),
                 bf(cache_k[l].reshape(bs, past, KV_WIDTH)), bf(cache_v[l].reshape(bs, past, KV_WIDTH)))
        mem16 = (bf(cache_mem_k[l].reshape(bs, -1, C_WIDTH)), bf(cache_mem_v[l].reshape(bs, -1, C_WIDTH)))
        row_tile, q_tile = _tiles(ts)
        xs, st_s = _layer(xs, pos_s, lower_bounds[l], lp, cache=cache, state0=state_hgrn[l].astype(F32),
                          mem16=mem16, row_tile=row_tile, q_tile=q_tile, key_tile=384)
        outs_s.append(st_s)

    new_k_p, new_v_p, new_ik_p, new_s_p, new_mk_p, new_mv_p = [jnp.stack(a) for a in zip(*outs_p)]
    new_k_s, new_v_s, new_ik_s, new_s_s = [jnp.stack(a) for a in zip(*outs_s)]
    return (xp, xs, new_k_p, new_v_p, new_ik_p, new_s_p, new_mk_p, new_mv_p, new_k_s, new_v_s, new_ik_s, new_s_s)
```
